```python
import math
import jax, jax.numpy as jnp
from jax import lax
import numpy as np

D_MODEL = 1024
BATCH = 8
SEQ = 4096
DEPTH = 2

CHUNK = 64
N_HEADS = 8
N_KV_HEADS = 2
HEAD_DIM = 64
Q_PER_KV = N_HEADS // N_KV_HEADS
WINDOW = 128
WIN_CHUNKS = WINDOW // CHUNK
ATT_W = N_HEADS * HEAD_DIM
KV_W = N_KV_HEADS * HEAD_DIM
SSM_W = 512
SSM_GROUP = 16
SSM_GROUPS = SSM_W // SSM_GROUP
SSM_STATE = 64
POOL_W = 512
POOL_WINDOWS = (2, 4, 8, 16)
POOL_GROUPS = len(POOL_WINDOWS)
POOL_GW = POOL_W // POOL_GROUPS
N_BRANCH = 3
SPLIT_SIZES = (ATT_W, KV_W, KV_W, SSM_W, POOL_W, ATT_W, SSM_W, POOL_W, N_BRANCH * D_MODEL)
IN_W = sum(SPLIT_SIZES)
EPS = 1e-6
NEG_INF = -1e30

kernel_name = "hybrid_gated_swa_s5_pool_adaln"


def rmsnorm(x, g):
    xf = x.astype(jnp.float32)
    y = xf * lax.rsqrt(jnp.mean(xf * xf, axis=-1, keepdims=True) + EPS)
    return (y * g.astype(jnp.float32)).astype(x.dtype)


def alibi_slopes(n):
    return jnp.asarray([2.0 ** (-8.0 * (h + 1) / n) for h in range(n)], dtype=jnp.float32)


def window_attention(q, k, v, sinks):
    b, l = q.shape[:2]
    nc = l // CHUNK
    pad = WIN_CHUNKS * CHUNK
    nk = (WIN_CHUNKS + 1) * CHUNK
    kp = jnp.pad(k, ((0, 0), (pad, 0), (0, 0), (0, 0))).reshape(b, nc + WIN_CHUNKS, CHUNK, N_KV_HEADS, HEAD_DIM)
    vp = jnp.pad(v, ((0, 0), (pad, 0), (0, 0), (0, 0))).reshape(b, nc + WIN_CHUNKS, CHUNK, N_KV_HEADS, HEAD_DIM)
    kb = jnp.concatenate([kp[:, j:j + nc] for j in range(WIN_CHUNKS + 1)], axis=2)
    vb = jnp.concatenate([vp[:, j:j + nc] for j in range(WIN_CHUNKS + 1)], axis=2)
    qb = q.reshape(b, nc, CHUNK, N_KV_HEADS, Q_PER_KV, HEAD_DIM)
    s = jnp.einsum('bcqkgd,bcskd->bckgqs', qb, kb).astype(jnp.float32) * (1.0 / math.sqrt(HEAD_DIM))
    qi = jnp.arange(CHUNK)[:, None]
    kj = jnp.arange(nk)[None, :]
    dist = jnp.abs(qi + pad - kj).astype(jnp.float32)
    slopes = alibi_slopes(N_HEADS).reshape(N_KV_HEADS, Q_PER_KV)
    s = s - slopes[:, :, None, None] * dist[None, None]
    valid = (jnp.arange(nc)[:, None] * CHUNK + jnp.arange(nk)[None, :]) >= pad
    s = jnp.where(valid[None, :, None, None, None, :], s, NEG_INF)
    sink = jnp.broadcast_to(sinks.astype(jnp.float32).reshape(N_KV_HEADS, Q_PER_KV)[None, None, :, :, None, None],
                            s.shape[:-1] + (1,))
    p = jax.nn.softmax(jnp.concatenate([s, sink], axis=-1), axis=-1)[..., :-1]
    o = jnp.einsum('bckgqs,bcskd->bcqkgd', p.astype(v.dtype), vb)
    return o.reshape(b, l, ATT_W)


def s5_layer(u, a_re, a_im, log_dt, b_re, b_im, c_re, c_im, d_skip, w_glu, b_glu):
    b, l = u.shape[:2]
    uf = u.astype(jnp.float32)
    lam = lax.complex(a_re.astype(jnp.float32), a_im.astype(jnp.float32))
    dt = jnp.exp(log_dt.astype(jnp.float32))[:, None]
    lam_bar = jnp.exp(lam * dt)
    bmat = lax.complex(b_re.astype(jnp.float32), b_im.astype(jnp.float32))
    b_bar = ((lam_bar - 1.0) / lam)[..., None] * bmat
    ug = uf.reshape(b, l, SSM_GROUPS, SSM_GROUP).astype(jnp.complex64)
    bu = jnp.einsum('gpc,blgc->blgp', b_bar, ug)
    a = jnp.broadcast_to(lam_bar, bu.shape)

    def combine(e1, e2):
        a1, x1 = e1
        a2, x2 = e2
        return a1 * a2, a2 * x1 + x2

    _, states = lax.associative_scan(combine, (a, bu), axis=1)
    cmat = lax.complex(c_re.astype(jnp.float32), c_im.astype(jnp.float32))
    y = jnp.real(jnp.einsum('gcp,blgp->blgc', cmat, states)).reshape(b, l, SSM_W)
    y = y + d_skip.astype(jnp.float32) * uf
    y = jax.nn.gelu(y)
    y = y * jax.nn.sigmoid(y @ w_glu.astype(jnp.float32) + b_glu.astype(jnp.float32))
    return y.astype(u.dtype)


def multiscale_pool(u, w_pool, pool_scale):
    b, l = u.shape[:2]
    uf = u.astype(jnp.float32).reshape(b, l, POOL_GROUPS, POOL_GW)
    cs = jnp.concatenate([jnp.zeros((b, 1, POOL_GROUPS, POOL_GW), jnp.float32), jnp.cumsum(uf, axis=1)], axis=1)
    t = jnp.arange(l)
    pooled = []
    for gi, w in enumerate(POOL_WINDOWS):
        csp = jnp.pad(cs[:, :, gi], ((0, 0), (w - 1, 0), (0, 0)))
        ssum = csp[:, w:w + l] - csp[:, :l]
        cnt = jnp.minimum(t + 1, w).astype(jnp.float32)[None, :, None]
        pooled.append(ssum / cnt - uf[:, :, gi])
    pooled = jnp.stack(pooled, axis=2)
    y = jnp.einsum('blgi,gio->blgo', pooled, w_pool.astype(jnp.float32)).reshape(b, l, POOL_W)
    return (y * pool_scale.astype(jnp.float32)).astype(u.dtype)


def setup_inputs(seed: int = 0) -> dict:
    key = jax.random.key(seed)
    ks = jax.random.split(key, 32)
    f32 = jnp.float32
    nrm = lambda k, shape, s: jax.random.normal(k, shape, f32) * s
    D = D_MODEL
    n_idx = jnp.arange(SSM_STATE, dtype=f32)
    a_re = -0.5 * (1.0 + 0.02 * jax.random.normal(ks[6], (DEPTH, SSM_GROUPS, SSM_STATE), f32))
    a_im = math.pi * n_idx[None, None, :] + 0.02 * jax.random.normal(ks[7], (DEPTH, SSM_GROUPS, SSM_STATE), f32)
    log_dt = jax.random.uniform(ks[8], (DEPTH, SSM_GROUPS), f32, math.log(1e-3), math.log(1e-1))
    return {
        "x": nrm(ks[0], (BATCH, SEQ, D), 1.0),
        "c": nrm(ks[1], (BATCH, D), 1.0),
        "norm_g": 1.0 + nrm(ks[2], (DEPTH, D), 0.02),
        "w_ada": nrm(ks[3], (DEPTH, D, 3 * D), 0.5 * D ** -0.5),
        "b_ada": nrm(ks[4], (DEPTH, 3 * D), 0.02),
        "w_in": nrm(ks[5], (DEPTH, D, IN_W), D ** -0.5),
        "attn_sinks": nrm(ks[9], (DEPTH, N_HEADS), 0.5),
        "ssm_a_re": a_re,
        "ssm_a_im": a_im,
        "ssm_log_dt": log_dt,
        "ssm_b_re": nrm(ks[10], (DEPTH, SSM_GROUPS, SSM_STATE, SSM_GROUP), (2 * SSM_GROUP) ** -0.5),
        "ssm_b_im": nrm(ks[11], (DEPTH, SSM_GROUPS, SSM_STATE, SSM_GROUP), (2 * SSM_GROUP) ** -0.5),
        "ssm_c_re": nrm(ks[12], (DEPTH, SSM_GROUPS, SSM_GROUP, SSM_STATE), (2 * SSM_STATE) ** -0.5),
        "ssm_c_im": nrm(ks[13], (DEPTH, SSM_GROUPS, SSM_GROUP, SSM_STATE), (2 * SSM_STATE) ** -0.5),
        "ssm_d": nrm(ks[14], (DEPTH, SSM_W), 1.0),
        "w_glu": nrm(ks[15], (DEPTH, SSM_W, SSM_W), SSM_W ** -0.5),
        "b_glu": nrm(ks[16], (DEPTH, SSM_W), 0.02),
        "w_pool": nrm(ks[17], (DEPTH, POOL_GROUPS, POOL_GW, POOL_GW), POOL_GW ** -0.5),
        "pool_scale": 1.0 + nrm(ks[18], (DEPTH, POOL_W), 0.1),
        "w_br_att": nrm(ks[19], (DEPTH, ATT_W, D), ATT_W ** -0.5),
        "w_br_ssm": nrm(ks[20], (DEPTH, SSM_W, D), SSM_W ** -0.5),
        "w_br_pool": nrm(ks[21], (DEPTH, POOL_W, D), POOL_W ** -0.5),
        "w_out": nrm(ks[22], (DEPTH, D, D), D ** -0.5),
        "final_g": 1.0 + nrm(ks[23], (D,), 0.02),
    }


def reference(x, c, norm_g, w_ada, b_ada, w_in, attn_sinks, ssm_a_re, ssm_a_im, ssm_log_dt,
              ssm_b_re, ssm_b_im, ssm_c_re, ssm_c_im, ssm_d, w_glu, b_glu, w_pool, pool_scale,
              w_br_att, w_br_ssm, w_br_pool, w_out, final_g):
    b, l, _ = x.shape
    split_idx = [int(v) for v in np.cumsum(SPLIT_SIZES)[:-1]]
    c_act = jax.nn.silu(c)
    for li in range(DEPTH):
        mod = c_act @ w_ada[li] + b_ada[li]
        shift, scale, gate = jnp.split(mod, 3, axis=-1)
        h = rmsnorm(x, norm_g[li]) * (1.0 + scale[:, None, :]) + shift[:, None, :]
        proj = h @ w_in[li]
        q, k, v, u_ssm, u_pool, z_att, z_ssm, z_pool, g_logits = jnp.split(proj, split_idx, axis=-1)
        y_att = window_attention(q.reshape(b, l, N_HEADS, HEAD_DIM),
                                 k.reshape(b, l, N_KV_HEADS, HEAD_DIM),
                                 v.reshape(b, l, N_KV_HEADS, HEAD_DIM), attn_sinks[li]) * jax.nn.silu(z_att)
        y_ssm = s5_layer(u_ssm, ssm_a_re[li], ssm_a_im[li], ssm_log_dt[li], ssm_b_re[li], ssm_b_im[li],
                         ssm_c_re[li], ssm_c_im[li], ssm_d[li], w_glu[li], b_glu[li]) * jax.nn.silu(z_ssm)
        y_pool = multiscale_pool(u_pool, w_pool[li], pool_scale[li]) * jax.nn.silu(z_pool)
        gates = jax.nn.sigmoid(g_logits).reshape(b, l, N_BRANCH, D_MODEL)
        merged = (gates[:, :, 0] * (y_att @ w_br_att[li])
                  + gates[:, :, 1] * (y_ssm @ w_br_ssm[li])
                  + gates[:, :, 2] * (y_pool @ w_br_pool[li]))
        x = x + gate[:, None, :] * (merged @ w_out[li])
    return rmsnorm(x, final_g)
```

```python
import functools
import math

import numpy as np
import jax
import jax.numpy as jnp
from jax import lax
from jax.experimental import pallas as pl
from jax.experimental.pallas import tpu as pltpu

F32 = jnp.float32
BF16 = jnp.bfloat16

D_MODEL = 1024
CHUNK = 64
N_HEADS = 8
N_KV_HEADS = 2
HEAD_DIM = 64
WIN_CHUNKS = 2
NK = (WIN_CHUNKS + 1) * CHUNK
ATT_W = N_HEADS * HEAD_DIM
KV_W = N_KV_HEADS * HEAD_DIM
SSM_W = 512
SSM_GROUP = 16
SSM_GROUPS = 32
SSM_STATE = 64
N_STATE = SSM_GROUPS * SSM_STATE
POOL_W = 512
POOL_WINDOWS = (2, 4, 8, 16)
POOL_GW = 128
POOL_HALO = 16
EPS = 1e-6
NEG_INF = -1e30
LANES = 128
SUBLANES = 8
VMEM_LIMIT = 58 * 1024 * 1024


def _sigmoid(x):
    return 0.5 * jnp.tanh(0.5 * x) + 0.5


def _silu(x):
    return x * _sigmoid(x)


def _gelu_tanh(x):
    return 0.5 * x * (1.0 + jnp.tanh(math.sqrt(2.0 / math.pi) * (x + 0.044715 * (x * x * x))))


def _const_spec(shape):
    nd = len(shape)
    return pl.BlockSpec(shape, lambda *_: (0,) * nd, pipeline_mode=pl.Buffered(1))


def _mod_kernel(c_ref, w_ref, b_ref, o_ref):
    ca = _silu(c_ref[...])
    o_ref[0] = jnp.dot(ca, w_ref[0], preferred_element_type=F32, precision=lax.Precision.HIGHEST) + b_ref[0]


def _modulation(c, w_ada, b_ada):
    depth, d, n3 = w_ada.shape
    b = c.shape[0]
    nb = n3 // d
    return pl.pallas_call(
        _mod_kernel,
        grid=(depth, nb),
        in_specs=[pl.BlockSpec((b, d), lambda l, j: (0, 0)),
                  pl.BlockSpec((1, d, d), lambda l, j: (l, 0, j)),
                  pl.BlockSpec((1, 1, d), lambda l, j: (l, 0, j))],
        out_specs=pl.BlockSpec((1, b, d), lambda l, j: (l, 0, j)),
        out_shape=jax.ShapeDtypeStruct((depth, b, n3), F32),
        name="adaln_mod",
    )(c, w_ada, b_ada.reshape(depth, 1, n3))


def _ssm_prep_kernel(are_ref, aim_ref, ldt_ref, lre_ref, lim_ref, cre_ref, cim_ref):
    a_re = are_ref[0]
    a_im = aim_ref[0]
    dt = jnp.exp(ldt_ref[0])
    mag = jnp.exp(a_re * dt)
    ang = a_im * dt
    l_re = mag * jnp.cos(ang)
    l_im = mag * jnp.sin(ang)
    n_re = l_re - 1.0
    den = a_re * a_re + a_im * a_im
    lre_ref[0] = l_re
    lim_ref[0] = l_im
    cre_ref[0] = (n_re * a_re + l_im * a_im) / den
    cim_ref[0] = (l_im * a_re - n_re * a_im) / den


def _ssm_prep(a_re, a_im, log_dt):
    depth, g, p = a_re.shape
    spec = pl.BlockSpec((1, g, p), lambda l: (l, 0, 0))
    out = jax.ShapeDtypeStruct((depth, g, p), F32)
    return pl.pallas_call(
        _ssm_prep_kernel,
        grid=(depth,),
        in_specs=[spec, spec, pl.BlockSpec((1, g, 1), lambda l: (l, 0, 0))],
        out_specs=[spec] * 4,
        out_shape=[out] * 4,
        name="ssm_discretise",
    )(a_re, a_im, log_dt.reshape(depth, g, 1))


def _mixer_kernel(sink_ref, x_ref, mod_ref, ng_ref, watt_ref, wssm_ref, wpl_ref, perm_ref, permt_ref,
                  bmat_ref, cre_ref, cim_ref, lam_ref, dskip_ref, wglu_ref, bglu_ref, wpool_ref, pscale_ref,
                  ya_ref, ys_ref, yp_ref,
                  hb_ref, att_ref, yatt_ref, kring_ref, vring_ref, bu_ref, st_ref, pext_ref,
                  *, nb):
    i = pl.program_id(0)
    tl = CHUNK
    m = nb * tl

    @pl.when(i == 0)
    def _init():
        kring_ref[...] = jnp.zeros_like(kring_ref)
        vring_ref[...] = jnp.zeros_like(vring_ref)
        st_ref[...] = jnp.zeros_like(st_ref)
        pext_ref[:, 0:POOL_HALO, :] = jnp.zeros((nb, POOL_HALO, POOL_W), F32)

    x = x_ref[...]
    ms = jnp.mean(x * x, axis=-1, keepdims=True)
    h = x * lax.rsqrt(ms + EPS) * ng_ref[...]
    h = h * (1.0 + mod_ref[1]) + mod_ref[0]
    hb_ref[...] = h.reshape(m, D_MODEL).astype(BF16)

    att_ref[...] = jnp.dot(hb_ref[...], watt_ref[...], preferred_element_type=F32)
    lane = lax.broadcasted_iota(jnp.int32, (m, LANES), 1)
    lo = lane < HEAD_DIM
    zero = jnp.zeros((m, LANES), F32)
    for ring, c0 in ((kring_ref, ATT_W), (vring_ref, ATT_W + KV_W)):
        new = att_ref[:, c0:c0 + KV_W]
        swp = pltpu.roll(new, HEAD_DIM, axis=1)
        variants = (jnp.where(lo, new, zero),
                    jnp.where(lo, zero, swp),
                    jnp.where(lo, swp, zero),
                    jnp.where(lo, zero, new))
        for var in range(4):
            for b in range(nb):
                ring[var, b, 0:NK - CHUNK, :] = ring[var, b, CHUNK:NK, :]
            ring[var, :, NK - CHUNK:NK, :] = variants[var].reshape(nb, tl, LANES).astype(BF16)

    qi = lax.broadcasted_iota(jnp.int32, (CHUNK, NK), 0)
    kj = lax.broadcasted_iota(jnp.int32, (CHUNK, NK), 1)
    dist = jnp.abs(qi + WIN_CHUNKS * CHUNK - kj).astype(F32)
    valid = (i * CHUNK + kj) >= WIN_CHUNKS * CHUNK

    def att_batch(b, carry):
        r0 = pl.multiple_of(b * tl, tl)
        for pr in range(N_HEADS // 2):
            qp = att_ref[pl.ds(r0, tl), pr * LANES:(pr + 1) * LANES].astype(BF16)
            acc = None
            for e in range(2):
                head = 2 * pr + e
                var = 2 * (head // (N_HEADS // N_KV_HEADS)) + e
                s = lax.dot_general(qp, kring_ref[var, b], (((1,), (1,)), ((), ())),
                                    preferred_element_type=F32)
                s = s - (2.0 ** (-(head + 1))) * dist
                s = jnp.where(valid, s, NEG_INF)
                sink = sink_ref[head]
                mx = jnp.maximum(jnp.max(s, axis=-1, keepdims=True), sink)
                p = jnp.exp(s - mx)
                den = jnp.sum(p, axis=-1, keepdims=True) + jnp.exp(sink - mx)
                o = jnp.dot(p.astype(BF16), vring_ref[var, b], preferred_element_type=F32)
                o = o / den
                acc = o if acc is None else acc + o
            yatt_ref[pl.ds(r0, tl), pr * LANES:(pr + 1) * LANES] = acc
        return carry

    lax.fori_loop(0, nb, att_batch, 0)
    z = att_ref[:, ATT_W + 2 * KV_W:]
    ya_ref[...] = (yatt_ref[...] * _silu(z)).reshape(nb, tl, ATT_W).astype(BF16)

    us = jnp.dot(hb_ref[...], wssm_ref[...], preferred_element_type=F32)
    u = us[:, :SSM_W]
    z = us[:, SSM_W:]
    u_t = jnp.dot(perm_ref[...], u.astype(BF16), preferred_element_type=F32).astype(BF16)
    half = N_STATE // 4
    for r in range(4):
        res = jnp.dot(u_t[:, r * LANES:(r + 1) * LANES], bmat_ref[r], preferred_element_type=F32)
        bu_ref[:, r * half:(r + 1) * half] = res[:, :half]
        bu_ref[:, N_STATE + r * half:N_STATE + (r + 1) * half] = res[:, half:]

    cw = 512
    for cb in range(N_STATE // cw):
        c0 = cb * cw
        a_re = jnp.broadcast_to(lam_ref[0:1, c0:c0 + cw], (nb, cw))
        a_im = jnp.broadcast_to(lam_ref[1:2, c0:c0 + cw], (nb, cw))

        def step(t, carry, c0=c0, a_re=a_re, a_im=a_im):
            s_re, s_im = carry
            r0 = pl.multiple_of(t * nb, nb)
            n_re = a_re * s_re - a_im * s_im + bu_ref[pl.ds(r0, nb), c0:c0 + cw]
            n_im = a_re * s_im + a_im * s_re + bu_ref[pl.ds(r0, nb), N_STATE + c0:N_STATE + c0 + cw]
            bu_ref[pl.ds(r0, nb), c0:c0 + cw] = n_re
            bu_ref[pl.ds(r0, nb), N_STATE + c0:N_STATE + c0 + cw] = n_im
            return n_re, n_im

        s_re, s_im = lax.fori_loop(0, tl, step, (st_ref[:, c0:c0 + cw], st_ref[:, N_STATE + c0:N_STATE + c0 + cw]),
                                   unroll=8)
        st_ref[:, c0:c0 + cw] = s_re
        st_ref[:, N_STATE + c0:N_STATE + c0 + cw] = s_im

    ycols = []
    for r in range(4):
        x_re = bu_ref[:, r * half:(r + 1) * half].astype(BF16)
        x_im = bu_ref[:, N_STATE + r * half:N_STATE + (r + 1) * half].astype(BF16)
        ycols.append(jnp.dot(x_re, cre_ref[r], preferred_element_type=F32)
                     + jnp.dot(x_im, cim_ref[r], preferred_element_type=F32))
    y_t = jnp.concatenate(ycols, axis=1)
    y = jnp.dot(permt_ref[...], y_t.astype(BF16), preferred_element_type=F32)
    y = _gelu_tanh(y + dskip_ref[...] * u)
    gl = jnp.dot(y.astype(BF16), wglu_ref[...], preferred_element_type=F32) + bglu_ref[...]
    y = y * _sigmoid(gl)
    ys_ref[...] = (y * _silu(z)).reshape(nb, tl, SSM_W).astype(BF16)

    up = jnp.dot(hb_ref[...], wpl_ref[...], preferred_element_type=F32)
    z = up[:, POOL_W:]
    pext_ref[:, POOL_HALO:POOL_HALO + tl, :] = up[:, :POOL_W].reshape(nb, tl, POOL_W)
    tpos = i * tl + lax.broadcasted_iota(jnp.int32, (1, tl, 1), 1)
    groups = []
    for gi, w in enumerate(POOL_WINDOWS):
        cs = slice(gi * POOL_GW, (gi + 1) * POOL_GW)
        tok = pext_ref[:, POOL_HALO:POOL_HALO + tl, cs]
        ssum = tok
        for k in range(1, w):
            ssum = ssum + pext_ref[:, POOL_HALO - k:POOL_HALO - k + tl, cs]
        cnt = jnp.minimum(tpos + 1, w).astype(F32)
        pooled = ssum / cnt - tok
        groups.append(jnp.dot(pooled.reshape(m, POOL_GW).astype(BF16), wpool_ref[gi], preferred_element_type=F32))
    y = jnp.concatenate(groups, axis=1) * pscale_ref[...]
    yp_ref[...] = (y * _silu(z)).reshape(nb, tl, POOL_W).astype(BF16)
    pext_ref[:, 0:POOL_HALO, :] = pext_ref[:, tl:tl + POOL_HALO, :]


def _mixers(x, mod, norm_g, sinks, wts):
    nb, seq, d = x.shape
    tl = CHUNK
    m = nb * tl
    consts = (wts["w_att"], wts["w_ssm"], wts["w_pl"], wts["perm"], wts["perm_t"], wts["bmat"], wts["cre"],
              wts["cim"], wts["lam"], wts["dskip"], wts["w_glu"], wts["b_glu"], wts["w_pool"], wts["pscale"])
    out_spec = pl.BlockSpec((nb, tl, SSM_W), lambda i: (0, i, 0))
    out_shape = jax.ShapeDtypeStruct((nb, seq, SSM_W), BF16)
    return pl.pallas_call(
        functools.partial(_mixer_kernel, nb=nb),
        grid=(seq // tl,),
        in_specs=[pl.BlockSpec(memory_space=pltpu.SMEM),
                  pl.BlockSpec((nb, tl, d), lambda i: (0, i, 0)),
                  _const_spec(mod.shape),
                  _const_spec(norm_g.shape)] + [_const_spec(c.shape) for c in consts],
        out_specs=[out_spec] * 3,
        out_shape=[out_shape] * 3,
        scratch_shapes=[pltpu.VMEM((m, d), BF16),
                        pltpu.VMEM((m, 2 * ATT_W + 2 * KV_W), F32),
                        pltpu.VMEM((m, ATT_W), F32),
                        pltpu.VMEM((4, nb, NK, LANES), BF16),
                        pltpu.VMEM((4, nb, NK, LANES), BF16),
                        pltpu.VMEM((m, 2 * N_STATE), F32),
                        pltpu.VMEM((nb, 2 * N_STATE), F32),
                        pltpu.VMEM((nb, POOL_HALO + tl, POOL_W), F32)],
        compiler_params=pltpu.CompilerParams(dimension_semantics=("arbitrary",), vmem_limit_bytes=VMEM_LIMIT),
        name="mixers",
    )(sinks, x, mod, norm_g, *consts)


def _merge_kernel(x_ref, mod_ref, ng_ref, ya_ref, ys_ref, yp_ref, wg_ref, wba_ref, wbs_ref, wbp_ref, wout_ref,
                  fg_ref, o_ref, *, final):
    x = x_ref[0]
    ms = jnp.mean(x * x, axis=-1, keepdims=True)
    h = x * lax.rsqrt(ms + EPS) * ng_ref[...]
    hb = (h * (1.0 + mod_ref[1, 0]) + mod_ref[0, 0]).astype(BF16)
    merged = None
    for j, (y_ref, wb_ref) in enumerate(((ya_ref, wba_ref), (ys_ref, wbs_ref), (yp_ref, wbp_ref))):
        g = jnp.dot(hb, wg_ref[:, j * D_MODEL:(j + 1) * D_MODEL], preferred_element_type=F32)
        p = jnp.dot(y_ref[0], wb_ref[...], preferred_element_type=F32)
        term = _sigmoid(g) * p
        merged = term if merged is None else merged + term
    out = jnp.dot(merged.astype(BF16), wout_ref[...], preferred_element_type=F32)
    xn = x + mod_ref[2, 0] * out
    if final:
        ms = jnp.mean(xn * xn, axis=-1, keepdims=True)
        xn = xn * lax.rsqrt(ms + EPS) * fg_ref[...]
    o_ref[0] = xn


def _merge(x, mod, norm_g, ya, ys, yp, wts, final_g, final):
    nb, seq, d = x.shape
    tm = 512
    row = lambda w: pl.BlockSpec((1, tm, w), lambda b, i: (b, i, 0))
    return pl.pallas_call(
        functools.partial(_merge_kernel, final=final),
        grid=(nb, seq // tm),
        in_specs=[row(d),
                  pl.BlockSpec((3, 1, 1, d), lambda b, i: (0, b, 0, 0)),
                  _const_spec(norm_g.shape),
                  row(ATT_W), row(SSM_W), row(POOL_W),
                  _const_spec(wts["w_g"].shape), _const_spec(wts["w_br_att"].shape),
                  _const_spec(wts["w_br_ssm"].shape), _const_spec(wts["w_br_pool"].shape),
                  _const_spec(wts["w_out"].shape), _const_spec(final_g.shape)],
        out_specs=row(d),
        out_shape=jax.ShapeDtypeStruct(x.shape, x.dtype),
        compiler_params=pltpu.CompilerParams(dimension_semantics=("arbitrary", "arbitrary"),
                                             vmem_limit_bytes=VMEM_LIMIT),
        name="merge",
    )(x, mod, norm_g, ya, ys, yp, wts["w_g"], wts["w_br_att"], wts["w_br_ssm"], wts["w_br_pool"], wts["w_out"],
      final_g)


def _block_diag(blocks):
    n, r, c = blocks.shape
    eye = jnp.eye(n, dtype=blocks.dtype)
    return (blocks[:, :, None, :] * eye[:, None, :, None]).reshape(n * r, n * c)


def _perm_matrix(nb, tl):
    p = np.zeros((nb * tl, nb * tl), np.float32)
    b, t = np.meshgrid(np.arange(nb), np.arange(tl), indexing="ij")
    p[(t * nb + b).ravel(), (b * tl + t).ravel()] = 1.0
    return p


def _layer_weights(li, nb, w_in, lam_re, lam_im, coef_re, coef_im, ssm_b_re, ssm_b_im, ssm_c_re, ssm_c_im, ssm_d,
                   w_glu, b_glu, w_pool, pool_scale, w_br_att, w_br_ssm, w_br_pool, w_out):
    w = w_in[li]
    o_k, o_v, o_us, o_up, o_za, o_zs, o_zp, o_g = 512, 640, 768, 1280, 1792, 2304, 2816, 3328
    wts = {}
    wts["w_att"] = jnp.concatenate([w[:, :o_k] * (1.0 / math.sqrt(HEAD_DIM)), w[:, o_k:o_us], w[:, o_za:o_zs]],
                                   axis=1).astype(BF16)
    wts["w_ssm"] = jnp.concatenate([w[:, o_us:o_up], w[:, o_zs:o_zp]], axis=1).astype(BF16)
    wts["w_pl"] = jnp.concatenate([w[:, o_up:o_za], w[:, o_zp:o_g]], axis=1).astype(BF16)
    wts["w_g"] = w[:, o_g:].astype(BF16)
    perm = _perm_matrix(nb, CHUNK)
    wts["perm"] = jnp.asarray(perm, BF16)
    wts["perm_t"] = jnp.asarray(perm.T, BF16)
    cr = coef_re[li][:, :, None]
    ci = coef_im[li][:, :, None]
    bb_re = cr * ssm_b_re[li] - ci * ssm_b_im[li]
    bb_im = cr * ssm_b_im[li] + ci * ssm_b_re[li]
    gpb = LANES // SSM_GROUP
    nblk = SSM_GROUPS // gpb
    bre = jnp.swapaxes(bb_re, 1, 2).reshape(nblk, gpb, SSM_GROUP, SSM_STATE)
    bim = jnp.swapaxes(bb_im, 1, 2).reshape(nblk, gpb, SSM_GROUP, SSM_STATE)
    wts["bmat"] = jnp.stack([jnp.concatenate([_block_diag(bre[r]), _block_diag(bim[r])], axis=1)
                             for r in range(nblk)]).astype(BF16)
    cre = jnp.swapaxes(ssm_c_re[li], 1, 2).reshape(nblk, gpb, SSM_STATE, SSM_GROUP)
    cim = jnp.swapaxes(-ssm_c_im[li], 1, 2).reshape(nblk, gpb, SSM_STATE, SSM_GROUP)
    wts["cre"] = jnp.stack([_block_diag(cre[r]) for r in range(nblk)]).astype(BF16)
    wts["cim"] = jnp.stack([_block_diag(cim[r]) for r in range(nblk)]).astype(BF16)
    wts["lam"] = jnp.stack([lam_re[li].reshape(N_STATE), lam_im[li].reshape(N_STATE)])
    wts["dskip"] = ssm_d[li].reshape(1, SSM_W)
    wts["w_glu"] = w_glu[li].astype(BF16)
    wts["b_glu"] = b_glu[li].reshape(1, SSM_W)
    wts["w_pool"] = w_pool[li].astype(BF16)
    wts["pscale"] = pool_scale[li].reshape(1, POOL_W)
    wts["w_br_att"] = w_br_att[li].astype(BF16)
    wts["w_br_ssm"] = w_br_ssm[li].astype(BF16)
    wts["w_br_pool"] = w_br_pool[li].astype(BF16)
    wts["w_out"] = w_out[li].astype(BF16)
    return wts


def kernel(x, c, norm_g, w_ada, b_ada, w_in, attn_sinks, ssm_a_re, ssm_a_im, ssm_log_dt, ssm_b_re, ssm_b_im,
           ssm_c_re, ssm_c_im, ssm_d, w_glu, b_glu, w_pool, pool_scale, w_br_att, w_br_ssm, w_br_pool, w_out,
           final_g):
    nb, seq, d = x.shape
    depth = w_in.shape[0]
    assert d == D_MODEL and nb == SUBLANES and seq % 512 == 0
    mod_all = _modulation(c, w_ada, b_ada)
    lam_re, lam_im, coef_re, coef_im = _ssm_prep(ssm_a_re, ssm_a_im, ssm_log_dt)
    fg = final_g.reshape(1, d)
    for li in range(depth):
        mod = jnp.transpose(mod_all[li].reshape(nb, 3, d), (1, 0, 2)).reshape(3, nb, 1, d)
        ng = norm_g[li].reshape(1, d)
        wts = _layer_weights(li, nb, w_in, lam_re, lam_im, coef_re, coef_im, ssm_b_re, ssm_b_im, ssm_c_re,
                             ssm_c_im, ssm_d, w_glu, b_glu, w_pool, pool_scale, w_br_att, w_br_ssm, w_br_pool,
                             w_out)
        ya, ys, yp = _mixers(x, mod, ng, attn_sinks[li], wts)
        x = _merge(x, mod, ng, ya, ys, yp, wts, fg, final=(li == depth - 1))
    return x
```

```python
import functools
import math

import numpy as np
import jax
import jax.numpy as jnp
from jax import lax
from jax.experimental import pallas as pl
from jax.experimental.pallas import tpu as pltpu

F32 = jnp.float32
BF16 = jnp.bfloat16

D_MODEL = 1024
CHUNK = 64
N_HEADS = 8
N_KV_HEADS = 2
HEAD_DIM = 64
WIN_CHUNKS = 2
NK = (WIN_CHUNKS + 1) * CHUNK
ATT_W = N_HEADS * HEAD_DIM
KV_W = N_KV_HEADS * HEAD_DIM
SSM_W = 512
SSM_GROUP = 16
SSM_GROUPS = 32
SSM_STATE = 64
N_STATE = SSM_GROUPS * SSM_STATE
POOL_W = 512
POOL_WINDOWS = (2, 4, 8, 16)
POOL_GW = 128
POOL_HALO = 16
EPS = 1e-6
NEG_INF = -1e30
LANES = 128
SUBLANES = 8
VMEM_LIMIT = 58 * 1024 * 1024


def _sigmoid(x):
    return 0.5 * jnp.tanh(0.5 * x) + 0.5


def _silu(x):
    return x * _sigmoid(x)


def _gelu_tanh(x):
    return 0.5 * x * (1.0 + jnp.tanh(math.sqrt(2.0 / math.pi) * (x + 0.044715 * (x * x * x))))


def _const_spec(shape):
    nd = len(shape)
    return pl.BlockSpec(shape, lambda *_: (0,) * nd, pipeline_mode=pl.Buffered(1))


def _mod_kernel(c_ref, w_ref, b_ref, o_ref):
    ca = _silu(c_ref[...])
    o_ref[0] = jnp.dot(ca, w_ref[0], preferred_element_type=F32, precision=lax.Precision.HIGHEST) + b_ref[0]


def _modulation(c, w_ada, b_ada):
    depth, d, n3 = w_ada.shape
    b = c.shape[0]
    nb = n3 // d
    return pl.pallas_call(
        _mod_kernel,
        grid=(depth, nb),
        in_specs=[pl.BlockSpec((b, d), lambda l, j: (0, 0)),
                  pl.BlockSpec((1, d, d), lambda l, j: (l, 0, j)),
                  pl.BlockSpec((1, 1, d), lambda l, j: (l, 0, j))],
        out_specs=pl.BlockSpec((1, b, d), lambda l, j: (l, 0, j)),
        out_shape=jax.ShapeDtypeStruct((depth, b, n3), F32),
        name="adaln_mod",
    )(c, w_ada, b_ada.reshape(depth, 1, n3))


def _ssm_prep_kernel(are_ref, aim_ref, ldt_ref, lre_ref, lim_ref, cre_ref, cim_ref):
    a_re = are_ref[0]
    a_im = aim_ref[0]
    dt = jnp.exp(ldt_ref[0])
    mag = jnp.exp(a_re * dt)
    ang = a_im * dt
    l_re = mag * jnp.cos(ang)
    l_im = mag * jnp.sin(ang)
    n_re = l_re - 1.0
    den = a_re * a_re + a_im * a_im
    lre_ref[0] = l_re
    lim_ref[0] = l_im
    cre_ref[0] = (n_re * a_re + l_im * a_im) / den
    cim_ref[0] = (l_im * a_re - n_re * a_im) / den


def _ssm_prep(a_re, a_im, log_dt):
    depth, g, p = a_re.shape
    spec = pl.BlockSpec((1, g, p), lambda l: (l, 0, 0))
    out = jax.ShapeDtypeStruct((depth, g, p), F32)
    return pl.pallas_call(
        _ssm_prep_kernel,
        grid=(depth,),
        in_specs=[spec, spec, pl.BlockSpec((1, g, 1), lambda l: (l, 0, 0))],
        out_specs=[spec] * 4,
        out_shape=[out] * 4,
        name="ssm_discretise",
    )(a_re, a_im, log_dt.reshape(depth, g, 1))


def _mixer_kernel(sink_ref, x_ref, mod_ref, ng_ref, watt_ref, wssm_ref, wpl_ref, perm_ref, permt_ref,
                  bmat_ref, cre_ref, cim_ref, lam_ref, dskip_ref, wglu_ref, bglu_ref, wpool_ref, pscale_ref,
                  ya_ref, ys_ref, yp_ref,
                  hb_ref, att_ref, s_ref, p_ref, kring_ref, vring_ref, bu_ref, st_ref, pext_ref,
                  *, nb):
    i = pl.program_id(0)
    tl = CHUNK
    m = nb * tl

    @pl.when(i == 0)
    def _init():
        kring_ref[...] = jnp.zeros_like(kring_ref)
        vring_ref[...] = jnp.zeros_like(vring_ref)
        st_ref[...] = jnp.zeros_like(st_ref)
        pext_ref[:, 0:POOL_HALO, :] = jnp.zeros((nb, POOL_HALO, POOL_W), F32)

    x = x_ref[...]
    ms = jnp.mean(x * x, axis=-1, keepdims=True)
    h = x * lax.rsqrt(ms + EPS) * ng_ref[...]
    h = h * (1.0 + mod_ref[1]) + mod_ref[0]
    hb_ref[...] = h.reshape(m, D_MODEL).astype(BF16)

    att_ref[...] = jnp.dot(hb_ref[...], watt_ref[...], preferred_element_type=F32)
    lane = lax.broadcasted_iota(jnp.int32, (m, LANES), 1)
    lo = lane < HEAD_DIM
    zero = jnp.zeros((m, LANES), F32)
    for ring, c0 in ((kring_ref, ATT_W), (vring_ref, ATT_W + KV_W)):
        new = att_ref[:, c0:c0 + KV_W]
        swp = pltpu.roll(new, HEAD_DIM, axis=1)
        variants = (jnp.where(lo, new, zero),
                    jnp.where(lo, zero, swp),
                    jnp.where(lo, swp, zero),
                    jnp.where(lo, zero, new))
        for var in range(4):
            for b in range(nb):
                ring[var, b, 0:NK - CHUNK, :] = ring[var, b, CHUNK:NK, :]
            ring[var, :, NK - CHUNK:NK, :] = variants[var].reshape(nb, tl, LANES).astype(BF16)

    hpk = N_HEADS // N_KV_HEADS
    blk_heads = []
    for kv in range(N_KV_HEADS):
        for e in range(2):
            blk_heads.append((kv * hpk + e, kv * hpk + 2 + e))
    nblk = len(blk_heads)
    rb = 2 * tl
    for b in range(nb):
        for kv in range(N_KV_HEADS):
            qq = jnp.concatenate([att_ref[b * tl:(b + 1) * tl, (2 * kv + j) * LANES:(2 * kv + j + 1) * LANES]
                                  for j in range(2)], axis=0).astype(BF16)
            for e in range(2):
                blk = 2 * kv + e
                s_ref[b, blk * rb:(blk + 1) * rb, :] = lax.dot_general(
                    qq, kring_ref[blk, b], (((1,), (1,)), ((), ())), preferred_element_type=F32)

    row = lax.broadcasted_iota(jnp.int32, (rb, NK), 0)
    kj = lax.broadcasted_iota(jnp.int32, (rb, NK), 1)
    qi = jnp.where(row < tl, row, row - tl)
    dist = jnp.abs(qi + WIN_CHUNKS * CHUNK - kj).astype(F32)
    valid = (i * CHUNK + kj) >= WIN_CHUNKS * CHUNK
    first = lax.broadcasted_iota(jnp.int32, (rb, 1), 0) < tl
    bias, sink = [], []
    for ha, hb_ in blk_heads:
        slope = jnp.where(first, 2.0 ** (-(ha + 1)), 2.0 ** (-(hb_ + 1)))
        bias.append(slope * dist)
        sink.append(jnp.where(first, sink_ref[ha], sink_ref[hb_]))

    def softmax_batch(b, carry):
        for blk in range(nblk):
            s = s_ref[b, blk * rb:(blk + 1) * rb, :] - bias[blk]
            s = jnp.where(valid, s, NEG_INF)
            mx = jnp.maximum(jnp.max(s, axis=-1, keepdims=True), sink[blk])
            p = jnp.exp(s - mx)
            den = jnp.sum(p, axis=-1, keepdims=True) + jnp.exp(sink[blk] - mx)
            p_ref[b, blk * rb:(blk + 1) * rb, :] = (p * (1.0 / den)).astype(BF16)
        return carry

    lax.fori_loop(0, nb, softmax_batch, 0)

    for b in range(nb):
        for kv in range(N_KV_HEADS):
            o = None
            for e in range(2):
                blk = 2 * kv + e
                t = jnp.dot(p_ref[b, blk * rb:(blk + 1) * rb, :], vring_ref[blk, b], preferred_element_type=F32)
                o = t if o is None else o + t
            for j in range(2):
                c0 = (2 * kv + j) * LANES
                z = att_ref[b * tl:(b + 1) * tl, ATT_W + 2 * KV_W + c0:ATT_W + 2 * KV_W + c0 + LANES]
                ya_ref[b, :, c0:c0 + LANES] = (o[j * tl:(j + 1) * tl] * _silu(z)).astype(BF16)

    us = jnp.dot(hb_ref[...], wssm_ref[...], preferred_element_type=F32)
    u = us[:, :SSM_W]
    z = us[:, SSM_W:]
    u_t = jnp.dot(perm_ref[...], u.astype(BF16), preferred_element_type=F32).astype(BF16)
    half = N_STATE // 4
    for r in range(4):
        res = jnp.dot(u_t[:, r * LANES:(r + 1) * LANES], bmat_ref[r], preferred_element_type=F32)
        bu_ref[:, r * half:(r + 1) * half] = res[:, :half]
        bu_ref[:, N_STATE + r * half:N_STATE + (r + 1) * half] = res[:, half:]

    cw = 512
    for cb in range(N_STATE // cw):
        c0 = cb * cw
        a_re = jnp.broadcast_to(lam_ref[0:1, c0:c0 + cw], (nb, cw))
        a_im = jnp.broadcast_to(lam_ref[1:2, c0:c0 + cw], (nb, cw))

        def step(t, carry, c0=c0, a_re=a_re, a_im=a_im):
            s_re, s_im = carry
            r0 = pl.multiple_of(t * nb, nb)
            n_re = a_re * s_re - a_im * s_im + bu_ref[pl.ds(r0, nb), c0:c0 + cw]
            n_im = a_re * s_im + a_im * s_re + bu_ref[pl.ds(r0, nb), N_STATE + c0:N_STATE + c0 + cw]
            bu_ref[pl.ds(r0, nb), c0:c0 + cw] = n_re
            bu_ref[pl.ds(r0, nb), N_STATE + c0:N_STATE + c0 + cw] = n_im
            return n_re, n_im

        s_re, s_im = lax.fori_loop(0, tl, step, (st_ref[:, c0:c0 + cw], st_ref[:, N_STATE + c0:N_STATE + c0 + cw]),
                                   unroll=8)
        st_ref[:, c0:c0 + cw] = s_re
        st_ref[:, N_STATE + c0:N_STATE + c0 + cw] = s_im

    ycols = []
    for r in range(4):
        x_re = bu_ref[:, r * half:(r + 1) * half].astype(BF16)
        x_im = bu_ref[:, N_STATE + r * half:N_STATE + (r + 1) * half].astype(BF16)
        ycols.append(jnp.dot(x_re, cre_ref[r], preferred_element_type=F32)
                     + jnp.dot(x_im, cim_ref[r], preferred_element_type=F32))
    y_t = jnp.concatenate(ycols, axis=1)
    y = jnp.dot(permt_ref[...], y_t.astype(BF16), preferred_element_type=F32)
    y = _gelu_tanh(y + dskip_ref[...] * u)
    gl = jnp.dot(y.astype(BF16), wglu_ref[...], preferred_element_type=F32) + bglu_ref[...]
    y = y * _sigmoid(gl)
    ys_ref[...] = (y * _silu(z)).reshape(nb, tl, SSM_W).astype(BF16)

    up = jnp.dot(hb_ref[...], wpl_ref[...], preferred_element_type=F32)
    z = up[:, POOL_W:]
    pext_ref[:, POOL_HALO:POOL_HALO + tl, :] = up[:, :POOL_W].reshape(nb, tl, POOL_W)
    tpos = i * tl + lax.broadcasted_iota(jnp.int32, (1, tl, 1), 1)
    groups = []
    for gi, w in enumerate(POOL_WINDOWS):
        cs = slice(gi * POOL_GW, (gi + 1) * POOL_GW)
        tok = pext_ref[:, POOL_HALO:POOL_HALO + tl, cs]
        ssum = tok
        for k in range(1, w):
            ssum = ssum + pext_ref[:, POOL_HALO - k:POOL_HALO - k + tl, cs]
        cnt = jnp.minimum(tpos + 1, w).astype(F32)
        pooled = ssum / cnt - tok
        groups.append(jnp.dot(pooled.reshape(m, POOL_GW).astype(BF16), wpool_ref[gi], preferred_element_type=F32))
    y = jnp.concatenate(groups, axis=1) * pscale_ref[...]
    yp_ref[...] = (y * _silu(z)).reshape(nb, tl, POOL_W).astype(BF16)
    pext_ref[:, 0:POOL_HALO, :] = pext_ref[:, tl:tl + POOL_HALO, :]


def _mixers(x, mod, norm_g, sinks, wts):
    nb, seq, d = x.shape
    tl = CHUNK
    m = nb * tl
    consts = (wts["w_att"], wts["w_ssm"], wts["w_pl"], wts["perm"], wts["perm_t"], wts["bmat"], wts["cre"],
              wts["cim"], wts["lam"], wts["dskip"], wts["w_glu"], wts["b_glu"], wts["w_pool"], wts["pscale"])
    out_spec = pl.BlockSpec((nb, tl, SSM_W), lambda i: (0, i, 0))
    out_shape = jax.ShapeDtypeStruct((nb, seq, SSM_W), BF16)
    return pl.pallas_call(
        functools.partial(_mixer_kernel, nb=nb),
        grid=(seq // tl,),
        in_specs=[pl.BlockSpec(memory_space=pltpu.SMEM),
                  pl.BlockSpec((nb, tl, d), lambda i: (0, i, 0)),
                  _const_spec(mod.shape),
                  _const_spec(norm_g.shape)] + [_const_spec(c.shape) for c in consts],
        out_specs=[out_spec] * 3,
        out_shape=[out_shape] * 3,
        scratch_shapes=[pltpu.VMEM((m, d), BF16),
                        pltpu.VMEM((m, 2 * ATT_W + 2 * KV_W), F32),
                        pltpu.VMEM((nb, N_HEADS * tl, NK), F32),
                        pltpu.VMEM((nb, N_HEADS * tl, NK), BF16),
                        pltpu.VMEM((4, nb, NK, LANES), BF16),
                        pltpu.VMEM((4, nb, NK, LANES), BF16),
                        pltpu.VMEM((m, 2 * N_STATE), F32),
                        pltpu.VMEM((nb, 2 * N_STATE), F32),
                        pltpu.VMEM((nb, POOL_HALO + tl, POOL_W), F32)],
        compiler_params=pltpu.CompilerParams(dimension_semantics=("arbitrary",), vmem_limit_bytes=VMEM_LIMIT),
        name="mixers",
    )(sinks, x, mod, norm_g, *consts)


def _merge_kernel(x_ref, mod_ref, ng_ref, ya_ref, ys_ref, yp_ref, wg_ref, wba_ref, wbs_ref, wbp_ref, wout_ref,
                  fg_ref, o_ref, *, final):
    x = x_ref[0]
    ms = jnp.mean(x * x, axis=-1, keepdims=True)
    h = x * lax.rsqrt(ms + EPS) * ng_ref[...]
    hb = (h * (1.0 + mod_ref[1, 0]) + mod_ref[0, 0]).astype(BF16)
    merged = None
    for j, (y_ref, wb_ref) in enumerate(((ya_ref, wba_ref), (ys_ref, wbs_ref), (yp_ref, wbp_ref))):
        g = jnp.dot(hb, wg_ref[:, j * D_MODEL:(j + 1) * D_MODEL], preferred_element_type=F32)
        p = jnp.dot(y_ref[0], wb_ref[...], preferred_element_type=F32)
        term = _sigmoid(g) * p
        merged = term if merged is None else merged + term
    out = jnp.dot(merged.astype(BF16), wout_ref[...], preferred_element_type=F32)
    xn = x + mod_ref[2, 0] * out
    if final:
        ms = jnp.mean(xn * xn, axis=-1, keepdims=True)
        xn = xn * lax.rsqrt(ms + EPS) * fg_ref[...]
    o_ref[0] = xn


def _merge(x, mod, norm_g, ya, ys, yp, wts, final_g, final):
    nb, seq, d = x.shape
    tm = 512
    row = lambda w: pl.BlockSpec((1, tm, w), lambda b, i: (b, i, 0))
    return pl.pallas_call(
        functools.partial(_merge_kernel, final=final),
        grid=(nb, seq // tm),
        in_specs=[row(d),
                  pl.BlockSpec((3, 1, 1, d), lambda b, i: (0, b, 0, 0)),
                  _const_spec(norm_g.shape),
                  row(ATT_W), row(SSM_W), row(POOL_W),
                  _const_spec(wts["w_g"].shape), _const_spec(wts["w_br_att"].shape),
                  _const_spec(wts["w_br_ssm"].shape), _const_spec(wts["w_br_pool"].shape),
                  _const_spec(wts["w_out"].shape), _const_spec(final_g.shape)],
        out_specs=row(d),
        out_shape=jax.ShapeDtypeStruct(x.shape, x.dtype),
        compiler_params=pltpu.CompilerParams(dimension_semantics=("arbitrary", "arbitrary"),
                                             vmem_limit_bytes=VMEM_LIMIT),
        name="merge",
    )(x, mod, norm_g, ya, ys, yp, wts["w_g"], wts["w_br_att"], wts["w_br_ssm"], wts["w_br_pool"], wts["w_out"],
      final_g)


def _block_diag(blocks):
    n, r, c = blocks.shape
    eye = jnp.eye(n, dtype=blocks.dtype)
    return (blocks[:, :, None, :] * eye[:, None, :, None]).reshape(n * r, n * c)


def _perm_matrix(nb, tl):
    p = np.zeros((nb * tl, nb * tl), np.float32)
    b, t = np.meshgrid(np.arange(nb), np.arange(tl), indexing="ij")
    p[(t * nb + b).ravel(), (b * tl + t).ravel()] = 1.0
    return p


def _layer_weights(li, nb, w_in, lam_re, lam_im, coef_re, coef_im, ssm_b_re, ssm_b_im, ssm_c_re, ssm_c_im, ssm_d,
                   w_glu, b_glu, w_pool, pool_scale, w_br_att, w_br_ssm, w_br_pool, w_out):
    w = w_in[li]
    o_k, o_v, o_us, o_up, o_za, o_zs, o_zp, o_g = 512, 640, 768, 1280, 1792, 2304, 2816, 3328
    wts = {}
    wts["w_att"] = jnp.concatenate([w[:, :o_k] * (1.0 / math.sqrt(HEAD_DIM)), w[:, o_k:o_us], w[:, o_za:o_zs]],
                                   axis=1).astype(BF16)
    wts["w_ssm"] = jnp.concatenate([w[:, o_us:o_up], w[:, o_zs:o_zp]], axis=1).astype(BF16)
    wts["w_pl"] = jnp.concatenate([w[:, o_up:o_za], w[:, o_zp:o_g]], axis=1).astype(BF16)
    wts["w_g"] = w[:, o_g:].astype(BF16)
    perm = _perm_matrix(nb, CHUNK)
    wts["perm"] = jnp.asarray(perm, BF16)
    wts["perm_t"] = jnp.asarray(perm.T, BF16)
    cr = coef_re[li][:, :, None]
    ci = coef_im[li][:, :, None]
    bb_re = cr * ssm_b_re[li] - ci * ssm_b_im[li]
    bb_im = cr * ssm_b_im[li] + ci * ssm_b_re[li]
    gpb = LANES // SSM_GROUP
    nblk = SSM_GROUPS // gpb
    bre = jnp.swapaxes(bb_re, 1, 2).reshape(nblk, gpb, SSM_GROUP, SSM_STATE)
    bim = jnp.swapaxes(bb_im, 1, 2).reshape(nblk, gpb, SSM_GROUP, SSM_STATE)
    wts["bmat"] = jnp.stack([jnp.concatenate([_block_diag(bre[r]), _block_diag(bim[r])], axis=1)
                             for r in range(nblk)]).astype(BF16)
    cre = jnp.swapaxes(ssm_c_re[li], 1, 2).reshape(nblk, gpb, SSM_STATE, SSM_GROUP)
    cim = jnp.swapaxes(-ssm_c_im[li], 1, 2).reshape(nblk, gpb, SSM_STATE, SSM_GROUP)
    wts["cre"] = jnp.stack([_block_diag(cre[r]) for r in range(nblk)]).astype(BF16)
    wts["cim"] = jnp.stack([_block_diag(cim[r]) for r in range(nblk)]).astype(BF16)
    wts["lam"] = jnp.stack([lam_re[li].reshape(N_STATE), lam_im[li].reshape(N_STATE)])
    wts["dskip"] = ssm_d[li].reshape(1, SSM_W)
    wts["w_glu"] = w_glu[li].astype(BF16)
    wts["b_glu"] = b_glu[li].reshape(1, SSM_W)
    wts["w_pool"] = w_pool[li].astype(BF16)
    wts["pscale"] = pool_scale[li].reshape(1, POOL_W)
    wts["w_br_att"] = w_br_att[li].astype(BF16)
    wts["w_br_ssm"] = w_br_ssm[li].astype(BF16)
    wts["w_br_pool"] = w_br_pool[li].astype(BF16)
    wts["w_out"] = w_out[li].astype(BF16)
    return wts


def kernel(x, c, norm_g, w_ada, b_ada, w_in, attn_sinks, ssm_a_re, ssm_a_im, ssm_log_dt, ssm_b_re, ssm_b_im,
           ssm_c_re, ssm_c_im, ssm_d, w_glu, b_glu, w_pool, pool_scale, w_br_att, w_br_ssm, w_br_pool, w_out,
           final_g):
    nb, seq, d = x.shape
    depth = w_in.shape[0]
    assert d == D_MODEL and nb == SUBLANES and seq % 512 == 0
    mod_all = _modulation(c, w_ada, b_ada)
    lam_re, lam_im, coef_re, coef_im = _ssm_prep(ssm_a_re, ssm_a_im, ssm_log_dt)
    fg = final_g.reshape(1, d)
    for li in range(depth):
        mod = jnp.transpose(mod_all[li].reshape(nb, 3, d), (1, 0, 2)).reshape(3, nb, 1, d)
        ng = norm_g[li].reshape(1, d)
        wts = _layer_weights(li, nb, w_in, lam_re, lam_im, coef_re, coef_im, ssm_b_re, ssm_b_im, ssm_c_re,
                             ssm_c_im, ssm_d, w_glu, b_glu, w_pool, pool_scale, w_br_att, w_br_ssm, w_br_pool,
                             w_out)
        ya, ys, yp = _mixers(x, mod, ng, attn_sinks[li], wts)
        x = _merge(x, mod, ng, ya, ys, yp, wts, fg, final=(li == depth - 1))
    return x
```

```python
import functools
import math

import numpy as np
import jax
import jax.numpy as jnp
from jax import lax
from jax.experimental import pallas as pl
from jax.experimental.pallas import tpu as pltpu

F32 = jnp.float32
BF16 = jnp.bfloat16

D_MODEL = 1024
CHUNK = 64
N_HEADS = 8
N_KV_HEADS = 2
HEAD_DIM = 64
WIN_CHUNKS = 2
NK = (WIN_CHUNKS + 1) * CHUNK
ATT_W = N_HEADS * HEAD_DIM
KV_W = N_KV_HEADS * HEAD_DIM
SSM_W = 512
SSM_GROUP = 16
SSM_GROUPS = 32
SSM_STATE = 64
N_STATE = SSM_GROUPS * SSM_STATE
POOL_W = 512
POOL_WINDOWS = (2, 4, 8, 16)
POOL_GW = 128
POOL_HALO = 16
EPS = 1e-6
NEG_INF = -1e30
LANES = 128
SUBLANES = 8
VMEM_LIMIT = 58 * 1024 * 1024


def _sigmoid(x):
    return 0.5 * jnp.tanh(0.5 * x) + 0.5


def _silu(x):
    return x * _sigmoid(x)


def _gelu_tanh(x):
    return 0.5 * x * (1.0 + jnp.tanh(math.sqrt(2.0 / math.pi) * (x + 0.044715 * (x * x * x))))


def _const_spec(shape):
    nd = len(shape)
    return pl.BlockSpec(shape, lambda *_: (0,) * nd, pipeline_mode=pl.Buffered(1))


def _mod_kernel(c_ref, w_ref, b_ref, o_ref):
    ca = _silu(c_ref[...])
    o_ref[0] = jnp.dot(ca, w_ref[0], preferred_element_type=F32, precision=lax.Precision.HIGHEST) + b_ref[0]


def _modulation(c, w_ada, b_ada):
    depth, d, n3 = w_ada.shape
    b = c.shape[0]
    nb = n3 // d
    return pl.pallas_call(
        _mod_kernel,
        grid=(depth, nb),
        in_specs=[pl.BlockSpec((b, d), lambda l, j: (0, 0)),
                  pl.BlockSpec((1, d, d), lambda l, j: (l, 0, j)),
                  pl.BlockSpec((1, 1, d), lambda l, j: (l, 0, j))],
        out_specs=pl.BlockSpec((1, b, d), lambda l, j: (l, 0, j)),
        out_shape=jax.ShapeDtypeStruct((depth, b, n3), F32),
        name="adaln_mod",
    )(c, w_ada, b_ada.reshape(depth, 1, n3))


def _ssm_prep_kernel(are_ref, aim_ref, ldt_ref, lre_ref, lim_ref, cre_ref, cim_ref):
    a_re = are_ref[0]
    a_im = aim_ref[0]
    dt = jnp.exp(ldt_ref[0])
    mag = jnp.exp(a_re * dt)
    ang = a_im * dt
    l_re = mag * jnp.cos(ang)
    l_im = mag * jnp.sin(ang)
    n_re = l_re - 1.0
    den = a_re * a_re + a_im * a_im
    lre_ref[0] = l_re
    lim_ref[0] = l_im
    cre_ref[0] = (n_re * a_re + l_im * a_im) / den
    cim_ref[0] = (l_im * a_re - n_re * a_im) / den


def _ssm_prep(a_re, a_im, log_dt):
    depth, g, p = a_re.shape
    spec = pl.BlockSpec((1, g, p), lambda l: (l, 0, 0))
    out = jax.ShapeDtypeStruct((depth, g, p), F32)
    return pl.pallas_call(
        _ssm_prep_kernel,
        grid=(depth,),
        in_specs=[spec, spec, pl.BlockSpec((1, g, 1), lambda l: (l, 0, 0))],
        out_specs=[spec] * 4,
        out_shape=[out] * 4,
        name="ssm_discretise",
    )(a_re, a_im, log_dt.reshape(depth, g, 1))


def _mixer_kernel(sink_ref, x_ref, mod_ref, ng_ref, watt_ref, wssm_ref, wpl_ref, perm_ref, permt_ref,
                  bmat_ref, cre_ref, cim_ref, lam_ref, dskip_ref, wglu_ref, bglu_ref, wpool_ref, pscale_ref,
                  ya_ref, ys_ref, yp_ref,
                  hb_ref, att_ref, s_ref, p_ref, kring_ref, vring_ref, us_ref, ut_ref, bu_ref, st_ref, yt_ref,
                  pext_ref, zp_ref,
                  *, nb):
    i = pl.program_id(0)
    tl = CHUNK
    m = nb * tl
    half = N_STATE // 4
    hpk = N_HEADS // N_KV_HEADS
    rb = 2 * tl
    blk_heads = [(kv * hpk + e, kv * hpk + 2 + e) for kv in range(N_KV_HEADS) for e in range(2)]

    @pl.when(i == 0)
    def _init():
        kring_ref[...] = jnp.zeros_like(kring_ref)
        vring_ref[...] = jnp.zeros_like(vring_ref)
        st_ref[...] = jnp.zeros_like(st_ref)
        pext_ref[:, 0:POOL_HALO, :] = jnp.zeros((nb, POOL_HALO, POOL_W), F32)

    def norm():
        x = x_ref[...]
        ms = jnp.mean(x * x, axis=-1, keepdims=True)
        h = x * lax.rsqrt(ms + EPS) * ng_ref[...]
        h = h * (1.0 + mod_ref[1]) + mod_ref[0]
        hb_ref[...] = h.reshape(m, D_MODEL).astype(BF16)

    def att_project():
        att_ref[...] = jnp.dot(hb_ref[...], watt_ref[...], preferred_element_type=F32)
        lane = lax.broadcasted_iota(jnp.int32, (m, LANES), 1)
        lo = lane < HEAD_DIM
        zero = jnp.zeros((m, LANES), F32)
        for ring, c0 in ((kring_ref, ATT_W), (vring_ref, ATT_W + KV_W)):
            new = att_ref[:, c0:c0 + KV_W]
            swp = pltpu.roll(new, HEAD_DIM, axis=1)
            variants = (jnp.where(lo, new, zero),
                        jnp.where(lo, zero, swp),
                        jnp.where(lo, swp, zero),
                        jnp.where(lo, zero, new))
            for var in range(4):
                for b in range(nb):
                    ring[var, b, 0:NK - CHUNK, :] = ring[var, b, CHUNK:NK, :]
                ring[var, :, NK - CHUNK:NK, :] = variants[var].reshape(nb, tl, LANES).astype(BF16)

    def att_scores():
        for b in range(nb):
            for kv in range(N_KV_HEADS):
                qq = jnp.concatenate([att_ref[b * tl:(b + 1) * tl, (2 * kv + j) * LANES:(2 * kv + j + 1) * LANES]
                                      for j in range(2)], axis=0).astype(BF16)
                for e in range(2):
                    blk = 2 * kv + e
                    s_ref[b, blk * rb:(blk + 1) * rb, :] = lax.dot_general(
                        qq, kring_ref[blk, b], (((1,), (1,)), ((), ())), preferred_element_type=F32)

    def att_softmax(blk):
        ha, hb_ = blk_heads[blk]
        row = lax.broadcasted_iota(jnp.int32, (rb, NK), 0)
        kj = lax.broadcasted_iota(jnp.int32, (rb, NK), 1)
        qi = jnp.where(row < tl, row, row - tl)
        dist = jnp.abs(qi + WIN_CHUNKS * CHUNK - kj).astype(F32)
        valid = (i * CHUNK + kj) >= WIN_CHUNKS * CHUNK
        first = lax.broadcasted_iota(jnp.int32, (rb, 1), 0) < tl
        bias = jnp.where(first, 2.0 ** (-(ha + 1)), 2.0 ** (-(hb_ + 1))) * dist
        sink = jnp.where(first, sink_ref[ha], sink_ref[hb_])
        for b in range(nb):
            s = s_ref[b, blk * rb:(blk + 1) * rb, :] - bias
            s = jnp.where(valid, s, NEG_INF)
            mx = jnp.maximum(jnp.max(s, axis=-1, keepdims=True), sink)
            p = jnp.exp(s - mx)
            den = jnp.sum(p, axis=-1, keepdims=True) + jnp.exp(sink - mx)
            p_ref[b, blk * rb:(blk + 1) * rb, :] = (p * (1.0 / den)).astype(BF16)

    def att_values():
        for b in range(nb):
            for kv in range(N_KV_HEADS):
                o = None
                for e in range(2):
                    blk = 2 * kv + e
                    t = jnp.dot(p_ref[b, blk * rb:(blk + 1) * rb, :], vring_ref[blk, b],
                                preferred_element_type=F32)
                    o = t if o is None else o + t
                for j in range(2):
                    c0 = (2 * kv + j) * LANES
                    z = att_ref[b * tl:(b + 1) * tl, ATT_W + 2 * KV_W + c0:ATT_W + 2 * KV_W + c0 + LANES]
                    ya_ref[b, :, c0:c0 + LANES] = (o[j * tl:(j + 1) * tl] * _silu(z)).astype(BF16)

    def ssm_project():
        us_ref[...] = jnp.dot(hb_ref[...], wssm_ref[...], preferred_element_type=F32)
        ut_ref[...] = jnp.dot(perm_ref[...], us_ref[:, :SSM_W].astype(BF16),
                              preferred_element_type=F32).astype(BF16)

    def ssm_expand(r):
        res = jnp.dot(ut_ref[:, r * LANES:(r + 1) * LANES], bmat_ref[r], preferred_element_type=F32)
        bu_ref[:, r * half:(r + 1) * half] = res[:, :half]
        bu_ref[:, N_STATE + r * half:N_STATE + (r + 1) * half] = res[:, half:]

    def ssm_scan(r):
        c0 = r * half
        a_re = jnp.broadcast_to(lam_ref[0:1, c0:c0 + half], (nb, half))
        a_im = jnp.broadcast_to(lam_ref[1:2, c0:c0 + half], (nb, half))
        s_re = st_ref[:, c0:c0 + half]
        s_im = st_ref[:, N_STATE + c0:N_STATE + c0 + half]
        for t in range(tl):
            r0 = t * nb
            n_re = a_re * s_re - a_im * s_im + bu_ref[r0:r0 + nb, c0:c0 + half]
            n_im = a_re * s_im + a_im * s_re + bu_ref[r0:r0 + nb, N_STATE + c0:N_STATE + c0 + half]
            bu_ref[r0:r0 + nb, c0:c0 + half] = n_re
            bu_ref[r0:r0 + nb, N_STATE + c0:N_STATE + c0 + half] = n_im
            s_re, s_im = n_re, n_im
        st_ref[:, c0:c0 + half] = s_re
        st_ref[:, N_STATE + c0:N_STATE + c0 + half] = s_im

    def ssm_contract(r):
        x_re = bu_ref[:, r * half:(r + 1) * half].astype(BF16)
        x_im = bu_ref[:, N_STATE + r * half:N_STATE + (r + 1) * half].astype(BF16)
        yt_ref[:, r * LANES:(r + 1) * LANES] = (jnp.dot(x_re, cre_ref[r], preferred_element_type=F32)
                                               + jnp.dot(x_im, cim_ref[r], preferred_element_type=F32)
                                               ).astype(BF16)

    def ssm_output():
        y = jnp.dot(permt_ref[...], yt_ref[...], preferred_element_type=F32)
        y = _gelu_tanh(y + dskip_ref[...] * us_ref[:, :SSM_W])
        gl = jnp.dot(y.astype(BF16), wglu_ref[...], preferred_element_type=F32) + bglu_ref[...]
        y = y * _sigmoid(gl)
        ys_ref[...] = (y * _silu(us_ref[:, SSM_W:])).reshape(nb, tl, SSM_W).astype(BF16)

    def pool_project():
        up = jnp.dot(hb_ref[...], wpl_ref[...], preferred_element_type=F32)
        zp_ref[...] = up[:, POOL_W:]
        pext_ref[:, POOL_HALO:POOL_HALO + tl, :] = up[:, :POOL_W].reshape(nb, tl, POOL_W)

    def pool_mix():
        tpos = i * tl + lax.broadcasted_iota(jnp.int32, (1, tl, 1), 1)
        groups = []
        for gi, w in enumerate(POOL_WINDOWS):
            cs = slice(gi * POOL_GW, (gi + 1) * POOL_GW)
            tok = pext_ref[:, POOL_HALO:POOL_HALO + tl, cs]
            ssum = tok
            for k in range(1, w):
                ssum = ssum + pext_ref[:, POOL_HALO - k:POOL_HALO - k + tl, cs]
            cnt = jnp.minimum(tpos + 1, w).astype(F32)
            pooled = ssum / cnt - tok
            groups.append(jnp.dot(pooled.reshape(m, POOL_GW).astype(BF16), wpool_ref[gi],
                                  preferred_element_type=F32))
        y = jnp.concatenate(groups, axis=1) * pscale_ref[...]
        yp_ref[...] = (y * _silu(zp_ref[...])).reshape(nb, tl, POOL_W).astype(BF16)
        pext_ref[:, 0:POOL_HALO, :] = pext_ref[:, tl:tl + POOL_HALO, :]

    P = functools.partial
    schedule = (norm, att_project, pool_project, att_scores, ssm_project,
                P(att_softmax, 0), P(ssm_expand, 0), P(att_softmax, 1), P(ssm_expand, 1),
                P(ssm_scan, 0), P(ssm_expand, 2), P(att_softmax, 2), P(ssm_scan, 1), P(ssm_expand, 3),
                P(att_softmax, 3), P(ssm_contract, 0), P(ssm_scan, 2), P(ssm_contract, 1), pool_mix,
                P(ssm_scan, 3), att_values, P(ssm_contract, 2), P(ssm_contract, 3), ssm_output)
    for phase in schedule:
        phase()


def _mixers(x, mod, norm_g, sinks, wts):
    nb, seq, d = x.shape
    tl = CHUNK
    m = nb * tl
    consts = (wts["w_att"], wts["w_ssm"], wts["w_pl"], wts["perm"], wts["perm_t"], wts["bmat"], wts["cre"],
              wts["cim"], wts["lam"], wts["dskip"], wts["w_glu"], wts["b_glu"], wts["w_pool"], wts["pscale"])
    out_spec = pl.BlockSpec((nb, tl, SSM_W), lambda i: (0, i, 0))
    out_shape = jax.ShapeDtypeStruct((nb, seq, SSM_W), BF16)
    return pl.pallas_call(
        functools.partial(_mixer_kernel, nb=nb),
        grid=(seq // tl,),
        in_specs=[pl.BlockSpec(memory_space=pltpu.SMEM),
                  pl.BlockSpec((nb, tl, d), lambda i: (0, i, 0)),
                  _const_spec(mod.shape),
                  _const_spec(norm_g.shape)] + [_const_spec(c.shape) for c in consts],
        out_specs=[out_spec] * 3,
        out_shape=[out_shape] * 3,
        scratch_shapes=[pltpu.VMEM((m, d), BF16),
                        pltpu.VMEM((m, 2 * ATT_W + 2 * KV_W), F32),
                        pltpu.VMEM((nb, N_HEADS * tl, NK), F32),
                        pltpu.VMEM((nb, N_HEADS * tl, NK), BF16),
                        pltpu.VMEM((4, nb, NK, LANES), BF16),
                        pltpu.VMEM((4, nb, NK, LANES), BF16),
                        pltpu.VMEM((m, 2 * SSM_W), F32),
                        pltpu.VMEM((m, SSM_W), BF16),
                        pltpu.VMEM((m, 2 * N_STATE), F32),
                        pltpu.VMEM((nb, 2 * N_STATE), F32),
                        pltpu.VMEM((m, SSM_W), BF16),
                        pltpu.VMEM((nb, POOL_HALO + tl, POOL_W), F32),
                        pltpu.VMEM((m, POOL_W), F32)],
        compiler_params=pltpu.CompilerParams(dimension_semantics=("arbitrary",), vmem_limit_bytes=VMEM_LIMIT),
        name="mixers",
    )(sinks, x, mod, norm_g, *consts)


def _merge_kernel(x_ref, mod_ref, ng_ref, ya_ref, ys_ref, yp_ref, wg_ref, wba_ref, wbs_ref, wbp_ref, wout_ref,
                  fg_ref, o_ref, *, final):
    x = x_ref[0]
    ms = jnp.mean(x * x, axis=-1, keepdims=True)
    h = x * lax.rsqrt(ms + EPS) * ng_ref[...]
    hb = (h * (1.0 + mod_ref[1, 0]) + mod_ref[0, 0]).astype(BF16)
    merged = None
    for j, (y_ref, wb_ref) in enumerate(((ya_ref, wba_ref), (ys_ref, wbs_ref), (yp_ref, wbp_ref))):
        g = jnp.dot(hb, wg_ref[:, j * D_MODEL:(j + 1) * D_MODEL], preferred_element_type=F32)
        p = jnp.dot(y_ref[0], wb_ref[...], preferred_element_type=F32)
        term = _sigmoid(g) * p
        merged = term if merged is None else merged + term
    out = jnp.dot(merged.astype(BF16), wout_ref[...], preferred_element_type=F32)
    xn = x + mod_ref[2, 0] * out
    if final:
        ms = jnp.mean(xn * xn, axis=-1, keepdims=True)
        xn = xn * lax.rsqrt(ms + EPS) * fg_ref[...]
    o_ref[0] = xn


def _merge(x, mod, norm_g, ya, ys, yp, wts, final_g, final):
    nb, seq, d = x.shape
    tm = 512
    row = lambda w: pl.BlockSpec((1, tm, w), lambda b, i: (b, i, 0))
    return pl.pallas_call(
        functools.partial(_merge_kernel, final=final),
        grid=(nb, seq // tm),
        in_specs=[row(d),
                  pl.BlockSpec((3, 1, 1, d), lambda b, i: (0, b, 0, 0)),
                  _const_spec(norm_g.shape),
                  row(ATT_W), row(SSM_W), row(POOL_W),
                  _const_spec(wts["w_g"].shape), _const_spec(wts["w_br_att"].shape),
                  _const_spec(wts["w_br_ssm"].shape), _const_spec(wts["w_br_pool"].shape),
                  _const_spec(wts["w_out"].shape), _const_spec(final_g.shape)],
        out_specs=row(d),
        out_shape=jax.ShapeDtypeStruct(x.shape, x.dtype),
        compiler_params=pltpu.CompilerParams(dimension_semantics=("arbitrary", "arbitrary"),
                                             vmem_limit_bytes=VMEM_LIMIT),
        name="merge",
    )(x, mod, norm_g, ya, ys, yp, wts["w_g"], wts["w_br_att"], wts["w_br_ssm"], wts["w_br_pool"], wts["w_out"],
      final_g)


def _block_diag(blocks):
    n, r, c = blocks.shape
    eye = jnp.eye(n, dtype=blocks.dtype)
    return (blocks[:, :, None, :] * eye[:, None, :, None]).reshape(n * r, n * c)


def _perm_matrix(nb, tl):
    p = np.zeros((nb * tl, nb * tl), np.float32)
    b, t = np.meshgrid(np.arange(nb), np.arange(tl), indexing="ij")
    p[(t * nb + b).ravel(), (b * tl + t).ravel()] = 1.0
    return p


def _layer_weights(li, nb, w_in, lam_re, lam_im, coef_re, coef_im, ssm_b_re, ssm_b_im, ssm_c_re, ssm_c_im, ssm_d,
                   w_glu, b_glu, w_pool, pool_scale, w_br_att, w_br_ssm, w_br_pool, w_out):
    w = w_in[li]
    o_k, o_v, o_us, o_up, o_za, o_zs, o_zp, o_g = 512, 640, 768, 1280, 1792, 2304, 2816, 3328
    wts = {}
    wts["w_att"] = jnp.concatenate([w[:, :o_k] * (1.0 / math.sqrt(HEAD_DIM)), w[:, o_k:o_us], w[:, o_za:o_zs]],
                                   axis=1).astype(BF16)
    wts["w_ssm"] = jnp.concatenate([w[:, o_us:o_up], w[:, o_zs:o_zp]], axis=1).astype(BF16)
    wts["w_pl"] = jnp.concatenate([w[:, o_up:o_za], w[:, o_zp:o_g]], axis=1).astype(BF16)
    wts["w_g"] = w[:, o_g:].astype(BF16)
    perm = _perm_matrix(nb, CHUNK)
    wts["perm"] = jnp.asarray(perm, BF16)
    wts["perm_t"] = jnp.asarray(perm.T, BF16)
    cr = coef_re[li][:, :, None]
    ci = coef_im[li][:, :, None]
    bb_re = cr * ssm_b_re[li] - ci * ssm_b_im[li]
    bb_im = cr * ssm_b_im[li] + ci * ssm_b_re[li]
    gpb = LANES // SSM_GROUP
    nblk = SSM_GROUPS // gpb
    bre = jnp.swapaxes(bb_re, 1, 2).reshape(nblk, gpb, SSM_GROUP, SSM_STATE)
    bim = jnp.swapaxes(bb_im, 1, 2).reshape(nblk, gpb, SSM_GROUP, SSM_STATE)
    wts["bmat"] = jnp.stack([jnp.concatenate([_block_diag(bre[r]), _block_diag(bim[r])], axis=1)
                             for r in range(nblk)]).astype(BF16)
    cre = jnp.swapaxes(ssm_c_re[li], 1, 2).reshape(nblk, gpb, SSM_STATE, SSM_GROUP)
    cim = jnp.swapaxes(-ssm_c_im[li], 1, 2).reshape(nblk, gpb, SSM_STATE, SSM_GROUP)
    wts["cre"] = jnp.stack([_block_diag(cre[r]) for r in range(nblk)]).astype(BF16)
    wts["cim"] = jnp.stack([_block_diag(cim[r]) for r in range(nblk)]).astype(BF16)
    wts["lam"] = jnp.stack([lam_re[li].reshape(N_STATE), lam_im[li].reshape(N_STATE)])
    wts["dskip"] = ssm_d[li].reshape(1, SSM_W)
    wts["w_glu"] = w_glu[li].astype(BF16)
    wts["b_glu"] = b_glu[li].reshape(1, SSM_W)
    wts["w_pool"] = w_pool[li].astype(BF16)
    wts["pscale"] = pool_scale[li].reshape(1, POOL_W)
    wts["w_br_att"] = w_br_att[li].astype(BF16)
    wts["w_br_ssm"] = w_br_ssm[li].astype(BF16)
    wts["w_br_pool"] = w_br_pool[li].astype(BF16)
    wts["w_out"] = w_out[li].astype(BF16)
    return wts


def kernel(x, c, norm_g, w_ada, b_ada, w_in, attn_sinks, ssm_a_re, ssm_a_im, ssm_log_dt, ssm_b_re, ssm_b_im,
           ssm_c_re, ssm_c_im, ssm_d, w_glu, b_glu, w_pool, pool_scale, w_br_att, w_br_ssm, w_br_pool, w_out,
           final_g):
    nb, seq, d = x.shape
    depth = w_in.shape[0]
    assert d == D_MODEL and nb == SUBLANES and seq % 512 == 0
    mod_all = _modulation(c, w_ada, b_ada)
    lam_re, lam_im, coef_re, coef_im = _ssm_prep(ssm_a_re, ssm_a_im, ssm_log_dt)
    fg = final_g.reshape(1, d)
    for li in range(depth):
        mod = jnp.transpose(mod_all[li].reshape(nb, 3, d), (1, 0, 2)).reshape(3, nb, 1, d)
        ng = norm_g[li].reshape(1, d)
        wts = _layer_weights(li, nb, w_in, lam_re, lam_im, coef_re, coef_im, ssm_b_re, ssm_b_im, ssm_c_re,
                             ssm_c_im, ssm_d, w_glu, b_glu, w_pool, pool_scale, w_br_att, w_br_ssm, w_br_pool,
                             w_out)
        ya, ys, yp = _mixers(x, mod, ng, attn_sinks[li], wts)
        x = _merge(x, mod, ng, ya, ys, yp, wts, fg, final=(li == depth - 1))
    return x
```

```python
import functools
import math

import jax
import jax.numpy as jnp
from jax import lax
from jax.experimental import pallas as pl
from jax.experimental.pallas import tpu as pltpu

F32 = jnp.float32
BF16 = jnp.bfloat16

D_MODEL = 1024
CHUNK = 64
N_HEADS = 8
N_KV_HEADS = 2
HEAD_DIM = 64
WIN_CHUNKS = 2
NK = (WIN_CHUNKS + 1) * CHUNK
ATT_W = N_HEADS * HEAD_DIM
KV_W = N_KV_HEADS * HEAD_DIM
SSM_W = 512
SSM_GROUP = 16
SSM_GROUPS = 32
SSM_STATE = 64
N_STATE = SSM_GROUPS * SSM_STATE
POOL_W = 512
POOL_WINDOWS = (2, 4, 8, 16)
POOL_GW = 128
POOL_HALO = 16
MERGE_ROWS = 1024
EPS = 1e-6
NEG_INF = -1e30
LOG2E = math.log2(math.e)
LANES = 128
SUBLANES = 8
VMEM_LIMIT = 58 * 1024 * 1024


def _sigmoid(x):
    return 0.5 * jnp.tanh(0.5 * x) + 0.5


def _silu(x):
    return x * _sigmoid(x)


def _gelu_tanh(x):
    return 0.5 * x * (1.0 + jnp.tanh(math.sqrt(2.0 / math.pi) * (x + 0.044715 * (x * x * x))))


def _const_spec(shape):
    nd = len(shape)
    return pl.BlockSpec(shape, lambda *_: (0,) * nd, pipeline_mode=pl.Buffered(1))


def _mod_kernel(c_ref, w_ref, b_ref, o_ref):
    ca = _silu(c_ref[...])
    o_ref[0] = jnp.dot(ca, w_ref[0], preferred_element_type=F32, precision=lax.Precision.HIGHEST) + b_ref[0]


def _modulation(c, w_ada, b_ada):
    depth, d, n3 = w_ada.shape
    b = c.shape[0]
    nb = n3 // d
    return pl.pallas_call(
        _mod_kernel,
        grid=(depth, nb),
        in_specs=[pl.BlockSpec((b, d), lambda l, j: (0, 0)),
                  pl.BlockSpec((1, d, d), lambda l, j: (l, 0, j)),
                  pl.BlockSpec((1, 1, d), lambda l, j: (l, 0, j))],
        out_specs=pl.BlockSpec((1, b, d), lambda l, j: (l, 0, j)),
        out_shape=jax.ShapeDtypeStruct((depth, b, n3), F32),
        name="adaln_mod",
    )(c, w_ada, b_ada.reshape(depth, 1, n3))


def _ssm_prep_kernel(are_ref, aim_ref, ldt_ref, lre_ref, lim_ref, cre_ref, cim_ref):
    a_re = are_ref[0]
    a_im = aim_ref[0]
    dt = jnp.exp(ldt_ref[0])
    mag = jnp.exp(a_re * dt)
    ang = a_im * dt
    l_re = mag * jnp.cos(ang)
    l_im = mag * jnp.sin(ang)
    n_re = l_re - 1.0
    den = a_re * a_re + a_im * a_im
    lre_ref[0] = l_re
    lim_ref[0] = l_im
    cre_ref[0] = (n_re * a_re + l_im * a_im) / den
    cim_ref[0] = (l_im * a_re - n_re * a_im) / den


def _ssm_prep(a_re, a_im, log_dt):
    depth, g, p = a_re.shape
    spec = pl.BlockSpec((1, g, p), lambda l: (l, 0, 0))
    out = jax.ShapeDtypeStruct((depth, g, p), F32)
    return pl.pallas_call(
        _ssm_prep_kernel,
        grid=(depth,),
        in_specs=[spec, spec, pl.BlockSpec((1, g, 1), lambda l: (l, 0, 0))],
        out_specs=[spec] * 4,
        out_shape=[out] * 4,
        name="ssm_discretise",
    )(a_re, a_im, log_dt.reshape(depth, g, 1))


def _mixer_kernel(sink_ref, x_ref, mod_ref, ng_ref, watt_ref, wssm_ref, wpl_ref,
                  bmat_ref, cre_ref, cim_ref, lam_ref, dskip_ref, wglu_ref, bglu_ref, wpool_ref, pscale_ref,
                  ya_ref, ys_ref, yp_ref,
                  hb_ref, att_ref, s_ref, p_ref, kring_ref, vring_ref, us_ref, ut_ref, bu_ref, st_ref, yt_ref,
                  pext_ref, zp_ref,
                  *, nb):
    i = pl.program_id(0)
    tl = CHUNK
    m = nb * tl
    half = N_STATE // 4
    hpk = N_HEADS // N_KV_HEADS
    rb = 2 * tl
    blk_heads = [(kv * hpk + e, kv * hpk + 2 + e) for kv in range(N_KV_HEADS) for e in range(2)]

    @pl.when(i == 0)
    def _init():
        kring_ref[...] = jnp.zeros_like(kring_ref)
        vring_ref[...] = jnp.zeros_like(vring_ref)
        st_ref[...] = jnp.zeros_like(st_ref)
        pext_ref[:, 0:POOL_HALO, :] = jnp.zeros((nb, POOL_HALO, POOL_W), F32)

    def norm():
        x = x_ref[...]
        ms = jnp.mean(x * x, axis=-1, keepdims=True)
        gain = ng_ref[...] * (1.0 + mod_ref[1])
        h = x * lax.rsqrt(ms + EPS) * gain + mod_ref[0]
        hb_ref[...] = h.reshape(m, D_MODEL).astype(BF16)

    def att_project():
        att_ref[...] = jnp.dot(hb_ref[...], watt_ref[...], preferred_element_type=F32)
        lane = lax.broadcasted_iota(jnp.int32, (m, LANES), 1)
        lo = lane < HEAD_DIM
        zero = jnp.zeros((m, LANES), F32)
        for ring, c0 in ((kring_ref, ATT_W), (vring_ref, ATT_W + KV_W)):
            new = att_ref[:, c0:c0 + KV_W]
            swp = pltpu.roll(new, HEAD_DIM, axis=1)
            variants = (jnp.where(lo, new, zero),
                        jnp.where(lo, zero, swp),
                        jnp.where(lo, swp, zero),
                        jnp.where(lo, zero, new))
            for var in range(4):
                for b in range(nb):
                    ring[var, b, 0:NK - CHUNK, :] = ring[var, b, CHUNK:NK, :]
                ring[var, :, NK - CHUNK:NK, :] = variants[var].reshape(nb, tl, LANES).astype(BF16)

    def att_scores():
        for b in range(nb):
            for kv in range(N_KV_HEADS):
                qq = jnp.concatenate([att_ref[b * tl:(b + 1) * tl, (2 * kv + j) * LANES:(2 * kv + j + 1) * LANES]
                                      for j in range(2)], axis=0).astype(BF16)
                for e in range(2):
                    blk = 2 * kv + e
                    s_ref[b, blk * rb:(blk + 1) * rb, :] = lax.dot_general(
                        qq, kring_ref[blk, b], (((1,), (1,)), ((), ())), preferred_element_type=F32)

    def att_softmax(blk):
        ha, hb_ = blk_heads[blk]
        row = lax.broadcasted_iota(jnp.int32, (rb, NK), 0)
        kj = lax.broadcasted_iota(jnp.int32, (rb, NK), 1)
        qi = jnp.where(row < tl, row, row - tl)
        dist = jnp.abs(qi + WIN_CHUNKS * CHUNK - kj).astype(F32)
        valid = (i * CHUNK + kj) >= WIN_CHUNKS * CHUNK
        first = lax.broadcasted_iota(jnp.int32, (rb, 1), 0) < tl
        slope = jnp.where(first, LOG2E * 2.0 ** (-(ha + 1)), LOG2E * 2.0 ** (-(hb_ + 1)))
        nbias = jnp.where(valid, -slope * dist, NEG_INF)
        sink = LOG2E * jnp.where(first, sink_ref[ha], sink_ref[hb_])
        for b in range(nb):
            s = s_ref[b, blk * rb:(blk + 1) * rb, :] + nbias
            mx = jnp.maximum(jnp.max(s, axis=-1, keepdims=True), sink)
            p = jnp.exp2(s - mx)
            den = jnp.sum(p, axis=-1, keepdims=True) + jnp.exp2(sink - mx)
            p_ref[b, blk * rb:(blk + 1) * rb, :] = (p * (1.0 / den)).astype(BF16)

    def att_values():
        for b in range(nb):
            for kv in range(N_KV_HEADS):
                o = None
                for e in range(2):
                    blk = 2 * kv + e
                    t = jnp.dot(p_ref[b, blk * rb:(blk + 1) * rb, :], vring_ref[blk, b],
                                preferred_element_type=F32)
                    o = t if o is None else o + t
                for j in range(2):
                    c0 = (2 * kv + j) * LANES
                    z = att_ref[b * tl:(b + 1) * tl, ATT_W + 2 * KV_W + c0:ATT_W + 2 * KV_W + c0 + LANES]
                    ya_ref[b, :, c0:c0 + LANES] = (o[j * tl:(j + 1) * tl] * _silu(z)).astype(BF16)

    def ssm_project():
        us_ref[...] = jnp.dot(hb_ref[...], wssm_ref[...], preferred_element_type=F32)
        u = us_ref[:, :SSM_W].reshape(nb, tl, SSM_W)
        ut_ref[...] = pltpu.einshape("btc->tbc", u).reshape(m, SSM_W).astype(BF16)

    def ssm_expand(r):
        res = jnp.dot(ut_ref[:, r * LANES:(r + 1) * LANES], bmat_ref[r], preferred_element_type=F32)
        bu_ref[:, r * half:(r + 1) * half] = res[:, :half]
        bu_ref[:, N_STATE + r * half:N_STATE + (r + 1) * half] = res[:, half:]

    def ssm_scan(r):
        c0 = r * half
        a_re = jnp.broadcast_to(lam_ref[0:1, c0:c0 + half], (nb, half))
        a_im = jnp.broadcast_to(lam_ref[1:2, c0:c0 + half], (nb, half))
        s_re = st_ref[:, c0:c0 + half]
        s_im = st_ref[:, N_STATE + c0:N_STATE + c0 + half]
        for t in range(tl):
            r0 = t * nb
            n_re = a_re * s_re - a_im * s_im + bu_ref[r0:r0 + nb, c0:c0 + half]
            n_im = a_re * s_im + a_im * s_re + bu_ref[r0:r0 + nb, N_STATE + c0:N_STATE + c0 + half]
            bu_ref[r0:r0 + nb, c0:c0 + half] = n_re
            bu_ref[r0:r0 + nb, N_STATE + c0:N_STATE + c0 + half] = n_im
            s_re, s_im = n_re, n_im
        st_ref[:, c0:c0 + half] = s_re
        st_ref[:, N_STATE + c0:N_STATE + c0 + half] = s_im

    def ssm_contract(r):
        x_re = bu_ref[:, r * half:(r + 1) * half].astype(BF16)
        x_im = bu_ref[:, N_STATE + r * half:N_STATE + (r + 1) * half].astype(BF16)
        yt_ref[:, r * LANES:(r + 1) * LANES] = (jnp.dot(x_re, cre_ref[r], preferred_element_type=F32)
                                               + jnp.dot(x_im, cim_ref[r], preferred_element_type=F32))

    def ssm_output():
        y = pltpu.einshape("tbc->btc", yt_ref[...].reshape(tl, nb, SSM_W)).reshape(m, SSM_W)
        y = _gelu_tanh(y + dskip_ref[...] * us_ref[:, :SSM_W])
        gl = jnp.dot(y.astype(BF16), wglu_ref[...], preferred_element_type=F32) + bglu_ref[...]
        y = y * _sigmoid(gl)
        ys_ref[...] = (y * _silu(us_ref[:, SSM_W:])).reshape(nb, tl, SSM_W).astype(BF16)

    def pool_project():
        up = jnp.dot(hb_ref[...], wpl_ref[...], preferred_element_type=F32)
        zp_ref[...] = up[:, POOL_W:]
        pext_ref[:, POOL_HALO:POOL_HALO + tl, :] = up[:, :POOL_W].reshape(nb, tl, POOL_W)

    def pool_mix():
        tpos = i * tl + lax.broadcasted_iota(jnp.int32, (1, tl, 1), 1)
        groups = []
        for gi, w in enumerate(POOL_WINDOWS):
            cs = slice(gi * POOL_GW, (gi + 1) * POOL_GW)
            ext = pext_ref[:, :, cs]
            tok = ext[:, POOL_HALO:, :]
            ssum = ext
            k = 1
            while k < w:
                ssum = ssum + pltpu.roll(ssum, k, axis=1)
                k *= 2
            cnt = jnp.minimum(tpos + 1, w).astype(F32)
            pooled = ssum[:, POOL_HALO:, :] / cnt - tok
            groups.append(jnp.dot(pooled.reshape(m, POOL_GW).astype(BF16), wpool_ref[gi],
                                  preferred_element_type=F32))
        y = jnp.concatenate(groups, axis=1) * pscale_ref[...]
        yp_ref[...] = (y * _silu(zp_ref[...])).reshape(nb, tl, POOL_W).astype(BF16)
        pext_ref[:, 0:POOL_HALO, :] = pext_ref[:, tl:tl + POOL_HALO, :]

    P = functools.partial
    schedule = (norm, att_project, pool_project, att_scores, ssm_project,
                P(att_softmax, 0), P(ssm_expand, 0), P(att_softmax, 1), P(ssm_expand, 1),
                P(ssm_scan, 0), P(ssm_expand, 2), P(att_softmax, 2), P(ssm_scan, 1), P(ssm_expand, 3),
                P(att_softmax, 3), P(ssm_contract, 0), P(ssm_scan, 2), P(ssm_contract, 1), pool_mix,
                P(ssm_scan, 3), att_values, P(ssm_contract, 2), P(ssm_contract, 3), ssm_output)
    for phase in schedule:
        phase()


def _mixers(x, mod, norm_g, sinks, wts):
    nb, seq, d = x.shape
    tl = CHUNK
    m = nb * tl
    consts = (wts["w_att"], wts["w_ssm"], wts["w_pl"], wts["bmat"], wts["cre"],
              wts["cim"], wts["lam"], wts["dskip"], wts["w_glu"], wts["b_glu"], wts["w_pool"], wts["pscale"])
    out_spec = pl.BlockSpec((nb, tl, SSM_W), lambda i: (0, i, 0))
    out_shape = jax.ShapeDtypeStruct((nb, seq, SSM_W), BF16)
    return pl.pallas_call(
        functools.partial(_mixer_kernel, nb=nb),
        grid=(seq // tl,),
        in_specs=[pl.BlockSpec(memory_space=pltpu.SMEM),
                  pl.BlockSpec((nb, tl, d), lambda i: (0, i, 0)),
                  _const_spec(mod.shape),
                  _const_spec(norm_g.shape)] + [_const_spec(c.shape) for c in consts],
        out_specs=[out_spec] * 3,
        out_shape=[out_shape] * 3,
        scratch_shapes=[pltpu.VMEM((m, d), BF16),
                        pltpu.VMEM((m, 2 * ATT_W + 2 * KV_W), F32),
                        pltpu.VMEM((nb, N_HEADS * tl, NK), F32),
                        pltpu.VMEM((nb, N_HEADS * tl, NK), BF16),
                        pltpu.VMEM((4, nb, NK, LANES), BF16),
                        pltpu.VMEM((4, nb, NK, LANES), BF16),
                        pltpu.VMEM((m, 2 * SSM_W), F32),
                        pltpu.VMEM((m, SSM_W), BF16),
                        pltpu.VMEM((m, 2 * N_STATE), F32),
                        pltpu.VMEM((nb, 2 * N_STATE), F32),
                        pltpu.VMEM((m, SSM_W), F32),
                        pltpu.VMEM((nb, POOL_HALO + tl, POOL_W), F32),
                        pltpu.VMEM((m, POOL_W), F32)],
        compiler_params=pltpu.CompilerParams(dimension_semantics=("arbitrary",), vmem_limit_bytes=VMEM_LIMIT),
        name="mixers",
    )(sinks, x, mod, norm_g, *consts)


def _merge_kernel(x_ref, mod_ref, ng_ref, ya_ref, ys_ref, yp_ref, wg_ref, wba_ref, wbs_ref, wbp_ref, wout_ref,
                  fg_ref, o_ref, *, final):
    x = x_ref[0]
    ms = jnp.mean(x * x, axis=-1, keepdims=True)
    gain = ng_ref[...] * (1.0 + mod_ref[1, 0])
    hb = (x * lax.rsqrt(ms + EPS) * gain + mod_ref[0, 0]).astype(BF16)
    merged = None
    for j, (y_ref, wb_ref) in enumerate(((ya_ref, wba_ref), (ys_ref, wbs_ref), (yp_ref, wbp_ref))):
        g = jnp.dot(hb, wg_ref[:, j * D_MODEL:(j + 1) * D_MODEL], preferred_element_type=F32)
        p = jnp.dot(y_ref[0], wb_ref[...], preferred_element_type=F32)
        term = _sigmoid(g) * p
        merged = term if merged is None else merged + term
    out = jnp.dot(merged.astype(BF16), wout_ref[...], preferred_element_type=F32)
    xn = x + mod_ref[2, 0] * out
    if final:
        ms = jnp.mean(xn * xn, axis=-1, keepdims=True)
        xn = xn * lax.rsqrt(ms + EPS) * fg_ref[...]
    o_ref[0] = xn


def _merge(x, mod, norm_g, ya, ys, yp, wts, final_g, final):
    nb, seq, d = x.shape
    tm = MERGE_ROWS
    row = lambda w: pl.BlockSpec((1, tm, w), lambda b, i: (b, i, 0))
    return pl.pallas_call(
        functools.partial(_merge_kernel, final=final),
        grid=(nb, seq // tm),
        in_specs=[row(d),
                  pl.BlockSpec((3, 1, 1, d), lambda b, i: (0, b, 0, 0)),
                  _const_spec(norm_g.shape),
                  row(ATT_W), row(SSM_W), row(POOL_W),
                  _const_spec(wts["w_g"].shape), _const_spec(wts["w_br_att"].shape),
                  _const_spec(wts["w_br_ssm"].shape), _const_spec(wts["w_br_pool"].shape),
                  _const_spec(wts["w_out"].shape), _const_spec(final_g.shape)],
        out_specs=row(d),
        out_shape=jax.ShapeDtypeStruct(x.shape, x.dtype),
        compiler_params=pltpu.CompilerParams(dimension_semantics=("arbitrary", "arbitrary"),
                                             vmem_limit_bytes=VMEM_LIMIT),
        name="merge",
    )(x, mod, norm_g, ya, ys, yp, wts["w_g"], wts["w_br_att"], wts["w_br_ssm"], wts["w_br_pool"], wts["w_out"],
      final_g)


def _block_diag(blocks):
    n, r, c = blocks.shape
    eye = jnp.eye(n, dtype=blocks.dtype)
    return (blocks[:, :, None, :] * eye[:, None, :, None]).reshape(n * r, n * c)


def _layer_weights(li, nb, w_in, lam_re, lam_im, coef_re, coef_im, ssm_b_re, ssm_b_im, ssm_c_re, ssm_c_im, ssm_d,
                   w_glu, b_glu, w_pool, pool_scale, w_br_att, w_br_ssm, w_br_pool, w_out):
    w = w_in[li]
    o_k, o_v, o_us, o_up, o_za, o_zs, o_zp, o_g = 512, 640, 768, 1280, 1792, 2304, 2816, 3328
    wts = {}
    wts["w_att"] = jnp.concatenate([w[:, :o_k] * (LOG2E / math.sqrt(HEAD_DIM)), w[:, o_k:o_us], w[:, o_za:o_zs]],
                                   axis=1).astype(BF16)
    wts["w_ssm"] = jnp.concatenate([w[:, o_us:o_up], w[:, o_zs:o_zp]], axis=1).astype(BF16)
    wts["w_pl"] = jnp.concatenate([w[:, o_up:o_za], w[:, o_zp:o_g]], axis=1).astype(BF16)
    wts["w_g"] = w[:, o_g:].astype(BF16)
    cr = coef_re[li][:, :, None]
    ci = coef_im[li][:, :, None]
    bb_re = cr * ssm_b_re[li] - ci * ssm_b_im[li]
    bb_im = cr * ssm_b_im[li] + ci * ssm_b_re[li]
    gpb = LANES // SSM_GROUP
    nblk = SSM_GROUPS // gpb
    bre = jnp.swapaxes(bb_re, 1, 2).reshape(nblk, gpb, SSM_GROUP, SSM_STATE)
    bim = jnp.swapaxes(bb_im, 1, 2).reshape(nblk, gpb, SSM_GROUP, SSM_STATE)
    wts["bmat"] = jnp.stack([jnp.concatenate([_block_diag(bre[r]), _block_diag(bim[r])], axis=1)
                             for r in range(nblk)]).astype(BF16)
    cre = jnp.swapaxes(ssm_c_re[li], 1, 2).reshape(nblk, gpb, SSM_STATE, SSM_GROUP)
    cim = jnp.swapaxes(-ssm_c_im[li], 1, 2).reshape(nblk, gpb, SSM_STATE, SSM_GROUP)
    wts["cre"] = jnp.stack([_block_diag(cre[r]) for r in range(nblk)]).astype(BF16)
    wts["cim"] = jnp.stack([_block_diag(cim[r]) for r in range(nblk)]).astype(BF16)
    wts["lam"] = jnp.stack([lam_re[li].reshape(N_STATE), lam_im[li].reshape(N_STATE)])
    wts["dskip"] = ssm_d[li].reshape(1, SSM_W)
    wts["w_glu"] = w_glu[li].astype(BF16)
    wts["b_glu"] = b_glu[li].reshape(1, SSM_W)
    wts["w_pool"] = w_pool[li].astype(BF16)
    wts["pscale"] = pool_scale[li].reshape(1, POOL_W)
    wts["w_br_att"] = w_br_att[li].astype(BF16)
    wts["w_br_ssm"] = w_br_ssm[li].astype(BF16)
    wts["w_br_pool"] = w_br_pool[li].astype(BF16)
    wts["w_out"] = w_out[li].astype(BF16)
    return wts


def kernel(x, c, norm_g, w_ada, b_ada, w_in, attn_sinks, ssm_a_re, ssm_a_im, ssm_log_dt, ssm_b_re, ssm_b_im,
           ssm_c_re, ssm_c_im, ssm_d, w_glu, b_glu, w_pool, pool_scale, w_br_att, w_br_ssm, w_br_pool, w_out,
           final_g):
    nb, seq, d = x.shape
    depth = w_in.shape[0]
    assert d == D_MODEL and nb == SUBLANES and seq % MERGE_ROWS == 0
    mod_all = _modulation(c, w_ada, b_ada)
    lam_re, lam_im, coef_re, coef_im = _ssm_prep(ssm_a_re, ssm_a_im, ssm_log_dt)
    fg = final_g.reshape(1, d)
    for li in range(depth):
        mod = jnp.transpose(mod_all[li].reshape(nb, 3, d), (1, 0, 2)).reshape(3, nb, 1, d)
        ng = norm_g[li].reshape(1, d)
        wts = _layer_weights(li, nb, w_in, lam_re, lam_im, coef_re, coef_im, ssm_b_re, ssm_b_im, ssm_c_re,
                             ssm_c_im, ssm_d, w_glu, b_glu, w_pool, pool_scale, w_br_att, w_br_ssm, w_br_pool,
                             w_out)
        ya, ys, yp = _mixers(x, mod, ng, attn_sinks[li], wts)
        x = _merge(x, mod, ng, ya, ys, yp, wts, fg, final=(li == depth - 1))
    return x
```

```python
import functools
import math

import jax
import jax.numpy as jnp
from jax import lax
from jax.experimental import pallas as pl
from jax.experimental.pallas import tpu as pltpu

F32 = jnp.float32
BF16 = jnp.bfloat16

D_MODEL = 1024
CHUNK = 64
N_HEADS = 8
N_KV_HEADS = 2
HEAD_DIM = 64
WIN_CHUNKS = 2
NK = (WIN_CHUNKS + 1) * CHUNK
ATT_W = N_HEADS * HEAD_DIM
KV_W = N_KV_HEADS * HEAD_DIM
SSM_W = 512
SSM_GROUP = 16
SSM_GROUPS = 32
SSM_STATE = 64
N_STATE = SSM_GROUPS * SSM_STATE
POOL_W = 512
POOL_WINDOWS = (2, 4, 8, 16)
POOL_GW = 128
POOL_HALO = 16
MERGE_ROWS = 1024
MOD_COLS = 512
EPS = 1e-6
NEG_INF = -1e30
LOG2E = math.log2(math.e)
LANES = 128
SUBLANES = 8
VMEM_LIMIT = 58 * 1024 * 1024


def _sigmoid(x):
    return 0.5 * jnp.tanh(0.5 * x) + 0.5


def _silu(x):
    return x * _sigmoid(x)


def _gelu_tanh(x):
    return 0.5 * x * (1.0 + jnp.tanh(math.sqrt(2.0 / math.pi) * (x + 0.044715 * (x * x * x))))


def _const_spec(shape):
    nd = len(shape)
    return pl.BlockSpec(shape, lambda *_: (0,) * nd, pipeline_mode=pl.Buffered(1))


def _layer_spec(shape, li):
    nd = len(shape)
    return pl.BlockSpec((None,) + tuple(shape[1:]), lambda *_: (li,) + (0,) * (nd - 1),
                        pipeline_mode=pl.Buffered(1))


def _mod_kernel(c_ref, w_ref, b_ref, o_ref):
    ca = _silu(c_ref[...])
    o_ref[0] = jnp.dot(ca, w_ref[0], preferred_element_type=F32, precision=lax.Precision.HIGHEST) + b_ref[0]


def _modulation(c, w_ada, b_ada):
    depth, d, n3 = w_ada.shape
    b = c.shape[0]
    bw = MOD_COLS
    return pl.pallas_call(
        _mod_kernel,
        grid=(depth, n3 // bw),
        in_specs=[pl.BlockSpec((b, d), lambda l, j: (0, 0)),
                  pl.BlockSpec((1, d, bw), lambda l, j: (l, 0, j)),
                  pl.BlockSpec((1, 1, bw), lambda l, j: (l, 0, j))],
        out_specs=pl.BlockSpec((1, b, bw), lambda l, j: (l, 0, j)),
        out_shape=jax.ShapeDtypeStruct((depth, b, n3), F32),
        name="adaln_mod",
    )(c, w_ada, b_ada.reshape(depth, 1, n3))


def _ssm_prep_kernel(are_ref, aim_ref, ldt_ref, lre_ref, lim_ref, cre_ref, cim_ref):
    a_re = are_ref[0]
    a_im = aim_ref[0]
    dt = jnp.exp(ldt_ref[0])
    mag = jnp.exp(a_re * dt)
    ang = a_im * dt
    l_re = mag * jnp.cos(ang)
    l_im = mag * jnp.sin(ang)
    n_re = l_re - 1.0
    den = a_re * a_re + a_im * a_im
    lre_ref[0] = l_re
    lim_ref[0] = l_im
    cre_ref[0] = (n_re * a_re + l_im * a_im) / den
    cim_ref[0] = (l_im * a_re - n_re * a_im) / den


def _ssm_prep(a_re, a_im, log_dt):
    depth, g, p = a_re.shape
    spec = pl.BlockSpec((1, g, p), lambda l: (l, 0, 0))
    out = jax.ShapeDtypeStruct((depth, g, p), F32)
    return pl.pallas_call(
        _ssm_prep_kernel,
        grid=(depth,),
        in_specs=[spec, spec, pl.BlockSpec((1, g, 1), lambda l: (l, 0, 0))],
        out_specs=[spec] * 4,
        out_shape=[out] * 4,
        name="ssm_discretise",
    )(a_re, a_im, log_dt.reshape(depth, g, 1))


def _mixer_kernel(sink_ref, x_ref, mod_ref, ng_ref, watt_ref, wssm_ref, wpl_ref,
                  bmat_ref, cre_ref, cim_ref, lam_ref, dskip_ref, wglu_ref, bglu_ref, wpool_ref, pscale_ref,
                  ya_ref, ys_ref, yp_ref,
                  hb_ref, att_ref, s_ref, p_ref, kring_ref, vring_ref, us_ref, ut_ref, bu_ref, st_ref, yt_ref,
                  pext_ref, zp_ref,
                  *, nb, li):
    i = pl.program_id(0)
    tl = CHUNK
    m = nb * tl
    half = N_STATE // 4
    hpk = N_HEADS // N_KV_HEADS
    rb = 2 * tl
    blk_heads = [(kv * hpk + e, kv * hpk + 2 + e) for kv in range(N_KV_HEADS) for e in range(2)]

    @pl.when(i == 0)
    def _init():
        kring_ref[...] = jnp.zeros_like(kring_ref)
        vring_ref[...] = jnp.zeros_like(vring_ref)
        st_ref[...] = jnp.zeros_like(st_ref)
        pext_ref[:, 0:POOL_HALO, :] = jnp.zeros((nb, POOL_HALO, POOL_W), F32)

    def norm():
        x = x_ref[...]
        ms = jnp.mean(x * x, axis=-1, keepdims=True)
        gain = ng_ref[...] * (1.0 + mod_ref[1])
        h = x * lax.rsqrt(ms + EPS) * gain + mod_ref[0]
        hb_ref[...] = h.reshape(m, D_MODEL).astype(BF16)

    def att_project():
        att_ref[...] = jnp.dot(hb_ref[...], watt_ref[...], preferred_element_type=F32)
        lane = lax.broadcasted_iota(jnp.int32, (m, LANES), 1)
        lo = lane < HEAD_DIM
        zero = jnp.zeros((m, LANES), F32)
        for ring, c0 in ((kring_ref, ATT_W), (vring_ref, ATT_W + KV_W)):
            new = att_ref[:, c0:c0 + KV_W]
            swp = pltpu.roll(new, HEAD_DIM, axis=1)
            variants = (jnp.where(lo, new, zero),
                        jnp.where(lo, zero, swp),
                        jnp.where(lo, swp, zero),
                        jnp.where(lo, zero, new))
            for var in range(4):
                for b in range(nb):
                    ring[var, b, 0:NK - CHUNK, :] = ring[var, b, CHUNK:NK, :]
                ring[var, :, NK - CHUNK:NK, :] = variants[var].reshape(nb, tl, LANES).astype(BF16)

    def att_scores():
        for b in range(nb):
            for kv in range(N_KV_HEADS):
                qq = jnp.concatenate([att_ref[b * tl:(b + 1) * tl, (2 * kv + j) * LANES:(2 * kv + j + 1) * LANES]
                                      for j in range(2)], axis=0).astype(BF16)
                for e in range(2):
                    blk = 2 * kv + e
                    s_ref[b, blk * rb:(blk + 1) * rb, :] = lax.dot_general(
                        qq, kring_ref[blk, b], (((1,), (1,)), ((), ())), preferred_element_type=F32)

    def att_softmax(blk):
        ha, hb_ = blk_heads[blk]
        row = lax.broadcasted_iota(jnp.int32, (rb, NK), 0)
        kj = lax.broadcasted_iota(jnp.int32, (rb, NK), 1)
        qi = jnp.where(row < tl, row, row - tl)
        dist = jnp.abs(qi + WIN_CHUNKS * CHUNK - kj).astype(F32)
        valid = (i * CHUNK + kj) >= WIN_CHUNKS * CHUNK
        first = lax.broadcasted_iota(jnp.int32, (rb, 1), 0) < tl
        slope = jnp.where(first, LOG2E * 2.0 ** (-(ha + 1)), LOG2E * 2.0 ** (-(hb_ + 1)))
        nbias = jnp.where(valid, -slope * dist, NEG_INF)
        sink = LOG2E * jnp.where(first, sink_ref[li, ha], sink_ref[li, hb_])
        for b in range(nb):
            s = s_ref[b, blk * rb:(blk + 1) * rb, :] + nbias
            mx = jnp.maximum(jnp.max(s, axis=-1, keepdims=True), sink)
            p = jnp.exp2(s - mx)
            den = jnp.sum(p, axis=-1, keepdims=True) + jnp.exp2(sink - mx)
            p_ref[b, blk * rb:(blk + 1) * rb, :] = (p * (1.0 / den)).astype(BF16)

    def att_values():
        for b in range(nb):
            for kv in range(N_KV_HEADS):
                o = None
                for e in range(2):
                    blk = 2 * kv + e
                    t = jnp.dot(p_ref[b, blk * rb:(blk + 1) * rb, :], vring_ref[blk, b],
                                preferred_element_type=F32)
                    o = t if o is None else o + t
                for j in range(2):
                    c0 = (2 * kv + j) * LANES
                    z = att_ref[b * tl:(b + 1) * tl, ATT_W + 2 * KV_W + c0:ATT_W + 2 * KV_W + c0 + LANES]
                    ya_ref[b, :, c0:c0 + LANES] = (o[j * tl:(j + 1) * tl] * _silu(z)).astype(BF16)

    def ssm_project():
        us_ref[...] = jnp.dot(hb_ref[...], wssm_ref[...], preferred_element_type=F32)
        u = us_ref[:, :SSM_W].reshape(nb, tl, SSM_W)
        ut_ref[...] = jnp.swapaxes(u, 0, 1).reshape(m, SSM_W).astype(BF16)

    def ssm_expand(r):
        res = jnp.dot(ut_ref[:, r * LANES:(r + 1) * LANES], bmat_ref[r], preferred_element_type=F32)
        bu_ref[:, r * half:(r + 1) * half] = res[:, :half]
        bu_ref[:, N_STATE + r * half:N_STATE + (r + 1) * half] = res[:, half:]

    def ssm_scan(r):
        c0 = r * half
        a_re = jnp.broadcast_to(lam_ref[0:1, c0:c0 + half], (nb, half))
        a_im = jnp.broadcast_to(lam_ref[1:2, c0:c0 + half], (nb, half))
        s_re = st_ref[:, c0:c0 + half]
        s_im = st_ref[:, N_STATE + c0:N_STATE + c0 + half]
        for t in range(tl):
            r0 = t * nb
            n_re = a_re * s_re - a_im * s_im + bu_ref[r0:r0 + nb, c0:c0 + half]
            n_im = a_re * s_im + a_im * s_re + bu_ref[r0:r0 + nb, N_STATE + c0:N_STATE + c0 + half]
            bu_ref[r0:r0 + nb, c0:c0 + half] = n_re
            bu_ref[r0:r0 + nb, N_STATE + c0:N_STATE + c0 + half] = n_im
            s_re, s_im = n_re, n_im
        st_ref[:, c0:c0 + half] = s_re
        st_ref[:, N_STATE + c0:N_STATE + c0 + half] = s_im

    def ssm_contract(r):
        x_re = bu_ref[:, r * half:(r + 1) * half].astype(BF16)
        x_im = bu_ref[:, N_STATE + r * half:N_STATE + (r + 1) * half].astype(BF16)
        yt_ref[:, r * LANES:(r + 1) * LANES] = (jnp.dot(x_re, cre_ref[r], preferred_element_type=F32)
                                               + jnp.dot(x_im, cim_ref[r], preferred_element_type=F32))

    def ssm_output():
        y = jnp.swapaxes(yt_ref[...].reshape(tl, nb, SSM_W), 0, 1).reshape(m, SSM_W)
        y = _gelu_tanh(y + dskip_ref[...] * us_ref[:, :SSM_W])
        gl = jnp.dot(y.astype(BF16), wglu_ref[...], preferred_element_type=F32) + bglu_ref[...]
        y = y * _sigmoid(gl)
        ys_ref[...] = (y * _silu(us_ref[:, SSM_W:])).reshape(nb, tl, SSM_W).astype(BF16)

    def pool_project():
        up = jnp.dot(hb_ref[...], wpl_ref[...], preferred_element_type=F32)
        zp_ref[...] = up[:, POOL_W:]
        pext_ref[:, POOL_HALO:POOL_HALO + tl, :] = up[:, :POOL_W].reshape(nb, tl, POOL_W)

    def pool_mix():
        tpos = i * tl + lax.broadcasted_iota(jnp.int32, (1, tl, 1), 1)
        groups = []
        for gi, w in enumerate(POOL_WINDOWS):
            cs = slice(gi * POOL_GW, (gi + 1) * POOL_GW)
            ext = pext_ref[:, :, cs]
            tok = ext[:, POOL_HALO:, :]
            ssum = ext
            k = 1
            while k < w:
                ssum = ssum + pltpu.roll(ssum, k, axis=1)
                k *= 2
            cnt = jnp.minimum(tpos + 1, w).astype(F32)
            pooled = ssum[:, POOL_HALO:, :] / cnt - tok
            groups.append(jnp.dot(pooled.reshape(m, POOL_GW).astype(BF16), wpool_ref[gi],
                                  preferred_element_type=F32))
        y = jnp.concatenate(groups, axis=1) * pscale_ref[...]
        yp_ref[...] = (y * _silu(zp_ref[...])).reshape(nb, tl, POOL_W).astype(BF16)
        pext_ref[:, 0:POOL_HALO, :] = pext_ref[:, tl:tl + POOL_HALO, :]

    P = functools.partial
    schedule = (norm, att_project, pool_project, att_scores, ssm_project,
                P(att_softmax, 0), P(ssm_expand, 0), P(att_softmax, 1), P(ssm_expand, 1),
                P(ssm_scan, 0), P(ssm_expand, 2), P(att_softmax, 2), P(ssm_scan, 1), P(ssm_expand, 3),
                P(att_softmax, 3), P(ssm_contract, 0), P(ssm_scan, 2), P(ssm_contract, 1), pool_mix,
                P(ssm_scan, 3), att_values, P(ssm_contract, 2), P(ssm_contract, 3), ssm_output)
    for phase in schedule:
        phase()


def _mixers(x, li, mod, norm_g, sinks, wts):
    nb, seq, d = x.shape
    tl = CHUNK
    m = nb * tl
    consts = (wts["w_att"], wts["w_ssm"], wts["w_pl"], wts["bmat"], wts["cre"],
              wts["cim"], wts["lam"], wts["dskip"], wts["w_glu"], wts["b_glu"], wts["w_pool"], wts["pscale"])
    out_spec = pl.BlockSpec((nb, tl, SSM_W), lambda i: (0, i, 0))
    out_shape = jax.ShapeDtypeStruct((nb, seq, SSM_W), BF16)
    return pl.pallas_call(
        functools.partial(_mixer_kernel, nb=nb, li=li),
        grid=(seq // tl,),
        in_specs=[pl.BlockSpec(memory_space=pltpu.SMEM),
                  pl.BlockSpec((nb, tl, d), lambda i: (0, i, 0)),
                  _layer_spec(mod.shape, li),
                  _layer_spec(norm_g.shape, li)] + [_layer_spec(c.shape, li) for c in consts],
        out_specs=[out_spec] * 3,
        out_shape=[out_shape] * 3,
        scratch_shapes=[pltpu.VMEM((m, d), BF16),
                        pltpu.VMEM((m, 2 * ATT_W + 2 * KV_W), F32),
                        pltpu.VMEM((nb, N_HEADS * tl, NK), F32),
                        pltpu.VMEM((nb, N_HEADS * tl, NK), BF16),
                        pltpu.VMEM((4, nb, NK, LANES), BF16),
                        pltpu.VMEM((4, nb, NK, LANES), BF16),
                        pltpu.VMEM((m, 2 * SSM_W), F32),
                        pltpu.VMEM((m, SSM_W), BF16),
                        pltpu.VMEM((m, 2 * N_STATE), F32),
                        pltpu.VMEM((nb, 2 * N_STATE), F32),
                        pltpu.VMEM((m, SSM_W), F32),
                        pltpu.VMEM((nb, POOL_HALO + tl, POOL_W), F32),
                        pltpu.VMEM((m, POOL_W), F32)],
        compiler_params=pltpu.CompilerParams(dimension_semantics=("arbitrary",), vmem_limit_bytes=VMEM_LIMIT),
        name="mixers",
    )(sinks, x, mod, norm_g, *consts)


def _merge_kernel(x_ref, mod_ref, ng_ref, ya_ref, ys_ref, yp_ref, wg_ref, wba_ref, wbs_ref, wbp_ref, wout_ref,
                  fg_ref, o_ref, *, final):
    x = x_ref[0]
    ms = jnp.mean(x * x, axis=-1, keepdims=True)
    gain = ng_ref[...] * (1.0 + mod_ref[1, 0])
    hb = (x * lax.rsqrt(ms + EPS) * gain + mod_ref[0, 0]).astype(BF16)
    merged = None
    for j, (y_ref, wb_ref) in enumerate(((ya_ref, wba_ref), (ys_ref, wbs_ref), (yp_ref, wbp_ref))):
        g = jnp.dot(hb, wg_ref[:, j * D_MODEL:(j + 1) * D_MODEL], preferred_element_type=F32)
        p = jnp.dot(y_ref[0], wb_ref[...], preferred_element_type=F32)
        term = _sigmoid(g) * p
        merged = term if merged is None else merged + term
    out = jnp.dot(merged.astype(BF16), wout_ref[...], preferred_element_type=F32)
    xn = x + mod_ref[2, 0] * out
    if final:
        ms = jnp.mean(xn * xn, axis=-1, keepdims=True)
        xn = xn * lax.rsqrt(ms + EPS) * fg_ref[...]
    o_ref[0] = xn


def _merge(x, li, mod, norm_g, ya, ys, yp, wts, final_g, final):
    nb, seq, d = x.shape
    tm = MERGE_ROWS
    row = lambda w: pl.BlockSpec((1, tm, w), lambda b, i: (b, i, 0))
    return pl.pallas_call(
        functools.partial(_merge_kernel, final=final),
        grid=(nb, seq // tm),
        in_specs=[row(d),
                  pl.BlockSpec((None, 3, 1, 1, d), lambda b, i: (li, 0, b, 0, 0)),
                  _layer_spec(norm_g.shape, li),
                  row(ATT_W), row(SSM_W), row(POOL_W),
                  _layer_spec(wts["w_g"].shape, li), _layer_spec(wts["w_br_att"].shape, li),
                  _layer_spec(wts["w_br_ssm"].shape, li), _layer_spec(wts["w_br_pool"].shape, li),
                  _layer_spec(wts["w_out"].shape, li), _const_spec(final_g.shape)],
        out_specs=row(d),
        out_shape=jax.ShapeDtypeStruct(x.shape, x.dtype),
        compiler_params=pltpu.CompilerParams(dimension_semantics=("arbitrary", "arbitrary"),
                                             vmem_limit_bytes=VMEM_LIMIT),
        name="merge",
    )(x, mod, norm_g, ya, ys, yp, wts["w_g"], wts["w_br_att"], wts["w_br_ssm"], wts["w_br_pool"], wts["w_out"],
      final_g)


def _block_diag(blocks):
    *lead, n, r, c = blocks.shape
    eye = jnp.eye(n, dtype=blocks.dtype)
    return (blocks[..., :, :, None, :] * eye[:, None, :, None]).reshape(*lead, n * r, n * c)


def _stacked_weights(w_in, lam_re, lam_im, coef_re, coef_im, ssm_b_re, ssm_b_im, ssm_c_re, ssm_c_im, ssm_d,
                     w_glu, b_glu, w_pool, pool_scale, w_br_att, w_br_ssm, w_br_pool, w_out):
    depth = w_in.shape[0]
    o_k, o_us, o_up, o_za, o_zs, o_zp, o_g = 512, 768, 1280, 1792, 2304, 2816, 3328
    wts = {}
    wts["w_att"] = jnp.concatenate([w_in[:, :, :o_k] * (LOG2E / math.sqrt(HEAD_DIM)), w_in[:, :, o_k:o_us],
                                    w_in[:, :, o_za:o_zs]], axis=2).astype(BF16)
    wts["w_ssm"] = jnp.concatenate([w_in[:, :, o_us:o_up], w_in[:, :, o_zs:o_zp]], axis=2).astype(BF16)
    wts["w_pl"] = jnp.concatenate([w_in[:, :, o_up:o_za], w_in[:, :, o_zp:o_g]], axis=2).astype(BF16)
    wts["w_g"] = w_in[:, :, o_g:].astype(BF16)
    cr = coef_re[..., None]
    ci = coef_im[..., None]
    bb_re = cr * ssm_b_re - ci * ssm_b_im
    bb_im = cr * ssm_b_im + ci * ssm_b_re
    gpb = LANES // SSM_GROUP
    nblk = SSM_GROUPS // gpb
    bre = jnp.swapaxes(bb_re, 2, 3).reshape(depth, nblk, gpb, SSM_GROUP, SSM_STATE)
    bim = jnp.swapaxes(bb_im, 2, 3).reshape(depth, nblk, gpb, SSM_GROUP, SSM_STATE)
    wts["bmat"] = jnp.concatenate([_block_diag(bre), _block_diag(bim)], axis=-1).astype(BF16)
    cre = jnp.swapaxes(ssm_c_re, 2, 3).reshape(depth, nblk, gpb, SSM_STATE, SSM_GROUP)
    cim = jnp.swapaxes(-ssm_c_im, 2, 3).reshape(depth, nblk, gpb, SSM_STATE, SSM_GROUP)
    wts["cre"] = _block_diag(cre).astype(BF16)
    wts["cim"] = _block_diag(cim).astype(BF16)
    wts["lam"] = jnp.stack([lam_re.reshape(depth, N_STATE), lam_im.reshape(depth, N_STATE)], axis=1)
    wts["dskip"] = ssm_d.reshape(depth, 1, SSM_W)
    wts["w_glu"] = w_glu.astype(BF16)
    wts["b_glu"] = b_glu.reshape(depth, 1, SSM_W)
    wts["w_pool"] = w_pool.astype(BF16)
    wts["pscale"] = pool_scale.reshape(depth, 1, POOL_W)
    wts["w_br_att"] = w_br_att.astype(BF16)
    wts["w_br_ssm"] = w_br_ssm.astype(BF16)
    wts["w_br_pool"] = w_br_pool.astype(BF16)
    wts["w_out"] = w_out.astype(BF16)
    return wts


def kernel(x, c, norm_g, w_ada, b_ada, w_in, attn_sinks, ssm_a_re, ssm_a_im, ssm_log_dt, ssm_b_re, ssm_b_im,
           ssm_c_re, ssm_c_im, ssm_d, w_glu, b_glu, w_pool, pool_scale, w_br_att, w_br_ssm, w_br_pool, w_out,
           final_g):
    nb, seq, d = x.shape
    depth = w_in.shape[0]
    assert d == D_MODEL and nb == SUBLANES and seq % MERGE_ROWS == 0
    mod_all = _modulation(c, w_ada, b_ada)
    lam_re, lam_im, coef_re, coef_im = _ssm_prep(ssm_a_re, ssm_a_im, ssm_log_dt)
    fg = final_g.reshape(1, d)
    mod = jnp.transpose(mod_all.reshape(depth, nb, 3, d), (0, 2, 1, 3)).reshape(depth, 3, nb, 1, d)
    ng = norm_g.reshape(depth, 1, d)
    wts = _stacked_weights(w_in, lam_re, lam_im, coef_re, coef_im, ssm_b_re, ssm_b_im, ssm_c_re, ssm_c_im, ssm_d,
                           w_glu, b_glu, w_pool, pool_scale, w_br_att, w_br_ssm, w_br_pool, w_out)
    for li in range(depth):
        ya, ys, yp = _mixers(x, li, mod, ng, attn_sinks, wts)
        x = _merge(x, li, mod, ng, ya, ys, yp, wts, fg, final=(li == depth - 1))
    return x
```

```python
import functools
import math

import jax
import jax.numpy as jnp
from jax import lax
from jax.experimental import pallas as pl
from jax.experimental.pallas import tpu as pltpu

F32 = jnp.float32
BF16 = jnp.bfloat16

D_MODEL = 1024
CHUNK = 64
N_HEADS = 8
N_KV_HEADS = 2
HEAD_DIM = 64
WIN_CHUNKS = 2
NK = (WIN_CHUNKS + 1) * CHUNK
ATT_W = N_HEADS * HEAD_DIM
KV_W = N_KV_HEADS * HEAD_DIM
SSM_W = 512
SSM_GROUP = 16
SSM_GROUPS = 32
SSM_STATE = 64
N_STATE = SSM_GROUPS * SSM_STATE
POOL_W = 512
POOL_WINDOWS = (2, 4, 8, 16)
POOL_GW = 128
POOL_HALO = 16
O_K, O_US, O_UP, O_ZA, O_ZS, O_ZP, O_G = 512, 768, 1280, 1792, 2304, 2816, 3328
MERGE_ROWS = 1024
MOD_COLS = 512
EPS = 1e-6
NEG_INF = -1e30
LOG2E = math.log2(math.e)
LANES = 128
SUBLANES = 8
VMEM_LIMIT = 58 * 1024 * 1024


def _sigmoid(x):
    return 0.5 * jnp.tanh(0.5 * x) + 0.5


def _silu(x):
    return x * _sigmoid(x)


def _gelu_tanh(x):
    return 0.5 * x * (1.0 + jnp.tanh(math.sqrt(2.0 / math.pi) * (x + 0.044715 * (x * x * x))))


def _const_spec(shape):
    nd = len(shape)
    return pl.BlockSpec(shape, lambda *_: (0,) * nd, pipeline_mode=pl.Buffered(1))


def _column_spec(shape, li, c0, width):
    return pl.BlockSpec((pl.Element(1), pl.Element(shape[1]), pl.Element(width)), lambda *_: (li, 0, c0),
                        pipeline_mode=pl.Buffered(1))


def _layer_spec(shape, li):
    nd = len(shape)
    return pl.BlockSpec((None,) + tuple(shape[1:]), lambda *_: (li,) + (0,) * (nd - 1),
                        pipeline_mode=pl.Buffered(1))


def _mod_kernel(c_ref, w_ref, b_ref, o_ref):
    ca = _silu(c_ref[...])
    o_ref[0] = jnp.dot(ca.astype(BF16), w_ref[0].astype(BF16), preferred_element_type=F32) + b_ref[0]


def _modulation(c, w_ada, b_ada):
    depth, d, n3 = w_ada.shape
    b = c.shape[0]
    bw = MOD_COLS
    return pl.pallas_call(
        _mod_kernel,
        grid=(depth, n3 // bw),
        in_specs=[pl.BlockSpec((b, d), lambda l, j: (0, 0)),
                  pl.BlockSpec((1, d, bw), lambda l, j: (l, 0, j)),
                  pl.BlockSpec((1, 1, bw), lambda l, j: (l, 0, j))],
        out_specs=pl.BlockSpec((1, b, bw), lambda l, j: (l, 0, j)),
        out_shape=jax.ShapeDtypeStruct((depth, b, n3), F32),
        name="adaln_mod",
    )(c, w_ada, b_ada.reshape(depth, 1, n3))


def _ssm_prep_kernel(are_ref, aim_ref, ldt_ref, lre_ref, lim_ref, cre_ref, cim_ref):
    a_re = are_ref[0]
    a_im = aim_ref[0]
    dt = jnp.exp(ldt_ref[0])
    mag = jnp.exp(a_re * dt)
    ang = a_im * dt
    l_re = mag * jnp.cos(ang)
    l_im = mag * jnp.sin(ang)
    n_re = l_re - 1.0
    den = a_re * a_re + a_im * a_im
    lre_ref[0] = l_re
    lim_ref[0] = l_im
    cre_ref[0] = (n_re * a_re + l_im * a_im) / den
    cim_ref[0] = (l_im * a_re - n_re * a_im) / den


def _ssm_prep(a_re, a_im, log_dt):
    depth, g, p = a_re.shape
    spec = pl.BlockSpec((1, g, p), lambda l: (l, 0, 0))
    out = jax.ShapeDtypeStruct((depth, g, p), F32)
    return pl.pallas_call(
        _ssm_prep_kernel,
        grid=(depth,),
        in_specs=[spec, spec, pl.BlockSpec((1, g, 1), lambda l: (l, 0, 0))],
        out_specs=[spec] * 4,
        out_shape=[out] * 4,
        name="ssm_discretise",
    )(a_re, a_im, log_dt.reshape(depth, g, 1))


def _mixer_kernel(sink_ref, x_ref, mod_ref, ng_ref, wq_ref, wkv_ref, wza_ref, wus_ref, wzs_ref, wup_ref, wzp_ref,
                  bmat_ref, cre_ref, cim_ref, lam_ref, dskip_ref, wglu_ref, bglu_ref, wpool_ref, pscale_ref,
                  ya_ref, ys_ref, yp_ref,
                  hb_ref, att_ref, s_ref, p_ref, kring_ref, vring_ref, us_ref, ut_ref, bu_ref, st_ref, yt_ref,
                  pext_ref, zp_ref,
                  *, nb, li):
    i = pl.program_id(0)
    tl = CHUNK
    m = nb * tl
    half = N_STATE // 4
    hpk = N_HEADS // N_KV_HEADS
    rb = 2 * tl
    blk_heads = [(kv * hpk + e, kv * hpk + 2 + e) for kv in range(N_KV_HEADS) for e in range(2)]

    @pl.when(i == 0)
    def _init():
        kring_ref[...] = jnp.zeros_like(kring_ref)
        vring_ref[...] = jnp.zeros_like(vring_ref)
        st_ref[...] = jnp.zeros_like(st_ref)
        pext_ref[:, 0:POOL_HALO, :] = jnp.zeros((nb, POOL_HALO, POOL_W), F32)

    def norm():
        x = x_ref[...]
        ms = jnp.mean(x * x, axis=-1, keepdims=True)
        gain = ng_ref[...] * (1.0 + mod_ref[1])
        h = x * lax.rsqrt(ms + EPS) * gain + mod_ref[0]
        hb_ref[...] = h.reshape(m, D_MODEL).astype(BF16)

    def att_project():
        hb = hb_ref[...]
        att_ref[:, :ATT_W] = jnp.dot(hb, wq_ref[0], preferred_element_type=F32) * (LOG2E / math.sqrt(HEAD_DIM))
        att_ref[:, ATT_W:ATT_W + 2 * KV_W] = jnp.dot(hb, wkv_ref[0], preferred_element_type=F32)
        att_ref[:, ATT_W + 2 * KV_W:] = jnp.dot(hb, wza_ref[0], preferred_element_type=F32)
        lane = lax.broadcasted_iota(jnp.int32, (m, LANES), 1)
        lo = lane < HEAD_DIM
        zero = jnp.zeros((m, LANES), F32)
        for ring, c0 in ((kring_ref, ATT_W), (vring_ref, ATT_W + KV_W)):
            new = att_ref[:, c0:c0 + KV_W]
            swp = pltpu.roll(new, HEAD_DIM, axis=1)
            variants = (jnp.where(lo, new, zero),
                        jnp.where(lo, zero, swp),
                        jnp.where(lo, swp, zero),
                        jnp.where(lo, zero, new))
            for var in range(4):
                for b in range(nb):
                    ring[var, b, 0:NK - CHUNK, :] = ring[var, b, CHUNK:NK, :]
                ring[var, :, NK - CHUNK:NK, :] = variants[var].reshape(nb, tl, LANES).astype(BF16)

    def att_scores():
        for b in range(nb):
            for kv in range(N_KV_HEADS):
                qq = jnp.concatenate([att_ref[b * tl:(b + 1) * tl, (2 * kv + j) * LANES:(2 * kv + j + 1) * LANES]
                                      for j in range(2)], axis=0).astype(BF16)
                for e in range(2):
                    blk = 2 * kv + e
                    s_ref[b, blk * rb:(blk + 1) * rb, :] = lax.dot_general(
                        qq, kring_ref[blk, b], (((1,), (1,)), ((), ())), preferred_element_type=F32)

    def att_softmax(blk):
        ha, hb_ = blk_heads[blk]
        row = lax.broadcasted_iota(jnp.int32, (rb, NK), 0)
        kj = lax.broadcasted_iota(jnp.int32, (rb, NK), 1)
        qi = jnp.where(row < tl, row, row - tl)
        dist = jnp.abs(qi + WIN_CHUNKS * CHUNK - kj).astype(F32)
        valid = (i * CHUNK + kj) >= WIN_CHUNKS * CHUNK
        first = lax.broadcasted_iota(jnp.int32, (rb, 1), 0) < tl
        slope = jnp.where(first, LOG2E * 2.0 ** (-(ha + 1)), LOG2E * 2.0 ** (-(hb_ + 1)))
        nbias = jnp.where(valid, -slope * dist, NEG_INF)
        sink = LOG2E * jnp.where(first, sink_ref[li, ha], sink_ref[li, hb_])
        for b in range(nb):
            s = s_ref[b, blk * rb:(blk + 1) * rb, :] + nbias
            mx = jnp.maximum(jnp.max(s, axis=-1, keepdims=True), sink)
            p = jnp.exp2(s - mx)
            den = jnp.sum(p, axis=-1, keepdims=True) + jnp.exp2(sink - mx)
            p_ref[b, blk * rb:(blk + 1) * rb, :] = (p * (1.0 / den)).astype(BF16)

    def att_values():
        for b in range(nb):
            for kv in range(N_KV_HEADS):
                o = None
                for e in range(2):
                    blk = 2 * kv + e
                    t = jnp.dot(p_ref[b, blk * rb:(blk + 1) * rb, :], vring_ref[blk, b],
                                preferred_element_type=F32)
                    o = t if o is None else o + t
                for j in range(2):
                    c0 = (2 * kv + j) * LANES
                    z = att_ref[b * tl:(b + 1) * tl, ATT_W + 2 * KV_W + c0:ATT_W + 2 * KV_W + c0 + LANES]
                    ya_ref[b, :, c0:c0 + LANES] = (o[j * tl:(j + 1) * tl] * _silu(z)).astype(BF16)

    def ssm_project():
        hb = hb_ref[...]
        us_ref[:, :SSM_W] = jnp.dot(hb, wus_ref[0], preferred_element_type=F32)
        us_ref[:, SSM_W:] = jnp.dot(hb, wzs_ref[0], preferred_element_type=F32)
        u = us_ref[:, :SSM_W].reshape(nb, tl, SSM_W)
        ut_ref[...] = jnp.swapaxes(u, 0, 1).reshape(m, SSM_W).astype(BF16)

    def ssm_expand(r):
        res = jnp.dot(ut_ref[:, r * LANES:(r + 1) * LANES], bmat_ref[r], preferred_element_type=F32)
        bu_ref[:, r * half:(r + 1) * half] = res[:, :half]
        bu_ref[:, N_STATE + r * half:N_STATE + (r + 1) * half] = res[:, half:]

    def ssm_scan(r):
        c0 = r * half
        a_re = jnp.broadcast_to(lam_ref[0:1, c0:c0 + half], (nb, half))
        a_im = jnp.broadcast_to(lam_ref[1:2, c0:c0 + half], (nb, half))
        s_re = st_ref[:, c0:c0 + half]
        s_im = st_ref[:, N_STATE + c0:N_STATE + c0 + half]
        for t in range(tl):
            r0 = t * nb
            n_re = a_re * s_re - a_im * s_im + bu_ref[r0:r0 + nb, c0:c0 + half]
            n_im = a_re * s_im + a_im * s_re + bu_ref[r0:r0 + nb, N_STATE + c0:N_STATE + c0 + half]
            bu_ref[r0:r0 + nb, c0:c0 + half] = n_re
            bu_ref[r0:r0 + nb, N_STATE + c0:N_STATE + c0 + half] = n_im
            s_re, s_im = n_re, n_im
        st_ref[:, c0:c0 + half] = s_re
        st_ref[:, N_STATE + c0:N_STATE + c0 + half] = s_im

    def ssm_contract(r):
        x_re = bu_ref[:, r * half:(r + 1) * half].astype(BF16)
        x_im = bu_ref[:, N_STATE + r * half:N_STATE + (r + 1) * half].astype(BF16)
        yt_ref[:, r * LANES:(r + 1) * LANES] = (jnp.dot(x_re, cre_ref[r], preferred_element_type=F32)
                                               + jnp.dot(x_im, cim_ref[r], preferred_element_type=F32))

    def ssm_output():
        y = jnp.swapaxes(yt_ref[...].reshape(tl, nb, SSM_W), 0, 1).reshape(m, SSM_W)
        y = _gelu_tanh(y + dskip_ref[...] * us_ref[:, :SSM_W])
        gl = jnp.dot(y.astype(BF16), wglu_ref[...], preferred_element_type=F32) + bglu_ref[...]
        y = y * _sigmoid(gl)
        ys_ref[...] = (y * _silu(us_ref[:, SSM_W:])).reshape(nb, tl, SSM_W).astype(BF16)

    def pool_project():
        hb = hb_ref[...]
        zp_ref[...] = jnp.dot(hb, wzp_ref[0], preferred_element_type=F32)
        pext_ref[:, POOL_HALO:POOL_HALO + tl, :] = jnp.dot(hb, wup_ref[0], preferred_element_type=F32
                                                           ).reshape(nb, tl, POOL_W)

    def pool_mix():
        tpos = i * tl + lax.broadcasted_iota(jnp.int32, (1, tl, 1), 1)
        groups = []
        for gi, w in enumerate(POOL_WINDOWS):
            cs = slice(gi * POOL_GW, (gi + 1) * POOL_GW)
            ext = pext_ref[:, :, cs]
            tok = ext[:, POOL_HALO:, :]
            ssum = ext
            k = 1
            while k < w:
                ssum = ssum + pltpu.roll(ssum, k, axis=1)
                k *= 2
            cnt = jnp.minimum(tpos + 1, w).astype(F32)
            pooled = ssum[:, POOL_HALO:, :] / cnt - tok
            groups.append(jnp.dot(pooled.reshape(m, POOL_GW).astype(BF16), wpool_ref[gi],
                                  preferred_element_type=F32))
        y = jnp.concatenate(groups, axis=1) * pscale_ref[...]
        yp_ref[...] = (y * _silu(zp_ref[...])).reshape(nb, tl, POOL_W).astype(BF16)
        pext_ref[:, 0:POOL_HALO, :] = pext_ref[:, tl:tl + POOL_HALO, :]

    P = functools.partial
    schedule = (norm, att_project, pool_project, att_scores, ssm_project,
                P(att_softmax, 0), P(ssm_expand, 0), P(att_softmax, 1), P(ssm_expand, 1),
                P(ssm_scan, 0), P(ssm_expand, 2), P(att_softmax, 2), P(ssm_scan, 1), P(ssm_expand, 3),
                P(att_softmax, 3), P(ssm_contract, 0), P(ssm_scan, 2), P(ssm_contract, 1), pool_mix,
                P(ssm_scan, 3), att_values, P(ssm_contract, 2), P(ssm_contract, 3), ssm_output)
    for phase in schedule:
        phase()


def _mixers(x, li, mod, norm_g, sinks, wts):
    nb, seq, d = x.shape
    tl = CHUNK
    m = nb * tl
    consts = (wts["bmat"], wts["cre"],
              wts["cim"], wts["lam"], wts["dskip"], wts["w_glu"], wts["b_glu"], wts["w_pool"], wts["pscale"])
    w_in = wts["w_in"]
    proj_cols = ((0, ATT_W), (O_K, 2 * KV_W), (O_ZA, ATT_W), (O_US, SSM_W), (O_ZS, SSM_W), (O_UP, POOL_W),
                 (O_ZP, POOL_W))
    out_spec = pl.BlockSpec((nb, tl, SSM_W), lambda i: (0, i, 0))
    out_shape = jax.ShapeDtypeStruct((nb, seq, SSM_W), BF16)
    return pl.pallas_call(
        functools.partial(_mixer_kernel, nb=nb, li=li),
        grid=(seq // tl,),
        in_specs=[pl.BlockSpec(memory_space=pltpu.SMEM),
                  pl.BlockSpec((nb, tl, d), lambda i: (0, i, 0)),
                  _layer_spec(mod.shape, li),
                  _layer_spec(norm_g.shape, li)]
                 + [_column_spec(w_in.shape, li, c0, width) for c0, width in proj_cols]
                 + [_layer_spec(c.shape, li) for c in consts],
        out_specs=[out_spec] * 3,
        out_shape=[out_shape] * 3,
        scratch_shapes=[pltpu.VMEM((m, d), BF16),
                        pltpu.VMEM((m, 2 * ATT_W + 2 * KV_W), F32),
                        pltpu.VMEM((nb, N_HEADS * tl, NK), F32),
                        pltpu.VMEM((nb, N_HEADS * tl, NK), BF16),
                        pltpu.VMEM((4, nb, NK, LANES), BF16),
                        pltpu.VMEM((4, nb, NK, LANES), BF16),
                        pltpu.VMEM((m, 2 * SSM_W), F32),
                        pltpu.VMEM((m, SSM_W), BF16),
                        pltpu.VMEM((m, 2 * N_STATE), F32),
                        pltpu.VMEM((nb, 2 * N_STATE), F32),
                        pltpu.VMEM((m, SSM_W), F32),
                        pltpu.VMEM((nb, POOL_HALO + tl, POOL_W), F32),
                        pltpu.VMEM((m, POOL_W), F32)],
        compiler_params=pltpu.CompilerParams(dimension_semantics=("arbitrary",), vmem_limit_bytes=VMEM_LIMIT),
        name="mixers",
    )(sinks, x, mod, norm_g, *([w_in] * len(proj_cols)), *consts)


def _merge_kernel(x_ref, mod_ref, ng_ref, ya_ref, ys_ref, yp_ref, wg_ref, wba_ref, wbs_ref, wbp_ref, wout_ref,
                  fg_ref, o_ref, *, final):
    x = x_ref[0]
    ms = jnp.mean(x * x, axis=-1, keepdims=True)
    gain = ng_ref[...] * (1.0 + mod_ref[1, 0])
    hb = (x * lax.rsqrt(ms + EPS) * gain + mod_ref[0, 0]).astype(BF16)
    merged = None
    for j, (y_ref, wb_ref) in enumerate(((ya_ref, wba_ref), (ys_ref, wbs_ref), (yp_ref, wbp_ref))):
        g = jnp.dot(hb, wg_ref[0, :, j * D_MODEL:(j + 1) * D_MODEL], preferred_element_type=F32)
        p = jnp.dot(y_ref[0], wb_ref[...], preferred_element_type=F32)
        term = _sigmoid(g) * p
        merged = term if merged is None else merged + term
    out = jnp.dot(merged.astype(BF16), wout_ref[...], preferred_element_type=F32)
    xn = x + mod_ref[2, 0] * out
    if final:
        ms = jnp.mean(xn * xn, axis=-1, keepdims=True)
        xn = xn * lax.rsqrt(ms + EPS) * fg_ref[...]
    o_ref[0] = xn


def _merge(x, li, mod, norm_g, ya, ys, yp, wts, final_g, final):
    nb, seq, d = x.shape
    tm = MERGE_ROWS
    row = lambda w: pl.BlockSpec((1, tm, w), lambda b, i: (b, i, 0))
    return pl.pallas_call(
        functools.partial(_merge_kernel, final=final),
        grid=(nb, seq // tm),
        in_specs=[row(d),
                  pl.BlockSpec((None, 3, 1, 1, d), lambda b, i: (li, 0, b, 0, 0)),
                  _layer_spec(norm_g.shape, li),
                  row(ATT_W), row(SSM_W), row(POOL_W),
                  _column_spec(wts["w_in"].shape, li, O_G, 3 * d), _layer_spec(wts["w_br_att"].shape, li),
                  _layer_spec(wts["w_br_ssm"].shape, li), _layer_spec(wts["w_br_pool"].shape, li),
                  _layer_spec(wts["w_out"].shape, li), _const_spec(final_g.shape)],
        out_specs=row(d),
        out_shape=jax.ShapeDtypeStruct(x.shape, x.dtype),
        compiler_params=pltpu.CompilerParams(dimension_semantics=("arbitrary", "arbitrary"),
                                             vmem_limit_bytes=VMEM_LIMIT),
        name="merge",
    )(x, mod, norm_g, ya, ys, yp, wts["w_in"], wts["w_br_att"], wts["w_br_ssm"], wts["w_br_pool"], wts["w_out"],
      final_g)


def _block_diag(blocks):
    *lead, n, r, c = blocks.shape
    eye = jnp.eye(n, dtype=blocks.dtype)
    return (blocks[..., :, :, None, :] * eye[:, None, :, None]).reshape(*lead, n * r, n * c)


def _stacked_weights(w_in, lam_re, lam_im, coef_re, coef_im, ssm_b_re, ssm_b_im, ssm_c_re, ssm_c_im, ssm_d,
                     w_glu, b_glu, w_pool, pool_scale, w_br_att, w_br_ssm, w_br_pool, w_out):
    depth = w_in.shape[0]
    wts = {"w_in": w_in.astype(BF16)}
    cr = coef_re[..., None]
    ci = coef_im[..., None]
    bb_re = cr * ssm_b_re - ci * ssm_b_im
    bb_im = cr * ssm_b_im + ci * ssm_b_re
    gpb = LANES // SSM_GROUP
    nblk = SSM_GROUPS // gpb
    bre = jnp.swapaxes(bb_re, 2, 3).reshape(depth, nblk, gpb, SSM_GROUP, SSM_STATE)
    bim = jnp.swapaxes(bb_im, 2, 3).reshape(depth, nblk, gpb, SSM_GROUP, SSM_STATE)
    wts["bmat"] = jnp.concatenate([_block_diag(bre), _block_diag(bim)], axis=-1).astype(BF16)
    cre = jnp.swapaxes(ssm_c_re, 2, 3).reshape(depth, nblk, gpb, SSM_STATE, SSM_GROUP)
    cim = jnp.swapaxes(-ssm_c_im, 2, 3).reshape(depth, nblk, gpb, SSM_STATE, SSM_GROUP)
    wts["cre"] = _block_diag(cre).astype(BF16)
    wts["cim"] = _block_diag(cim).astype(BF16)
    wts["lam"] = jnp.stack([lam_re.reshape(depth, N_STATE), lam_im.reshape(depth, N_STATE)], axis=1)
    wts["dskip"] = ssm_d.reshape(depth, 1, SSM_W)
    wts["w_glu"] = w_glu.astype(BF16)
    wts["b_glu"] = b_glu.reshape(depth, 1, SSM_W)
    wts["w_pool"] = w_pool.astype(BF16)
    wts["pscale"] = pool_scale.reshape(depth, 1, POOL_W)
    wts["w_br_att"] = w_br_att.astype(BF16)
    wts["w_br_ssm"] = w_br_ssm.astype(BF16)
    wts["w_br_pool"] = w_br_pool.astype(BF16)
    wts["w_out"] = w_out.astype(BF16)
    return wts


def kernel(x, c, norm_g, w_ada, b_ada, w_in, attn_sinks, ssm_a_re, ssm_a_im, ssm_log_dt, ssm_b_re, ssm_b_im,
           ssm_c_re, ssm_c_im, ssm_d, w_glu, b_glu, w_pool, pool_scale, w_br_att, w_br_ssm, w_br_pool, w_out,
           final_g):
    nb, seq, d = x.shape
    depth = w_in.shape[0]
    assert d == D_MODEL and nb == SUBLANES and seq % MERGE_ROWS == 0
    mod_all = _modulation(c, w_ada, b_ada)
    lam_re, lam_im, coef_re, coef_im = _ssm_prep(ssm_a_re, ssm_a_im, ssm_log_dt)
    fg = final_g.reshape(1, d)
    mod = jnp.transpose(mod_all.reshape(depth, nb, 3, d), (0, 2, 1, 3)).reshape(depth, 3, nb, 1, d)
    ng = norm_g.reshape(depth, 1, d)
    wts = _stacked_weights(w_in, lam_re, lam_im, coef_re, coef_im, ssm_b_re, ssm_b_im, ssm_c_re, ssm_c_im, ssm_d,
                           w_glu, b_glu, w_pool, pool_scale, w_br_att, w_br_ssm, w_br_pool, w_out)
    for li in range(depth):
        ya, ys, yp = _mixers(x, li, mod, ng, attn_sinks, wts)
        x = _merge(x, li, mod, ng, ya, ys, yp, wts, fg, final=(li == depth - 1))
    return x
```

```python
import functools
import math

import jax
import jax.numpy as jnp
from jax import lax
from jax.experimental import pallas as pl
from jax.experimental.pallas import tpu as pltpu

F32 = jnp.float32
BF16 = jnp.bfloat16

D_MODEL = 1024
CHUNK = 64
N_HEADS = 8
N_KV_HEADS = 2
HEAD_DIM = 64
WIN_CHUNKS = 2
NK = (WIN_CHUNKS + 1) * CHUNK
ATT_W = N_HEADS * HEAD_DIM
KV_W = N_KV_HEADS * HEAD_DIM
SSM_W = 512
SSM_GROUP = 16
SSM_GROUPS = 32
SSM_STATE = 64
N_STATE = SSM_GROUPS * SSM_STATE
POOL_W = 512
POOL_WINDOWS = (2, 4, 8, 16)
POOL_GW = 128
POOL_HALO = 16
O_K, O_US, O_UP, O_ZA, O_ZS, O_ZP, O_G = 512, 768, 1280, 1792, 2304, 2816, 3328
CHUNKS_PER_STEP = 2
MERGE_ROWS = 1024
MOD_COLS = 512
EPS = 1e-6
NEG_INF = -1e30
LOG2E = math.log2(math.e)
LANES = 128
SUBLANES = 8
VMEM_LIMIT = 58 * 1024 * 1024


def _sigmoid(x):
    return 0.5 * jnp.tanh(0.5 * x) + 0.5


def _silu(x):
    return x * _sigmoid(x)


def _gelu_tanh(x):
    return 0.5 * x * (1.0 + jnp.tanh(math.sqrt(2.0 / math.pi) * (x + 0.044715 * (x * x * x))))


def _const_spec(shape):
    nd = len(shape)
    return pl.BlockSpec(shape, lambda *_: (0,) * nd, pipeline_mode=pl.Buffered(1))


def _column_spec(shape, li, c0, width):
    return pl.BlockSpec((pl.Element(1), pl.Element(shape[1]), pl.Element(width)), lambda *_: (li, 0, c0),
                        pipeline_mode=pl.Buffered(1))


def _layer_spec(shape, li):
    nd = len(shape)
    return pl.BlockSpec((None,) + tuple(shape[1:]), lambda *_: (li,) + (0,) * (nd - 1),
                        pipeline_mode=pl.Buffered(1))


def _mod_kernel(c_ref, w_ref, b_ref, o_ref):
    ca = _silu(c_ref[...])
    o_ref[0] = jnp.dot(ca.astype(BF16), w_ref[0].astype(BF16), preferred_element_type=F32) + b_ref[0]


def _modulation(c, w_ada, b_ada):
    depth, d, n3 = w_ada.shape
    b = c.shape[0]
    bw = MOD_COLS
    return pl.pallas_call(
        _mod_kernel,
        grid=(depth, n3 // bw),
        in_specs=[pl.BlockSpec((b, d), lambda l, j: (0, 0)),
                  pl.BlockSpec((1, d, bw), lambda l, j: (l, 0, j)),
                  pl.BlockSpec((1, 1, bw), lambda l, j: (l, 0, j))],
        out_specs=pl.BlockSpec((1, b, bw), lambda l, j: (l, 0, j)),
        out_shape=jax.ShapeDtypeStruct((depth, b, n3), F32),
        name="adaln_mod",
    )(c, w_ada, b_ada.reshape(depth, 1, n3))


def _ssm_prep_kernel(are_ref, aim_ref, ldt_ref, lre_ref, lim_ref, cre_ref, cim_ref):
    a_re = are_ref[0]
    a_im = aim_ref[0]
    dt = jnp.exp(ldt_ref[0])
    mag = jnp.exp(a_re * dt)
    ang = a_im * dt
    l_re = mag * jnp.cos(ang)
    l_im = mag * jnp.sin(ang)
    n_re = l_re - 1.0
    den = a_re * a_re + a_im * a_im
    lre_ref[0] = l_re
    lim_ref[0] = l_im
    cre_ref[0] = (n_re * a_re + l_im * a_im) / den
    cim_ref[0] = (l_im * a_re - n_re * a_im) / den


def _ssm_prep(a_re, a_im, log_dt):
    depth, g, p = a_re.shape
    spec = pl.BlockSpec((1, g, p), lambda l: (l, 0, 0))
    out = jax.ShapeDtypeStruct((depth, g, p), F32)
    return pl.pallas_call(
        _ssm_prep_kernel,
        grid=(depth,),
        in_specs=[spec, spec, pl.BlockSpec((1, g, 1), lambda l: (l, 0, 0))],
        out_specs=[spec] * 4,
        out_shape=[out] * 4,
        name="ssm_discretise",
    )(a_re, a_im, log_dt.reshape(depth, g, 1))


def _mixer_kernel(sink_ref, x_ref, mod_ref, ng_ref, wq_ref, wkv_ref, wza_ref, wus_ref, wzs_ref, wup_ref, wzp_ref,
                  bmat_ref, cre_ref, cim_ref, lam_ref, dskip_ref, wglu_ref, bglu_ref, wpool_ref, pscale_ref,
                  ya_ref, ys_ref, yp_ref,
                  hb_ref, att_ref, s_ref, p_ref, kring_ref, vring_ref, us_ref, ut_ref, bu_ref, st_ref, yt_ref,
                  pext_ref, zp_ref,
                  *, nb, li):
    step = pl.program_id(0)
    tl = CHUNK
    m = nb * tl
    half = N_STATE // 4
    hpk = N_HEADS // N_KV_HEADS
    rb = 2 * tl
    blk_heads = [(kv * hpk + e, kv * hpk + 2 + e) for kv in range(N_KV_HEADS) for e in range(2)]

    def chunk_index(c):
        return step * CHUNKS_PER_STEP + c

    def chunk_rows(c):
        return slice(c * tl, (c + 1) * tl)

    @pl.when(step == 0)
    def _init():
        kring_ref[...] = jnp.zeros_like(kring_ref)
        vring_ref[...] = jnp.zeros_like(vring_ref)
        st_ref[...] = jnp.zeros_like(st_ref)
        pext_ref[:, 0:POOL_HALO, :] = jnp.zeros((nb, POOL_HALO, POOL_W), F32)

    def norm(c):
        x = x_ref[:, chunk_rows(c), :]
        ms = jnp.mean(x * x, axis=-1, keepdims=True)
        gain = ng_ref[...] * (1.0 + mod_ref[1])
        h = x * lax.rsqrt(ms + EPS) * gain + mod_ref[0]
        hb_ref[...] = h.reshape(m, D_MODEL).astype(BF16)

    def att_project():
        hb = hb_ref[...]
        att_ref[:, :ATT_W] = jnp.dot(hb, wq_ref[0], preferred_element_type=F32) * (LOG2E / math.sqrt(HEAD_DIM))
        att_ref[:, ATT_W:ATT_W + 2 * KV_W] = jnp.dot(hb, wkv_ref[0], preferred_element_type=F32)
        att_ref[:, ATT_W + 2 * KV_W:] = jnp.dot(hb, wza_ref[0], preferred_element_type=F32)
        lane = lax.broadcasted_iota(jnp.int32, (m, LANES), 1)
        lo = lane < HEAD_DIM
        zero = jnp.zeros((m, LANES), F32)
        for ring, c0 in ((kring_ref, ATT_W), (vring_ref, ATT_W + KV_W)):
            new = att_ref[:, c0:c0 + KV_W]
            swp = pltpu.roll(new, HEAD_DIM, axis=1)
            variants = (jnp.where(lo, new, zero),
                        jnp.where(lo, zero, swp),
                        jnp.where(lo, swp, zero),
                        jnp.where(lo, zero, new))
            for var in range(4):
                for b in range(nb):
                    ring[var, b, 0:NK - CHUNK, :] = ring[var, b, CHUNK:NK, :]
                ring[var, :, NK - CHUNK:NK, :] = variants[var].reshape(nb, tl, LANES).astype(BF16)

    def att_scores():
        for b in range(nb):
            for kv in range(N_KV_HEADS):
                qq = jnp.concatenate([att_ref[b * tl:(b + 1) * tl, (2 * kv + j) * LANES:(2 * kv + j + 1) * LANES]
                                      for j in range(2)], axis=0).astype(BF16)
                for e in range(2):
                    blk = 2 * kv + e
                    s_ref[b, blk * rb:(blk + 1) * rb, :] = lax.dot_general(
                        qq, kring_ref[blk, b], (((1,), (1,)), ((), ())), preferred_element_type=F32)

    def att_softmax(c, blk):
        ha, hb_ = blk_heads[blk]
        row = lax.broadcasted_iota(jnp.int32, (rb, NK), 0)
        kj = lax.broadcasted_iota(jnp.int32, (rb, NK), 1)
        qi = jnp.where(row < tl, row, row - tl)
        dist = jnp.abs(qi + WIN_CHUNKS * CHUNK - kj).astype(F32)
        valid = (chunk_index(c) * CHUNK + kj) >= WIN_CHUNKS * CHUNK
        first = lax.broadcasted_iota(jnp.int32, (rb, 1), 0) < tl
        slope = jnp.where(first, LOG2E * 2.0 ** (-(ha + 1)), LOG2E * 2.0 ** (-(hb_ + 1)))
        nbias = jnp.where(valid, -slope * dist, NEG_INF)
        sink = LOG2E * jnp.where(first, sink_ref[li, ha], sink_ref[li, hb_])
        for b in range(nb):
            s = s_ref[b, blk * rb:(blk + 1) * rb, :] + nbias
            mx = jnp.maximum(jnp.max(s, axis=-1, keepdims=True), sink)
            p = jnp.exp2(s - mx)
            den = jnp.sum(p, axis=-1, keepdims=True) + jnp.exp2(sink - mx)
            p_ref[b, blk * rb:(blk + 1) * rb, :] = (p * (1.0 / den)).astype(BF16)

    def att_values(c):
        for b in range(nb):
            for kv in range(N_KV_HEADS):
                o = None
                for e in range(2):
                    blk = 2 * kv + e
                    t = jnp.dot(p_ref[b, blk * rb:(blk + 1) * rb, :], vring_ref[blk, b],
                                preferred_element_type=F32)
                    o = t if o is None else o + t
                for j in range(2):
                    c0 = (2 * kv + j) * LANES
                    z = att_ref[b * tl:(b + 1) * tl, ATT_W + 2 * KV_W + c0:ATT_W + 2 * KV_W + c0 + LANES]
                    ya_ref[b, chunk_rows(c), c0:c0 + LANES] = (o[j * tl:(j + 1) * tl] * _silu(z)).astype(BF16)

    def ssm_project():
        hb = hb_ref[...]
        us_ref[:, :SSM_W] = jnp.dot(hb, wus_ref[0], preferred_element_type=F32)
        us_ref[:, SSM_W:] = jnp.dot(hb, wzs_ref[0], preferred_element_type=F32)
        u = us_ref[:, :SSM_W].reshape(nb, tl, SSM_W)
        ut_ref[...] = jnp.swapaxes(u, 0, 1).reshape(m, SSM_W).astype(BF16)

    def ssm_expand(r):
        res = jnp.dot(ut_ref[:, r * LANES:(r + 1) * LANES], bmat_ref[r], preferred_element_type=F32)
        bu_ref[:, r * half:(r + 1) * half] = res[:, :half]
        bu_ref[:, N_STATE + r * half:N_STATE + (r + 1) * half] = res[:, half:]

    def ssm_scan(r):
        c0 = r * half
        a_re = jnp.broadcast_to(lam_ref[0:1, c0:c0 + half], (nb, half))
        a_im = jnp.broadcast_to(lam_ref[1:2, c0:c0 + half], (nb, half))
        s_re = st_ref[:, c0:c0 + half]
        s_im = st_ref[:, N_STATE + c0:N_STATE + c0 + half]
        for t in range(tl):
            r0 = t * nb
            n_re = a_re * s_re - a_im * s_im + bu_ref[r0:r0 + nb, c0:c0 + half]
            n_im = a_re * s_im + a_im * s_re + bu_ref[r0:r0 + nb, N_STATE + c0:N_STATE + c0 + half]
            bu_ref[r0:r0 + nb, c0:c0 + half] = n_re
            bu_ref[r0:r0 + nb, N_STATE + c0:N_STATE + c0 + half] = n_im
            s_re, s_im = n_re, n_im
        st_ref[:, c0:c0 + half] = s_re
        st_ref[:, N_STATE + c0:N_STATE + c0 + half] = s_im

    def ssm_contract(r):
        x_re = bu_ref[:, r * half:(r + 1) * half].astype(BF16)
        x_im = bu_ref[:, N_STATE + r * half:N_STATE + (r + 1) * half].astype(BF16)
        yt_ref[:, r * LANES:(r + 1) * LANES] = (jnp.dot(x_re, cre_ref[r], preferred_element_type=F32)
                                               + jnp.dot(x_im, cim_ref[r], preferred_element_type=F32))

    def ssm_output(c):
        y = jnp.swapaxes(yt_ref[...].reshape(tl, nb, SSM_W), 0, 1).reshape(m, SSM_W)
        y = _gelu_tanh(y + dskip_ref[...] * us_ref[:, :SSM_W])
        gl = jnp.dot(y.astype(BF16), wglu_ref[...], preferred_element_type=F32) + bglu_ref[...]
        y = y * _sigmoid(gl)
        ys_ref[:, chunk_rows(c), :] = (y * _silu(us_ref[:, SSM_W:])).reshape(nb, tl, SSM_W).astype(BF16)

    def pool_project():
        hb = hb_ref[...]
        zp_ref[...] = jnp.dot(hb, wzp_ref[0], preferred_element_type=F32)
        pext_ref[:, POOL_HALO:POOL_HALO + tl, :] = jnp.dot(hb, wup_ref[0], preferred_element_type=F32
                                                           ).reshape(nb, tl, POOL_W)

    def pool_mix(c):
        tpos = chunk_index(c) * tl + lax.broadcasted_iota(jnp.int32, (1, tl, 1), 1)
        groups = []
        for gi, w in enumerate(POOL_WINDOWS):
            cs = slice(gi * POOL_GW, (gi + 1) * POOL_GW)
            ext = pext_ref[:, :, cs]
            tok = ext[:, POOL_HALO:, :]
            ssum = ext
            k = 1
            while k < w:
                ssum = ssum + pltpu.roll(ssum, k, axis=1)
                k *= 2
            cnt = jnp.minimum(tpos + 1, w).astype(F32)
            pooled = ssum[:, POOL_HALO:, :] / cnt - tok
            groups.append(jnp.dot(pooled.reshape(m, POOL_GW).astype(BF16), wpool_ref[gi],
                                  preferred_element_type=F32))
        y = jnp.concatenate(groups, axis=1) * pscale_ref[...]
        yp_ref[:, chunk_rows(c), :] = (y * _silu(zp_ref[...])).reshape(nb, tl, POOL_W).astype(BF16)
        pext_ref[:, 0:POOL_HALO, :] = pext_ref[:, tl:tl + POOL_HALO, :]

    P = functools.partial
    for c in range(CHUNKS_PER_STEP):
        schedule = (P(norm, c), att_project, pool_project, att_scores, ssm_project,
                    P(att_softmax, c, 0), P(ssm_expand, 0), P(att_softmax, c, 1), P(ssm_expand, 1),
                    P(ssm_scan, 0), P(ssm_expand, 2), P(att_softmax, c, 2), P(ssm_scan, 1), P(ssm_expand, 3),
                    P(att_softmax, c, 3), P(ssm_contract, 0), P(ssm_scan, 2), P(ssm_contract, 1), P(pool_mix, c),
                    P(ssm_scan, 3), P(att_values, c), P(ssm_contract, 2), P(ssm_contract, 3), P(ssm_output, c))
        for phase in schedule:
            phase()


def _mixers(x, li, mod, norm_g, sinks, wts):
    nb, seq, d = x.shape
    tl = CHUNK
    m = nb * tl
    consts = (wts["bmat"], wts["cre"],
              wts["cim"], wts["lam"], wts["dskip"], wts["w_glu"], wts["b_glu"], wts["w_pool"], wts["pscale"])
    w_in = wts["w_in"]
    proj_cols = ((0, ATT_W), (O_K, 2 * KV_W), (O_ZA, ATT_W), (O_US, SSM_W), (O_ZS, SSM_W), (O_UP, POOL_W),
                 (O_ZP, POOL_W))
    step_rows = CHUNKS_PER_STEP * tl
    out_spec = pl.BlockSpec((nb, step_rows, SSM_W), lambda i: (0, i, 0))
    out_shape = jax.ShapeDtypeStruct((nb, seq, SSM_W), BF16)
    return pl.pallas_call(
        functools.partial(_mixer_kernel, nb=nb, li=li),
        grid=(seq // step_rows,),
        in_specs=[pl.BlockSpec(memory_space=pltpu.SMEM),
                  pl.BlockSpec((nb, step_rows, d), lambda i: (0, i, 0)),
                  _layer_spec(mod.shape, li),
                  _layer_spec(norm_g.shape, li)]
                 + [_column_spec(w_in.shape, li, c0, width) for c0, width in proj_cols]
                 + [_layer_spec(c.shape, li) for c in consts],
        out_specs=[out_spec] * 3,
        out_shape=[out_shape] * 3,
        scratch_shapes=[pltpu.VMEM((m, d), BF16),
                        pltpu.VMEM((m, 2 * ATT_W + 2 * KV_W), F32),
                        pltpu.VMEM((nb, N_HEADS * tl, NK), F32),
                        pltpu.VMEM((nb, N_HEADS * tl, NK), BF16),
                        pltpu.VMEM((4, nb, NK, LANES), BF16),
                        pltpu.VMEM((4, nb, NK, LANES), BF16),
                        pltpu.VMEM((m, 2 * SSM_W), F32),
                        pltpu.VMEM((m, SSM_W), BF16),
                        pltpu.VMEM((m, 2 * N_STATE), F32),
                        pltpu.VMEM((nb, 2 * N_STATE), F32),
                        pltpu.VMEM((m, SSM_W), F32),
                        pltpu.VMEM((nb, POOL_HALO + tl, POOL_W), F32),
                        pltpu.VMEM((m, POOL_W), F32)],
        compiler_params=pltpu.CompilerParams(dimension_semantics=("arbitrary",), vmem_limit_bytes=VMEM_LIMIT),
        name="mixers",
    )(sinks, x, mod, norm_g, *([w_in] * len(proj_cols)), *consts)


def _merge_kernel(x_ref, mod_ref, ng_ref, ya_ref, ys_ref, yp_ref, wg_ref, wba_ref, wbs_ref, wbp_ref, wout_ref,
                  fg_ref, o_ref, *, final):
    x = x_ref[0]
    ms = jnp.mean(x * x, axis=-1, keepdims=True)
    gain = ng_ref[...] * (1.0 + mod_ref[1, 0])
    hb = (x * lax.rsqrt(ms + EPS) * gain + mod_ref[0, 0]).astype(BF16)
    merged = None
    for j, (y_ref, wb_ref) in enumerate(((ya_ref, wba_ref), (ys_ref, wbs_ref), (yp_ref, wbp_ref))):
        g = jnp.dot(hb, wg_ref[0, :, j * D_MODEL:(j + 1) * D_MODEL], preferred_element_type=F32)
        p = jnp.dot(y_ref[0], wb_ref[...], preferred_element_type=F32)
        term = _sigmoid(g) * p
        merged = term if merged is None else merged + term
    out = jnp.dot(merged.astype(BF16), wout_ref[...], preferred_element_type=F32)
    xn = x + mod_ref[2, 0] * out
    if final:
        ms = jnp.mean(xn * xn, axis=-1, keepdims=True)
        xn = xn * lax.rsqrt(ms + EPS) * fg_ref[...]
    o_ref[0] = xn


def _merge(x, li, mod, norm_g, ya, ys, yp, wts, final_g, final):
    nb, seq, d = x.shape
    tm = MERGE_ROWS
    row = lambda w: pl.BlockSpec((1, tm, w), lambda b, i: (b, i, 0))
    return pl.pallas_call(
        functools.partial(_merge_kernel, final=final),
        grid=(nb, seq // tm),
        in_specs=[row(d),
                  pl.BlockSpec((None, 3, 1, 1, d), lambda b, i: (li, 0, b, 0, 0)),
                  _layer_spec(norm_g.shape, li),
                  row(ATT_W), row(SSM_W), row(POOL_W),
                  _column_spec(wts["w_in"].shape, li, O_G, 3 * d), _layer_spec(wts["w_br_att"].shape, li),
                  _layer_spec(wts["w_br_ssm"].shape, li), _layer_spec(wts["w_br_pool"].shape, li),
                  _layer_spec(wts["w_out"].shape, li), _const_spec(final_g.shape)],
        out_specs=row(d),
        out_shape=jax.ShapeDtypeStruct(x.shape, x.dtype),
        compiler_params=pltpu.CompilerParams(dimension_semantics=("arbitrary", "arbitrary"),
                                             vmem_limit_bytes=VMEM_LIMIT),
        name="merge",
    )(x, mod, norm_g, ya, ys, yp, wts["w_in"], wts["w_br_att"], wts["w_br_ssm"], wts["w_br_pool"], wts["w_out"],
      final_g)


def _block_diag(blocks):
    *lead, n, r, c = blocks.shape
    eye = jnp.eye(n, dtype=blocks.dtype)
    return (blocks[..., :, :, None, :] * eye[:, None, :, None]).reshape(*lead, n * r, n * c)


def _stacked_weights(w_in, lam_re, lam_im, coef_re, coef_im, ssm_b_re, ssm_b_im, ssm_c_re, ssm_c_im, ssm_d,
                     w_glu, b_glu, w_pool, pool_scale, w_br_att, w_br_ssm, w_br_pool, w_out):
    depth = w_in.shape[0]
    wts = {"w_in": w_in.astype(BF16)}
    cr = coef_re[..., None]
    ci = coef_im[..., None]
    bb_re = cr * ssm_b_re - ci * ssm_b_im
    bb_im = cr * ssm_b_im + ci * ssm_b_re
    gpb = LANES // SSM_GROUP
    nblk = SSM_GROUPS // gpb
    bre = jnp.swapaxes(bb_re, 2, 3).reshape(depth, nblk, gpb, SSM_GROUP, SSM_STATE)
    bim = jnp.swapaxes(bb_im, 2, 3).reshape(depth, nblk, gpb, SSM_GROUP, SSM_STATE)
    wts["bmat"] = jnp.concatenate([_block_diag(bre), _block_diag(bim)], axis=-1).astype(BF16)
    cre = jnp.swapaxes(ssm_c_re, 2, 3).reshape(depth, nblk, gpb, SSM_STATE, SSM_GROUP)
    cim = jnp.swapaxes(-ssm_c_im, 2, 3).reshape(depth, nblk, gpb, SSM_STATE, SSM_GROUP)
    wts["cre"] = _block_diag(cre).astype(BF16)
    wts["cim"] = _block_diag(cim).astype(BF16)
    wts["lam"] = jnp.stack([lam_re.reshape(depth, N_STATE), lam_im.reshape(depth, N_STATE)], axis=1)
    wts["dskip"] = ssm_d.reshape(depth, 1, SSM_W)
    wts["w_glu"] = w_glu.astype(BF16)
    wts["b_glu"] = b_glu.reshape(depth, 1, SSM_W)
    wts["w_pool"] = w_pool.astype(BF16)
    wts["pscale"] = pool_scale.reshape(depth, 1, POOL_W)
    wts["w_br_att"] = w_br_att.astype(BF16)
    wts["w_br_ssm"] = w_br_ssm.astype(BF16)
    wts["w_br_pool"] = w_br_pool.astype(BF16)
    wts["w_out"] = w_out.astype(BF16)
    return wts


def kernel(x, c, norm_g, w_ada, b_ada, w_in, attn_sinks, ssm_a_re, ssm_a_im, ssm_log_dt, ssm_b_re, ssm_b_im,
           ssm_c_re, ssm_c_im, ssm_d, w_glu, b_glu, w_pool, pool_scale, w_br_att, w_br_ssm, w_br_pool, w_out,
           final_g):
    nb, seq, d = x.shape
    depth = w_in.shape[0]
    assert d == D_MODEL and nb == SUBLANES and seq % MERGE_ROWS == 0
    mod_all = _modulation(c, w_ada, b_ada)
    lam_re, lam_im, coef_re, coef_im = _ssm_prep(ssm_a_re, ssm_a_im, ssm_log_dt)
    fg = final_g.reshape(1, d)
    mod = jnp.transpose(mod_all.reshape(depth, nb, 3, d), (0, 2, 1, 3)).reshape(depth, 3, nb, 1, d)
    ng = norm_g.reshape(depth, 1, d)
    wts = _stacked_weights(w_in, lam_re, lam_im, coef_re, coef_im, ssm_b_re, ssm_b_im, ssm_c_re, ssm_c_im, ssm_d,
                           w_glu, b_glu, w_pool, pool_scale, w_br_att, w_br_ssm, w_br_pool, w_out)
    for li in range(depth):
        ya, ys, yp = _mixers(x, li, mod, ng, attn_sinks, wts)
        x = _merge(x, li, mod, ng, ya, ys, yp, wts, fg, final=(li == depth - 1))
    return x
```

```python
import functools
import math

import jax
import jax.numpy as jnp
from jax import lax
from jax.experimental import pallas as pl
from jax.experimental.pallas import tpu as pltpu

F32 = jnp.float32
BF16 = jnp.bfloat16

D_MODEL = 1024
CHUNK = 64
N_HEADS = 8
N_KV_HEADS = 2
HEAD_DIM = 64
WIN_CHUNKS = 2
NK = (WIN_CHUNKS + 1) * CHUNK
ATT_W = N_HEADS * HEAD_DIM
KV_W = N_KV_HEADS * HEAD_DIM
SSM_W = 512
SSM_GROUP = 16
SSM_GROUPS = 32
SSM_STATE = 64
N_STATE = SSM_GROUPS * SSM_STATE
POOL_W = 512
POOL_WINDOWS = (2, 4, 8, 16)
POOL_GW = 128
POOL_HALO = 16
O_K, O_US, O_UP, O_ZA, O_ZS, O_ZP, O_G = 512, 768, 1280, 1792, 2304, 2816, 3328
CHUNKS_PER_STEP = 1
MERGE_ROWS = 1024
MOD_COLS = 512
EPS = 1e-6
NEG_INF = -1e30
LOG2E = math.log2(math.e)
LANES = 128
SUBLANES = 8
VMEM_LIMIT = 58 * 1024 * 1024


def _sigmoid(x):
    return 0.5 * jnp.tanh(0.5 * x) + 0.5


def _silu(x):
    return x * _sigmoid(x)


def _gelu_tanh(x):
    return 0.5 * x * (1.0 + jnp.tanh(math.sqrt(2.0 / math.pi) * (x + 0.044715 * (x * x * x))))


def _const_spec(shape):
    nd = len(shape)
    return pl.BlockSpec(shape, lambda *_: (0,) * nd, pipeline_mode=pl.Buffered(1))


def _column_spec(shape, li, c0, width):
    return pl.BlockSpec((pl.Element(1), pl.Element(shape[1]), pl.Element(width)), lambda *_: (li, 0, c0),
                        pipeline_mode=pl.Buffered(1))


def _layer_spec(shape, li):
    nd = len(shape)
    return pl.BlockSpec((None,) + tuple(shape[1:]), lambda *_: (li,) + (0,) * (nd - 1),
                        pipeline_mode=pl.Buffered(1))


def _mod_kernel(c_ref, w_ref, b_ref, o_ref):
    ca = _silu(c_ref[...])
    o_ref[0] = jnp.dot(ca.astype(BF16), w_ref[0].astype(BF16), preferred_element_type=F32) + b_ref[0]


def _modulation(c, w_ada, b_ada):
    depth, d, n3 = w_ada.shape
    b = c.shape[0]
    bw = MOD_COLS
    return pl.pallas_call(
        _mod_kernel,
        grid=(depth, n3 // bw),
        in_specs=[pl.BlockSpec((b, d), lambda l, j: (0, 0)),
                  pl.BlockSpec((1, d, bw), lambda l, j: (l, 0, j)),
                  pl.BlockSpec((1, 1, bw), lambda l, j: (l, 0, j))],
        out_specs=pl.BlockSpec((1, b, bw), lambda l, j: (l, 0, j)),
        out_shape=jax.ShapeDtypeStruct((depth, b, n3), F32),
        name="adaln_mod",
    )(c, w_ada, b_ada.reshape(depth, 1, n3))


def _ssm_prep_kernel(are_ref, aim_ref, ldt_ref, lre_ref, lim_ref, cre_ref, cim_ref):
    a_re = are_ref[0]
    a_im = aim_ref[0]
    dt = jnp.exp(ldt_ref[0])
    mag = jnp.exp(a_re * dt)
    ang = a_im * dt
    l_re = mag * jnp.cos(ang)
    l_im = mag * jnp.sin(ang)
    n_re = l_re - 1.0
    den = a_re * a_re + a_im * a_im
    lre_ref[0] = l_re
    lim_ref[0] = l_im
    cre_ref[0] = (n_re * a_re + l_im * a_im) / den
    cim_ref[0] = (l_im * a_re - n_re * a_im) / den


def _ssm_prep(a_re, a_im, log_dt):
    depth, g, p = a_re.shape
    spec = pl.BlockSpec((1, g, p), lambda l: (l, 0, 0))
    out = jax.ShapeDtypeStruct((depth, g, p), F32)
    return pl.pallas_call(
        _ssm_prep_kernel,
        grid=(depth,),
        in_specs=[spec, spec, pl.BlockSpec((1, g, 1), lambda l: (l, 0, 0))],
        out_specs=[spec] * 4,
        out_shape=[out] * 4,
        name="ssm_discretise",
    )(a_re, a_im, log_dt.reshape(depth, g, 1))


def _mixer_kernel(sink_ref, x_ref, mod_ref, ng_ref, wq_ref, wkv_ref, wza_ref, wus_ref, wzs_ref, wup_ref, wzp_ref,
                  bmat_ref, cre_ref, cim_ref, lam_ref, dskip_ref, wglu_ref, bglu_ref, wpool_ref, pscale_ref,
                  ya_ref, ys_ref, yp_ref,
                  hb_ref, att_ref, s_ref, p_ref, kring_ref, vring_ref, us_ref, ut_ref, bu_ref, st_ref, yt_ref,
                  pext_ref, zp_ref,
                  *, nb, li):
    step = pl.program_id(0)
    tl = CHUNK
    m = nb * tl
    half = N_STATE // 4
    im0 = N_STATE + LANES
    hpk = N_HEADS // N_KV_HEADS
    rb = 2 * tl
    blk_heads = [(kv * hpk + e, kv * hpk + 2 + e) for kv in range(N_KV_HEADS) for e in range(2)]

    def chunk_index(c):
        return step * CHUNKS_PER_STEP + c

    def chunk_rows(c):
        return slice(c * tl, (c + 1) * tl)

    @pl.when(step == 0)
    def _init():
        kring_ref[...] = jnp.zeros_like(kring_ref)
        vring_ref[...] = jnp.zeros_like(vring_ref)
        st_ref[...] = jnp.zeros_like(st_ref)
        pext_ref[:, 0:POOL_HALO, :] = jnp.zeros((nb, POOL_HALO, POOL_W), F32)

    def norm(c):
        x = x_ref[:, chunk_rows(c), :]
        ms = jnp.mean(x * x, axis=-1, keepdims=True)
        gain = ng_ref[...] * (1.0 + mod_ref[1])
        h = x * lax.rsqrt(ms + EPS) * gain + mod_ref[0]
        hb_ref[...] = h.reshape(m, D_MODEL).astype(BF16)

    def att_project():
        hb = hb_ref[...]
        att_ref[:, :ATT_W] = jnp.dot(hb, wq_ref[0], preferred_element_type=F32) * (LOG2E / math.sqrt(HEAD_DIM))
        att_ref[:, ATT_W:ATT_W + 2 * KV_W] = jnp.dot(hb, wkv_ref[0], preferred_element_type=F32)
        att_ref[:, ATT_W + 2 * KV_W:] = jnp.dot(hb, wza_ref[0], preferred_element_type=F32)
        lane = lax.broadcasted_iota(jnp.int32, (m, LANES), 1)
        lo = lane < HEAD_DIM
        zero = jnp.zeros((m, LANES), F32)
        for ring, c0 in ((kring_ref, ATT_W), (vring_ref, ATT_W + KV_W)):
            new = att_ref[:, c0:c0 + KV_W]
            swp = pltpu.roll(new, HEAD_DIM, axis=1)
            variants = (jnp.where(lo, new, zero),
                        jnp.where(lo, zero, swp),
                        jnp.where(lo, swp, zero),
                        jnp.where(lo, zero, new))
            for var in range(4):
                for b in range(nb):
                    ring[var, b, 0:NK - CHUNK, :] = ring[var, b, CHUNK:NK, :]
                ring[var, :, NK - CHUNK:NK, :] = variants[var].reshape(nb, tl, LANES).astype(BF16)

    def att_scores():
        for b in range(nb):
            for kv in range(N_KV_HEADS):
                qq = jnp.concatenate([att_ref[b * tl:(b + 1) * tl, (2 * kv + j) * LANES:(2 * kv + j + 1) * LANES]
                                      for j in range(2)], axis=0).astype(BF16)
                for e in range(2):
                    blk = 2 * kv + e
                    s_ref[b, blk * rb:(blk + 1) * rb, :] = lax.dot_general(
                        qq, kring_ref[blk, b], (((1,), (1,)), ((), ())), preferred_element_type=F32)

    def att_softmax(c, blk):
        ha, hb_ = blk_heads[blk]
        row = lax.broadcasted_iota(jnp.int32, (rb, NK), 0)
        kj = lax.broadcasted_iota(jnp.int32, (rb, NK), 1)
        qi = jnp.where(row < tl, row, row - tl)
        dist = jnp.abs(qi + WIN_CHUNKS * CHUNK - kj).astype(F32)
        valid = (chunk_index(c) * CHUNK + kj) >= WIN_CHUNKS * CHUNK
        first = lax.broadcasted_iota(jnp.int32, (rb, 1), 0) < tl
        slope = jnp.where(first, LOG2E * 2.0 ** (-(ha + 1)), LOG2E * 2.0 ** (-(hb_ + 1)))
        nbias = jnp.where(valid, -slope * dist, NEG_INF)
        sink = LOG2E * jnp.where(first, sink_ref[li, ha], sink_ref[li, hb_])
        for b in range(nb):
            s = s_ref[b, blk * rb:(blk + 1) * rb, :] + nbias
            mx = jnp.maximum(jnp.max(s, axis=-1, keepdims=True), sink)
            p = jnp.exp2(s - mx)
            den = jnp.sum(p, axis=-1, keepdims=True) + jnp.exp2(sink - mx)
            p_ref[b, blk * rb:(blk + 1) * rb, :] = (p * (1.0 / den)).astype(BF16)

    def att_values(c):
        for b in range(nb):
            for kv in range(N_KV_HEADS):
                o = None
                for e in range(2):
                    blk = 2 * kv + e
                    t = jnp.dot(p_ref[b, blk * rb:(blk + 1) * rb, :], vring_ref[blk, b],
                                preferred_element_type=F32)
                    o = t if o is None else o + t
                for j in range(2):
                    c0 = (2 * kv + j) * LANES
                    z = att_ref[b * tl:(b + 1) * tl, ATT_W + 2 * KV_W + c0:ATT_W + 2 * KV_W + c0 + LANES]
                    ya_ref[b, chunk_rows(c), c0:c0 + LANES] = (o[j * tl:(j + 1) * tl] * _silu(z)).astype(BF16)

    def ssm_project():
        hb = hb_ref[...]
        us_ref[:, :SSM_W] = jnp.dot(hb, wus_ref[0], preferred_element_type=F32)
        us_ref[:, SSM_W:] = jnp.dot(hb, wzs_ref[0], preferred_element_type=F32)
        u = us_ref[:, :SSM_W].reshape(nb, tl, SSM_W)
        ut_ref[...] = jnp.swapaxes(u, 0, 1).reshape(m, SSM_W).astype(BF16)

    def ssm_expand(r):
        res = jnp.dot(ut_ref[:, r * LANES:(r + 1) * LANES], bmat_ref[r], preferred_element_type=F32)
        bu_ref[:, r * half:(r + 1) * half] = res[:, :half]
        bu_ref[:, im0 + r * half:im0 + (r + 1) * half] = res[:, half:]

    def ssm_scan(r):
        c0 = r * half
        a_re = jnp.broadcast_to(lam_ref[0:1, c0:c0 + half], (nb, half))
        a_im = jnp.broadcast_to(lam_ref[1:2, c0:c0 + half], (nb, half))
        s_re = st_ref[:, c0:c0 + half]
        s_im = st_ref[:, N_STATE + c0:N_STATE + c0 + half]
        for t in range(tl):
            r0 = t * nb
            n_re = a_re * s_re - a_im * s_im + bu_ref[r0:r0 + nb, c0:c0 + half]
            n_im = a_re * s_im + a_im * s_re + bu_ref[r0:r0 + nb, im0 + c0:im0 + c0 + half]
            bu_ref[r0:r0 + nb, c0:c0 + half] = n_re
            bu_ref[r0:r0 + nb, im0 + c0:im0 + c0 + half] = n_im
            s_re, s_im = n_re, n_im
        st_ref[:, c0:c0 + half] = s_re
        st_ref[:, N_STATE + c0:N_STATE + c0 + half] = s_im

    def ssm_contract(r):
        x_re = bu_ref[:, r * half:(r + 1) * half].astype(BF16)
        x_im = bu_ref[:, im0 + r * half:im0 + (r + 1) * half].astype(BF16)
        yt_ref[:, r * LANES:(r + 1) * LANES] = (jnp.dot(x_re, cre_ref[r], preferred_element_type=F32)
                                               + jnp.dot(x_im, cim_ref[r], preferred_element_type=F32))

    def ssm_output(c):
        y = jnp.swapaxes(yt_ref[...].reshape(tl, nb, SSM_W), 0, 1).reshape(m, SSM_W)
        y = _gelu_tanh(y + dskip_ref[...] * us_ref[:, :SSM_W])
        gl = jnp.dot(y.astype(BF16), wglu_ref[...], preferred_element_type=F32) + bglu_ref[...]
        y = y * _sigmoid(gl)
        ys_ref[:, chunk_rows(c), :] = (y * _silu(us_ref[:, SSM_W:])).reshape(nb, tl, SSM_W).astype(BF16)

    def pool_project():
        hb = hb_ref[...]
        zp_ref[...] = jnp.dot(hb, wzp_ref[0], preferred_element_type=F32)
        pext_ref[:, POOL_HALO:POOL_HALO + tl, :] = jnp.dot(hb, wup_ref[0], preferred_element_type=F32
                                                           ).reshape(nb, tl, POOL_W)

    def pool_mix(c):
        tpos = chunk_index(c) * tl + lax.broadcasted_iota(jnp.int32, (1, tl, 1), 1)
        groups = []
        for gi, w in enumerate(POOL_WINDOWS):
            cs = slice(gi * POOL_GW, (gi + 1) * POOL_GW)
            ext = pext_ref[:, :, cs]
            tok = ext[:, POOL_HALO:, :]
            ssum = ext
            k = 1
            while k < w:
                ssum = ssum + pltpu.roll(ssum, k, axis=1)
                k *= 2
            cnt = jnp.minimum(tpos + 1, w).astype(F32)
            pooled = ssum[:, POOL_HALO:, :] / cnt - tok
            groups.append(jnp.dot(pooled.reshape(m, POOL_GW).astype(BF16), wpool_ref[gi],
                                  preferred_element_type=F32))
        y = jnp.concatenate(groups, axis=1) * pscale_ref[...]
        yp_ref[:, chunk_rows(c), :] = (y * _silu(zp_ref[...])).reshape(nb, tl, POOL_W).astype(BF16)
        pext_ref[:, 0:POOL_HALO, :] = pext_ref[:, tl:tl + POOL_HALO, :]

    P = functools.partial
    for c in range(CHUNKS_PER_STEP):
        schedule = (P(norm, c), att_project, pool_project, att_scores, ssm_project,
                    P(att_softmax, c, 0), P(ssm_expand, 0), P(att_softmax, c, 1), P(ssm_expand, 1),
                    P(ssm_scan, 0), P(ssm_expand, 2), P(att_softmax, c, 2), P(ssm_scan, 1), P(ssm_expand, 3),
                    P(att_softmax, c, 3), P(ssm_contract, 0), P(ssm_scan, 2), P(ssm_contract, 1), P(pool_mix, c),
                    P(ssm_scan, 3), P(att_values, c), P(ssm_contract, 2), P(ssm_contract, 3), P(ssm_output, c))
        for phase in schedule:
            phase()


def _mixers(x, li, mod, norm_g, sinks, wts):
    nb, seq, d = x.shape
    tl = CHUNK
    m = nb * tl
    consts = (wts["bmat"], wts["cre"],
              wts["cim"], wts["lam"], wts["dskip"], wts["w_glu"], wts["b_glu"], wts["w_pool"], wts["pscale"])
    w_in = wts["w_in"]
    proj_cols = ((0, ATT_W), (O_K, 2 * KV_W), (O_ZA, ATT_W), (O_US, SSM_W), (O_ZS, SSM_W), (O_UP, POOL_W),
                 (O_ZP, POOL_W))
    step_rows = CHUNKS_PER_STEP * tl
    out_spec = pl.BlockSpec((nb, step_rows, SSM_W), lambda i: (0, i, 0))
    out_shape = jax.ShapeDtypeStruct((nb, seq, SSM_W), BF16)
    return pl.pallas_call(
        functools.partial(_mixer_kernel, nb=nb, li=li),
        grid=(seq // step_rows,),
        in_specs=[pl.BlockSpec(memory_space=pltpu.SMEM),
                  pl.BlockSpec((nb, step_rows, d), lambda i: (0, i, 0)),
                  _layer_spec(mod.shape, li),
                  _layer_spec(norm_g.shape, li)]
                 + [_column_spec(w_in.shape, li, c0, width) for c0, width in proj_cols]
                 + [_layer_spec(c.shape, li) for c in consts],
        out_specs=[out_spec] * 3,
        out_shape=[out_shape] * 3,
        scratch_shapes=[pltpu.VMEM((m, d), BF16),
                        pltpu.VMEM((m, 2 * ATT_W + 2 * KV_W), F32),
                        pltpu.VMEM((nb, N_HEADS * tl, NK), F32),
                        pltpu.VMEM((nb, N_HEADS * tl, NK), BF16),
                        pltpu.VMEM((4, nb, NK, LANES), BF16),
                        pltpu.VMEM((4, nb, NK, LANES), BF16),
                        pltpu.VMEM((m, 2 * SSM_W), F32),
                        pltpu.VMEM((m, SSM_W), BF16),
                        pltpu.VMEM((m, 2 * N_STATE + LANES), F32),
                        pltpu.VMEM((nb, 2 * N_STATE), F32),
                        pltpu.VMEM((m, SSM_W), F32),
                        pltpu.VMEM((nb, POOL_HALO + tl, POOL_W), F32),
                        pltpu.VMEM((m, POOL_W), F32)],
        compiler_params=pltpu.CompilerParams(dimension_semantics=("arbitrary",), vmem_limit_bytes=VMEM_LIMIT),
        name="mixers",
    )(sinks, x, mod, norm_g, *([w_in] * len(proj_cols)), *consts)


def _merge_kernel(x_ref, mod_ref, ng_ref, ya_ref, ys_ref, yp_ref, wg_ref, wba_ref, wbs_ref, wbp_ref, wout_ref,
                  fg_ref, o_ref, *, final):
    x = x_ref[0]
    ms = jnp.mean(x * x, axis=-1, keepdims=True)
    gain = ng_ref[...] * (1.0 + mod_ref[1, 0])
    hb = (x * lax.rsqrt(ms + EPS) * gain + mod_ref[0, 0]).astype(BF16)
    merged = None
    for j, (y_ref, wb_ref) in enumerate(((ya_ref, wba_ref), (ys_ref, wbs_ref), (yp_ref, wbp_ref))):
        g = jnp.dot(hb, wg_ref[0, :, j * D_MODEL:(j + 1) * D_MODEL], preferred_element_type=F32)
        p = jnp.dot(y_ref[0], wb_ref[...], preferred_element_type=F32)
        term = _sigmoid(g) * p
        merged = term if merged is None else merged + term
    out = jnp.dot(merged.astype(BF16), wout_ref[...], preferred_element_type=F32)
    xn = x + mod_ref[2, 0] * out
    if final:
        ms = jnp.mean(xn * xn, axis=-1, keepdims=True)
        xn = xn * lax.rsqrt(ms + EPS) * fg_ref[...]
    o_ref[0] = xn


def _merge(x, li, mod, norm_g, ya, ys, yp, wts, final_g, final):
    nb, seq, d = x.shape
    tm = MERGE_ROWS
    row = lambda w: pl.BlockSpec((1, tm, w), lambda b, i: (b, i, 0))
    return pl.pallas_call(
        functools.partial(_merge_kernel, final=final),
        grid=(nb, seq // tm),
        in_specs=[row(d),
                  pl.BlockSpec((None, 3, 1, 1, d), lambda b, i: (li, 0, b, 0, 0)),
                  _layer_spec(norm_g.shape, li),
                  row(ATT_W), row(SSM_W), row(POOL_W),
                  _column_spec(wts["w_in"].shape, li, O_G, 3 * d), _layer_spec(wts["w_br_att"].shape, li),
                  _layer_spec(wts["w_br_ssm"].shape, li), _layer_spec(wts["w_br_pool"].shape, li),
                  _layer_spec(wts["w_out"].shape, li), _const_spec(final_g.shape)],
        out_specs=row(d),
        out_shape=jax.ShapeDtypeStruct(x.shape, x.dtype),
        compiler_params=pltpu.CompilerParams(dimension_semantics=("arbitrary", "arbitrary"),
                                             vmem_limit_bytes=VMEM_LIMIT),
        name="merge",
    )(x, mod, norm_g, ya, ys, yp, wts["w_in"], wts["w_br_att"], wts["w_br_ssm"], wts["w_br_pool"], wts["w_out"],
      final_g)


def _block_diag(blocks):
    *lead, n, r, c = blocks.shape
    eye = jnp.eye(n, dtype=blocks.dtype)
    return (blocks[..., :, :, None, :] * eye[:, None, :, None]).reshape(*lead, n * r, n * c)


def _stacked_weights(w_in, lam_re, lam_im, coef_re, coef_im, ssm_b_re, ssm_b_im, ssm_c_re, ssm_c_im, ssm_d,
                     w_glu, b_glu, w_pool, pool_scale, w_br_att, w_br_ssm, w_br_pool, w_out):
    depth = w_in.shape[0]
    wts = {"w_in": w_in.astype(BF16)}
    cr = coef_re[..., None]
    ci = coef_im[..., None]
    bb_re = cr * ssm_b_re - ci * ssm_b_im
    bb_im = cr * ssm_b_im + ci * ssm_b_re
    gpb = LANES // SSM_GROUP
    nblk = SSM_GROUPS // gpb
    bre = jnp.swapaxes(bb_re, 2, 3).reshape(depth, nblk, gpb, SSM_GROUP, SSM_STATE)
    bim = jnp.swapaxes(bb_im, 2, 3).reshape(depth, nblk, gpb, SSM_GROUP, SSM_STATE)
    wts["bmat"] = jnp.concatenate([_block_diag(bre), _block_diag(bim)], axis=-1).astype(BF16)
    cre = jnp.swapaxes(ssm_c_re, 2, 3).reshape(depth, nblk, gpb, SSM_STATE, SSM_GROUP)
    cim = jnp.swapaxes(-ssm_c_im, 2, 3).reshape(depth, nblk, gpb, SSM_STATE, SSM_GROUP)
    wts["cre"] = _block_diag(cre).astype(BF16)
    wts["cim"] = _block_diag(cim).astype(BF16)
    wts["lam"] = jnp.stack([lam_re.reshape(depth, N_STATE), lam_im.reshape(depth, N_STATE)], axis=1)
    wts["dskip"] = ssm_d.reshape(depth, 1, SSM_W)
    wts["w_glu"] = w_glu.astype(BF16)
    wts["b_glu"] = b_glu.reshape(depth, 1, SSM_W)
    wts["w_pool"] = w_pool.astype(BF16)
    wts["pscale"] = pool_scale.reshape(depth, 1, POOL_W)
    wts["w_br_att"] = w_br_att.astype(BF16)
    wts["w_br_ssm"] = w_br_ssm.astype(BF16)
    wts["w_br_pool"] = w_br_pool.astype(BF16)
    wts["w_out"] = w_out.astype(BF16)
    return wts


def kernel(x, c, norm_g, w_ada, b_ada, w_in, attn_sinks, ssm_a_re, ssm_a_im, ssm_log_dt, ssm_b_re, ssm_b_im,
           ssm_c_re, ssm_c_im, ssm_d, w_glu, b_glu, w_pool, pool_scale, w_br_att, w_br_ssm, w_br_pool, w_out,
           final_g):
    nb, seq, d = x.shape
    depth = w_in.shape[0]
    assert d == D_MODEL and nb == SUBLANES and seq % MERGE_ROWS == 0
    mod_all = _modulation(c, w_ada, b_ada)
    lam_re, lam_im, coef_re, coef_im = _ssm_prep(ssm_a_re, ssm_a_im, ssm_log_dt)
    fg = final_g.reshape(1, d)
    mod = jnp.transpose(mod_all.reshape(depth, nb, 3, d), (0, 2, 1, 3)).reshape(depth, 3, nb, 1, d)
    ng = norm_g.reshape(depth, 1, d)
    wts = _stacked_weights(w_in, lam_re, lam_im, coef_re, coef_im, ssm_b_re, ssm_b_im, ssm_c_re, ssm_c_im, ssm_d,
                           w_glu, b_glu, w_pool, pool_scale, w_br_att, w_br_ssm, w_br_pool, w_out)
    for li in range(depth):
        ya, ys, yp = _mixers(x, li, mod, ng, attn_sinks, wts)
        x = _merge(x, li, mod, ng, ya, ys, yp, wts, fg, final=(li == depth - 1))
    return x
```

```python
import functools
import math

import numpy as np
import jax
import jax.numpy as jnp
from jax import lax
from jax.experimental import pallas as pl
from jax.experimental.pallas import tpu as pltpu

F32 = jnp.float32
BF16 = jnp.bfloat16

D_MODEL = 1024
CHUNK = 64
N_HEADS = 8
N_KV_HEADS = 2
HEAD_DIM = 64
WIN_CHUNKS = 2
NK = (WIN_CHUNKS + 1) * CHUNK
ATT_W = N_HEADS * HEAD_DIM
KV_W = N_KV_HEADS * HEAD_DIM
SSM_W = 512
SSM_GROUP = 16
SSM_GROUPS = 32
SSM_STATE = 64
N_STATE = SSM_GROUPS * SSM_STATE
POOL_W = 512
POOL_WINDOWS = (2, 4, 8, 16)
POOL_GW = 128
POOL_HALO = 16
O_K, O_US, O_UP, O_ZA, O_ZS, O_ZP, O_G = 512, 768, 1280, 1792, 2304, 2816, 3328
CHUNKS_PER_STEP = 1
MERGE_ROWS = 1024
MOD_COLS = 1024
EPS = 1e-6
NEG_INF = -1e30
LOG2E = math.log2(math.e)
LANES = 128
SUBLANES = 8
VMEM_LIMIT = 58 * 1024 * 1024


def _sigmoid(x):
    return 0.5 * jnp.tanh(0.5 * x) + 0.5


def _silu(x):
    return x * _sigmoid(x)


def _gelu_tanh(x):
    return 0.5 * x * (1.0 + jnp.tanh(math.sqrt(2.0 / math.pi) * (x + 0.044715 * (x * x * x))))


def _const_spec(shape):
    nd = len(shape)
    return pl.BlockSpec(shape, lambda *_: (0,) * nd, pipeline_mode=pl.Buffered(1))


def _column_spec(shape, li, c0, width):
    return pl.BlockSpec((pl.Element(1), pl.Element(shape[1]), pl.Element(width)), lambda *_: (li, 0, c0),
                        pipeline_mode=pl.Buffered(1))


def _layer_spec(shape, li):
    nd = len(shape)
    return pl.BlockSpec((None,) + tuple(shape[1:]), lambda *_: (li,) + (0,) * (nd - 1),
                        pipeline_mode=pl.Buffered(1))


def _mod_kernel(c_ref, w_ref, b_ref, o_ref):
    ca = _silu(c_ref[...])
    o_ref[0] = jnp.dot(ca.astype(BF16), w_ref[0].astype(BF16), preferred_element_type=F32) + b_ref[0]


def _modulation(c, w_ada, b_ada):
    depth, d, n3 = w_ada.shape
    b = c.shape[0]
    bw = MOD_COLS
    return pl.pallas_call(
        _mod_kernel,
        grid=(depth, n3 // bw),
        in_specs=[pl.BlockSpec((b, d), lambda l, j: (0, 0)),
                  pl.BlockSpec((1, d, bw), lambda l, j: (l, 0, j)),
                  pl.BlockSpec((1, 1, bw), lambda l, j: (l, 0, j))],
        out_specs=pl.BlockSpec((1, b, bw), lambda l, j: (l, 0, j)),
        out_shape=jax.ShapeDtypeStruct((depth, b, n3), F32),
        name="adaln_mod",
    )(c, w_ada, b_ada.reshape(depth, 1, n3))


def _ssm_prep_kernel(are_ref, aim_ref, ldt_ref, lre_ref, lim_ref, cre_ref, cim_ref):
    a_re = are_ref[0]
    a_im = aim_ref[0]
    dt = jnp.exp(ldt_ref[0])
    mag = jnp.exp(a_re * dt)
    ang = a_im * dt
    l_re = mag * jnp.cos(ang)
    l_im = mag * jnp.sin(ang)
    n_re = l_re - 1.0
    den = a_re * a_re + a_im * a_im
    lre_ref[0] = l_re
    lim_ref[0] = l_im
    cre_ref[0] = (n_re * a_re + l_im * a_im) / den
    cim_ref[0] = (l_im * a_re - n_re * a_im) / den


def _ssm_prep(a_re, a_im, log_dt):
    depth, g, p = a_re.shape
    spec = pl.BlockSpec((1, g, p), lambda l: (l, 0, 0))
    out = jax.ShapeDtypeStruct((depth, g, p), F32)
    return pl.pallas_call(
        _ssm_prep_kernel,
        grid=(depth,),
        in_specs=[spec, spec, pl.BlockSpec((1, g, 1), lambda l: (l, 0, 0))],
        out_specs=[spec] * 4,
        out_shape=[out] * 4,
        name="ssm_discretise",
    )(a_re, a_im, log_dt.reshape(depth, g, 1))


def _mixer_kernel(sink_ref, x_ref, mod_ref, ng_ref, wq_ref, wkv_ref, wza_ref, wus_ref, wzs_ref, wup_ref, wzp_ref,
                  bmat_ref, cre_ref, cim_ref, lam_ref, dskip_ref, wglu_ref, bglu_ref, wpool_ref, pscale_ref,
                  ya_ref, ys_ref, yp_ref,
                  hb_ref, att_ref, s_ref, p_ref, kring_ref, vring_ref, us_ref, ut_ref, bu_ref, st_ref, yt_ref,
                  pext_ref, zp_ref,
                  *, nb, li):
    step = pl.program_id(0)
    tl = CHUNK
    m = nb * tl
    half = N_STATE // 4
    im0 = N_STATE + LANES
    hpk = N_HEADS // N_KV_HEADS
    rb = 2 * tl
    blk_heads = [(kv * hpk + e, kv * hpk + 2 + e) for kv in range(N_KV_HEADS) for e in range(2)]

    def chunk_index(c):
        return step * CHUNKS_PER_STEP + c

    def chunk_rows(c):
        return slice(c * tl, (c + 1) * tl)

    @pl.when(step == 0)
    def _init():
        kring_ref[...] = jnp.zeros_like(kring_ref)
        vring_ref[...] = jnp.zeros_like(vring_ref)
        st_ref[...] = jnp.zeros_like(st_ref)
        pext_ref[:, 0:POOL_HALO, :] = jnp.zeros((nb, POOL_HALO, POOL_W), F32)

    def norm(c):
        x = x_ref[:, chunk_rows(c), :]
        ms = jnp.mean(x * x, axis=-1, keepdims=True)
        gain = ng_ref[...] * (1.0 + mod_ref[1])
        h = x * lax.rsqrt(ms + EPS) * gain + mod_ref[0]
        hb_ref[...] = h.reshape(m, D_MODEL).astype(BF16)

    def att_project():
        hb = hb_ref[...]
        att_ref[:, :ATT_W] = jnp.dot(hb, wq_ref[0], preferred_element_type=F32) * (LOG2E / math.sqrt(HEAD_DIM))
        att_ref[:, ATT_W:ATT_W + 2 * KV_W] = jnp.dot(hb, wkv_ref[0], preferred_element_type=F32)
        att_ref[:, ATT_W + 2 * KV_W:] = jnp.dot(hb, wza_ref[0], preferred_element_type=F32)
        lane = lax.broadcasted_iota(jnp.int32, (m, LANES), 1)
        lo = lane < HEAD_DIM
        zero = jnp.zeros((m, LANES), F32)
        for ring, c0 in ((kring_ref, ATT_W), (vring_ref, ATT_W + KV_W)):
            new = att_ref[:, c0:c0 + KV_W]
            swp = pltpu.roll(new, HEAD_DIM, axis=1)
            variants = (jnp.where(lo, new, zero),
                        jnp.where(lo, zero, swp),
                        jnp.where(lo, swp, zero),
                        jnp.where(lo, zero, new))
            for var in range(4):
                for b in range(nb):
                    ring[var, b, 0:NK - CHUNK, :] = ring[var, b, CHUNK:NK, :]
                ring[var, :, NK - CHUNK:NK, :] = variants[var].reshape(nb, tl, LANES).astype(BF16)

    def att_scores():
        for b in range(nb):
            for kv in range(N_KV_HEADS):
                qq = jnp.concatenate([att_ref[b * tl:(b + 1) * tl, (2 * kv + j) * LANES:(2 * kv + j + 1) * LANES]
                                      for j in range(2)], axis=0).astype(BF16)
                for e in range(2):
                    blk = 2 * kv + e
                    s_ref[b, blk * rb:(blk + 1) * rb, :] = lax.dot_general(
                        qq, kring_ref[blk, b], (((1,), (1,)), ((), ())), preferred_element_type=F32)

    def att_softmax(c, blk):
        ha, hb_ = blk_heads[blk]
        row = lax.broadcasted_iota(jnp.int32, (rb, NK), 0)
        kj = lax.broadcasted_iota(jnp.int32, (rb, NK), 1)
        qi = jnp.where(row < tl, row, row - tl)
        dist = jnp.abs(qi + WIN_CHUNKS * CHUNK - kj).astype(F32)
        valid = (chunk_index(c) * CHUNK + kj) >= WIN_CHUNKS * CHUNK
        first = lax.broadcasted_iota(jnp.int32, (rb, 1), 0) < tl
        slope = jnp.where(first, LOG2E * 2.0 ** (-(ha + 1)), LOG2E * 2.0 ** (-(hb_ + 1)))
        nbias = jnp.where(valid, -slope * dist, NEG_INF)
        sink = LOG2E * jnp.where(first, sink_ref[li, ha], sink_ref[li, hb_])
        for b in range(nb):
            s = s_ref[b, blk * rb:(blk + 1) * rb, :] + nbias
            mx = jnp.maximum(jnp.max(s, axis=-1, keepdims=True), sink)
            p = jnp.exp2(s - mx)
            den = jnp.sum(p, axis=-1, keepdims=True) + jnp.exp2(sink - mx)
            p_ref[b, blk * rb:(blk + 1) * rb, :] = (p * (1.0 / den)).astype(BF16)

    def att_values(c):
        for b in range(nb):
            for kv in range(N_KV_HEADS):
                o = None
                for e in range(2):
                    blk = 2 * kv + e
                    t = jnp.dot(p_ref[b, blk * rb:(blk + 1) * rb, :], vring_ref[blk, b],
                                preferred_element_type=F32)
                    o = t if o is None else o + t
                for j in range(2):
                    c0 = (2 * kv + j) * LANES
                    z = att_ref[b * tl:(b + 1) * tl, ATT_W + 2 * KV_W + c0:ATT_W + 2 * KV_W + c0 + LANES]
                    ya_ref[b, chunk_rows(c), c0:c0 + LANES] = (o[j * tl:(j + 1) * tl] * _silu(z)).astype(BF16)

    def ssm_project():
        hb = hb_ref[...]
        us_ref[:, :SSM_W] = jnp.dot(hb, wus_ref[0], preferred_element_type=F32)
        us_ref[:, SSM_W:] = jnp.dot(hb, wzs_ref[0], preferred_element_type=F32)
        u = us_ref[:, :SSM_W].reshape(nb, tl, SSM_W)
        ut_ref[...] = jnp.swapaxes(u, 0, 1).reshape(m, SSM_W).astype(BF16)

    def ssm_expand(r):
        res = jnp.dot(ut_ref[:, r * LANES:(r + 1) * LANES], bmat_ref[r], preferred_element_type=F32)
        bu_ref[:, r * half:(r + 1) * half] = res[:, :half]
        bu_ref[:, im0 + r * half:im0 + (r + 1) * half] = res[:, half:]

    def ssm_scan(r):
        c0 = r * half
        a_re = jnp.broadcast_to(lam_ref[0:1, c0:c0 + half], (nb, half))
        a_im = jnp.broadcast_to(lam_ref[1:2, c0:c0 + half], (nb, half))
        s_re = st_ref[:, c0:c0 + half]
        s_im = st_ref[:, N_STATE + c0:N_STATE + c0 + half]
        for t in range(tl):
            r0 = t * nb
            n_re = a_re * s_re - a_im * s_im + bu_ref[r0:r0 + nb, c0:c0 + half]
            n_im = a_re * s_im + a_im * s_re + bu_ref[r0:r0 + nb, im0 + c0:im0 + c0 + half]
            bu_ref[r0:r0 + nb, c0:c0 + half] = n_re
            bu_ref[r0:r0 + nb, im0 + c0:im0 + c0 + half] = n_im
            s_re, s_im = n_re, n_im
        st_ref[:, c0:c0 + half] = s_re
        st_ref[:, N_STATE + c0:N_STATE + c0 + half] = s_im

    def ssm_contract(r):
        x_re = bu_ref[:, r * half:(r + 1) * half].astype(BF16)
        x_im = bu_ref[:, im0 + r * half:im0 + (r + 1) * half].astype(BF16)
        yt_ref[:, r * LANES:(r + 1) * LANES] = (jnp.dot(x_re, cre_ref[r], preferred_element_type=F32)
                                               + jnp.dot(x_im, cim_ref[r], preferred_element_type=F32))

    def ssm_output(c):
        y = jnp.swapaxes(yt_ref[...].reshape(tl, nb, SSM_W), 0, 1).reshape(m, SSM_W)
        y = _gelu_tanh(y + dskip_ref[...] * us_ref[:, :SSM_W])
        gl = jnp.dot(y.astype(BF16), wglu_ref[...], preferred_element_type=F32) + bglu_ref[...]
        y = y * _sigmoid(gl)
        ys_ref[:, chunk_rows(c), :] = (y * _silu(us_ref[:, SSM_W:])).reshape(nb, tl, SSM_W).astype(BF16)

    def pool_project():
        hb = hb_ref[...]
        zp_ref[...] = jnp.dot(hb, wzp_ref[0], preferred_element_type=F32)
        pext_ref[:, POOL_HALO:POOL_HALO + tl, :] = jnp.dot(hb, wup_ref[0], preferred_element_type=F32
                                                           ).reshape(nb, tl, POOL_W)

    def pool_mix(c):
        tpos = chunk_index(c) * tl + lax.broadcasted_iota(jnp.int32, (1, tl, 1), 1)
        groups = []
        for gi, w in enumerate(POOL_WINDOWS):
            cs = slice(gi * POOL_GW, (gi + 1) * POOL_GW)
            ext = pext_ref[:, :, cs]
            tok = ext[:, POOL_HALO:, :]
            ssum = ext
            k = 1
            while k < w:
                ssum = ssum + pltpu.roll(ssum, k, axis=1)
                k *= 2
            cnt = jnp.minimum(tpos + 1, w).astype(F32)
            pooled = ssum[:, POOL_HALO:, :] / cnt - tok
            groups.append(jnp.dot(pooled.reshape(m, POOL_GW).astype(BF16), wpool_ref[gi],
                                  preferred_element_type=F32))
        y = jnp.concatenate(groups, axis=1) * pscale_ref[...]
        yp_ref[:, chunk_rows(c), :] = (y * _silu(zp_ref[...])).reshape(nb, tl, POOL_W).astype(BF16)
        pext_ref[:, 0:POOL_HALO, :] = pext_ref[:, tl:tl + POOL_HALO, :]

    P = functools.partial
    for c in range(CHUNKS_PER_STEP):
        schedule = (P(norm, c), att_project, pool_project, att_scores, ssm_project,
                    P(att_softmax, c, 0), P(ssm_expand, 0), P(att_softmax, c, 1), P(ssm_expand, 1),
                    P(ssm_scan, 0), P(ssm_expand, 2), P(att_softmax, c, 2), P(ssm_scan, 1), P(ssm_expand, 3),
                    P(att_softmax, c, 3), P(ssm_contract, 0), P(ssm_scan, 2), P(ssm_contract, 1), P(pool_mix, c),
                    P(ssm_scan, 3), P(att_values, c), P(ssm_contract, 2), P(ssm_contract, 3), P(ssm_output, c))
        for phase in schedule:
            phase()


def _mixers(x, li, mod, norm_g, sinks, wts):
    nb, seq, d = x.shape
    tl = CHUNK
    m = nb * tl
    consts = (wts["bmat"], wts["cre"],
              wts["cim"], wts["lam"], wts["dskip"], wts["w_glu"], wts["b_glu"], wts["w_pool"], wts["pscale"])
    w_in = wts["w_in"]
    proj_cols = ((0, ATT_W), (O_K, 2 * KV_W), (O_ZA, ATT_W), (O_US, SSM_W), (O_ZS, SSM_W), (O_UP, POOL_W),
                 (O_ZP, POOL_W))
    step_rows = CHUNKS_PER_STEP * tl
    out_spec = pl.BlockSpec((nb, step_rows, SSM_W), lambda i: (0, i, 0))
    out_shape = jax.ShapeDtypeStruct((nb, seq, SSM_W), BF16)
    return pl.pallas_call(
        functools.partial(_mixer_kernel, nb=nb, li=li),
        grid=(seq // step_rows,),
        in_specs=[pl.BlockSpec(memory_space=pltpu.SMEM),
                  pl.BlockSpec((nb, step_rows, d), lambda i: (0, i, 0)),
                  _layer_spec(mod.shape, li),
                  _layer_spec(norm_g.shape, li)]
                 + [_column_spec(w_in.shape, li, c0, width) for c0, width in proj_cols]
                 + [_layer_spec(c.shape, li) for c in consts],
        out_specs=[out_spec] * 3,
        out_shape=[out_shape] * 3,
        scratch_shapes=[pltpu.VMEM((m, d), BF16),
                        pltpu.VMEM((m, 2 * ATT_W + 2 * KV_W), F32),
                        pltpu.VMEM((nb, N_HEADS * tl, NK), F32),
                        pltpu.VMEM((nb, N_HEADS * tl, NK), BF16),
                        pltpu.VMEM((4, nb, NK, LANES), BF16),
                        pltpu.VMEM((4, nb, NK, LANES), BF16),
                        pltpu.VMEM((m, 2 * SSM_W), F32),
                        pltpu.VMEM((m, SSM_W), BF16),
                        pltpu.VMEM((m, 2 * N_STATE + LANES), F32),
                        pltpu.VMEM((nb, 2 * N_STATE), F32),
                        pltpu.VMEM((m, SSM_W), F32),
                        pltpu.VMEM((nb, POOL_HALO + tl, POOL_W), F32),
                        pltpu.VMEM((m, POOL_W), F32)],
        compiler_params=pltpu.CompilerParams(dimension_semantics=("arbitrary",), vmem_limit_bytes=VMEM_LIMIT),
        name="mixers",
    )(sinks, x, mod, norm_g, *([w_in] * len(proj_cols)), *consts)


def _merge_kernel(x_ref, mod_ref, ng_ref, ya_ref, ys_ref, yp_ref, wg_ref, wba_ref, wbs_ref, wbp_ref, wout_ref,
                  fg_ref, o_ref, *, final):
    x = x_ref[0]
    ms = jnp.mean(x * x, axis=-1, keepdims=True)
    gain = ng_ref[...] * (1.0 + mod_ref[1, 0])
    hb = (x * lax.rsqrt(ms + EPS) * gain + mod_ref[0, 0]).astype(BF16)
    merged = None
    for j, (y_ref, wb_ref) in enumerate(((ya_ref, wba_ref), (ys_ref, wbs_ref), (yp_ref, wbp_ref))):
        g = jnp.dot(hb, wg_ref[0, :, j * D_MODEL:(j + 1) * D_MODEL], preferred_element_type=F32)
        p = jnp.dot(y_ref[0], wb_ref[...], preferred_element_type=F32)
        term = _sigmoid(g) * p
        merged = term if merged is None else merged + term
    out = jnp.dot(merged.astype(BF16), wout_ref[...], preferred_element_type=F32)
    xn = x + mod_ref[2, 0] * out
    if final:
        ms = jnp.mean(xn * xn, axis=-1, keepdims=True)
        xn = xn * lax.rsqrt(ms + EPS) * fg_ref[...]
    o_ref[0] = xn


def _merge(x, li, mod, norm_g, ya, ys, yp, wts, final_g, final):
    nb, seq, d = x.shape
    tm = MERGE_ROWS
    row = lambda w: pl.BlockSpec((1, tm, w), lambda b, i: (b, i, 0))
    return pl.pallas_call(
        functools.partial(_merge_kernel, final=final),
        grid=(nb, seq // tm),
        in_specs=[row(d),
                  pl.BlockSpec((None, 3, 1, 1, d), lambda b, i: (li, 0, b, 0, 0)),
                  _layer_spec(norm_g.shape, li),
                  row(ATT_W), row(SSM_W), row(POOL_W),
                  _column_spec(wts["w_in"].shape, li, O_G, 3 * d), _layer_spec(wts["w_br_att"].shape, li),
                  _layer_spec(wts["w_br_ssm"].shape, li), _layer_spec(wts["w_br_pool"].shape, li),
                  _layer_spec(wts["w_out"].shape, li), _const_spec(final_g.shape)],
        out_specs=row(d),
        out_shape=jax.ShapeDtypeStruct(x.shape, x.dtype),
        compiler_params=pltpu.CompilerParams(dimension_semantics=("arbitrary", "arbitrary"),
                                             vmem_limit_bytes=VMEM_LIMIT),
        name="merge",
    )(x, mod, norm_g, ya, ys, yp, wts["w_in"], wts["w_br_att"], wts["w_br_ssm"], wts["w_br_pool"], wts["w_out"],
      final_g)


def _block_diag(blocks):
    *lead, n, r, c = blocks.shape
    mask = jnp.asarray(np.kron(np.eye(n, dtype=np.float32), np.ones((r, c), np.float32)))
    rows = blocks.reshape(*lead, n * r, c)
    return jnp.tile(rows, (1,) * len(lead) + (1, n)) * mask


def _stacked_weights(w_in, lam_re, lam_im, coef_re, coef_im, ssm_b_re, ssm_b_im, ssm_c_re, ssm_c_im, ssm_d,
                     w_glu, b_glu, w_pool, pool_scale, w_br_att, w_br_ssm, w_br_pool, w_out):
    depth = w_in.shape[0]
    wts = {"w_in": w_in.astype(BF16)}
    cr = coef_re[..., None]
    ci = coef_im[..., None]
    bb_re = cr * ssm_b_re - ci * ssm_b_im
    bb_im = cr * ssm_b_im + ci * ssm_b_re
    gpb = LANES // SSM_GROUP
    nblk = SSM_GROUPS // gpb
    bre = jnp.swapaxes(bb_re, 2, 3).reshape(depth, nblk, gpb, SSM_GROUP, SSM_STATE)
    bim = jnp.swapaxes(bb_im, 2, 3).reshape(depth, nblk, gpb, SSM_GROUP, SSM_STATE)
    wts["bmat"] = jnp.concatenate([_block_diag(bre), _block_diag(bim)], axis=-1).astype(BF16)
    cre = jnp.swapaxes(ssm_c_re, 2, 3).reshape(depth, nblk, gpb, SSM_STATE, SSM_GROUP)
    cim = jnp.swapaxes(-ssm_c_im, 2, 3).reshape(depth, nblk, gpb, SSM_STATE, SSM_GROUP)
    wts["cre"] = _block_diag(cre).astype(BF16)
    wts["cim"] = _block_diag(cim).astype(BF16)
    wts["lam"] = jnp.stack([lam_re.reshape(depth, N_STATE), lam_im.reshape(depth, N_STATE)], axis=1)
    wts["dskip"] = ssm_d.reshape(depth, 1, SSM_W)
    wts["w_glu"] = w_glu.astype(BF16)
    wts["b_glu"] = b_glu.reshape(depth, 1, SSM_W)
    wts["w_pool"] = w_pool.astype(BF16)
    wts["pscale"] = pool_scale.reshape(depth, 1, POOL_W)
    wts["w_br_att"] = w_br_att.astype(BF16)
    wts["w_br_ssm"] = w_br_ssm.astype(BF16)
    wts["w_br_pool"] = w_br_pool.astype(BF16)
    wts["w_out"] = w_out.astype(BF16)
    return wts


def kernel(x, c, norm_g, w_ada, b_ada, w_in, attn_sinks, ssm_a_re, ssm_a_im, ssm_log_dt, ssm_b_re, ssm_b_im,
           ssm_c_re, ssm_c_im, ssm_d, w_glu, b_glu, w_pool, pool_scale, w_br_att, w_br_ssm, w_br_pool, w_out,
           final_g):
    nb, seq, d = x.shape
    depth = w_in.shape[0]
    assert d == D_MODEL and nb == SUBLANES and seq % MERGE_ROWS == 0
    mod_all = _modulation(c, w_ada, b_ada)
    lam_re, lam_im, coef_re, coef_im = _ssm_prep(ssm_a_re, ssm_a_im, ssm_log_dt)
    fg = final_g.reshape(1, d)
    mod = jnp.transpose(mod_all.reshape(depth, nb, 3, d), (0, 2, 1, 3)).reshape(depth, 3, nb, 1, d)
    ng = norm_g.reshape(depth, 1, d)
    wts = _stacked_weights(w_in, lam_re, lam_im, coef_re, coef_im, ssm_b_re, ssm_b_im, ssm_c_re, ssm_c_im, ssm_d,
                           w_glu, b_glu, w_pool, pool_scale, w_br_att, w_br_ssm, w_br_pool, w_out)
    for li in range(depth):
        ya, ys, yp = _mixers(x, li, mod, ng, attn_sinks, wts)
        x = _merge(x, li, mod, ng, ya, ys, yp, wts, fg, final=(li == depth - 1))
    return x
```

```python
import functools
import math

import numpy as np
import jax
import jax.numpy as jnp
from jax import lax
from jax.experimental import pallas as pl
from jax.experimental.pallas import tpu as pltpu

F32 = jnp.float32
BF16 = jnp.bfloat16

D_MODEL = 1024
CHUNK = 64
N_HEADS = 8
N_KV_HEADS = 2
HEAD_DIM = 64
WIN_CHUNKS = 2
NK = (WIN_CHUNKS + 1) * CHUNK
ATT_W = N_HEADS * HEAD_DIM
KV_W = N_KV_HEADS * HEAD_DIM
SSM_W = 512
SSM_GROUP = 16
SSM_GROUPS = 32
SSM_STATE = 64
N_STATE = SSM_GROUPS * SSM_STATE
POOL_W = 512
POOL_WINDOWS = (2, 4, 8, 16)
POOL_GW = 128
POOL_HALO = 16
O_K, O_US, O_UP, O_ZA, O_ZS, O_ZP, O_G = 512, 768, 1280, 1792, 2304, 2816, 3328
CHUNKS_PER_STEP = 1
MERGE_ROWS = 1024
MOD_COLS = 1024
EPS = 1e-6
NEG_INF = -1e30
LOG2E = math.log2(math.e)
LANES = 128
SUBLANES = 8
VMEM_LIMIT = 58 * 1024 * 1024


def _sigmoid(x):
    return 0.5 * jnp.tanh(0.5 * x) + 0.5


def _silu(x):
    return x * _sigmoid(x)


def _gelu_tanh(x):
    return 0.5 * x * (1.0 + jnp.tanh(math.sqrt(2.0 / math.pi) * (x + 0.044715 * (x * x * x))))


def _const_spec(shape):
    nd = len(shape)
    return pl.BlockSpec(shape, lambda *_: (0,) * nd, pipeline_mode=pl.Buffered(1))


def _column_spec(shape, li, c0, width):
    return pl.BlockSpec((pl.Element(1), pl.Element(shape[1]), pl.Element(width)), lambda *_: (li, 0, c0),
                        pipeline_mode=pl.Buffered(1))


def _layer_spec(shape, li):
    nd = len(shape)
    return pl.BlockSpec((None,) + tuple(shape[1:]), lambda *_: (li,) + (0,) * (nd - 1),
                        pipeline_mode=pl.Buffered(1))


def _mod_kernel(c_ref, w_ref, b_ref, o_ref):
    ca = _silu(c_ref[...])
    o_ref[0] = jnp.dot(ca.astype(BF16), w_ref[0].astype(BF16), preferred_element_type=F32) + b_ref[0]


def _modulation(c, w_ada, b_ada):
    depth, d, n3 = w_ada.shape
    b = c.shape[0]
    bw = MOD_COLS
    return pl.pallas_call(
        _mod_kernel,
        grid=(depth, n3 // bw),
        in_specs=[pl.BlockSpec((b, d), lambda l, j: (0, 0)),
                  pl.BlockSpec((1, d, bw), lambda l, j: (l, 0, j)),
                  pl.BlockSpec((1, 1, bw), lambda l, j: (l, 0, j))],
        out_specs=pl.BlockSpec((1, b, bw), lambda l, j: (l, 0, j)),
        out_shape=jax.ShapeDtypeStruct((depth, b, n3), F32),
        name="adaln_mod",
    )(c, w_ada, b_ada.reshape(depth, 1, n3))


def _ssm_prep_kernel(are_ref, aim_ref, ldt_ref, lre_ref, lim_ref, cre_ref, cim_ref):
    a_re = are_ref[0]
    a_im = aim_ref[0]
    dt = jnp.exp(ldt_ref[0])
    mag = jnp.exp(a_re * dt)
    ang = a_im * dt
    l_re = mag * jnp.cos(ang)
    l_im = mag * jnp.sin(ang)
    n_re = l_re - 1.0
    den = a_re * a_re + a_im * a_im
    lre_ref[0] = l_re
    lim_ref[0] = l_im
    cre_ref[0] = (n_re * a_re + l_im * a_im) / den
    cim_ref[0] = (l_im * a_re - n_re * a_im) / den


def _ssm_prep(a_re, a_im, log_dt):
    depth, g, p = a_re.shape
    spec = pl.BlockSpec((1, g, p), lambda l: (l, 0, 0))
    out = jax.ShapeDtypeStruct((depth, g, p), F32)
    return pl.pallas_call(
        _ssm_prep_kernel,
        grid=(depth,),
        in_specs=[spec, spec, pl.BlockSpec((1, g, 1), lambda l: (l, 0, 0))],
        out_specs=[spec] * 4,
        out_shape=[out] * 4,
        name="ssm_discretise",
    )(a_re, a_im, log_dt.reshape(depth, g, 1))


def _mixer_kernel(sink_ref, x_ref, mod_ref, ng_ref, wq_ref, wkv_ref, wza_ref, wus_ref, wzs_ref, wup_ref, wzp_ref,
                  bmat_ref, cre_ref, cim_ref, lam_ref, dskip_ref, wglu_ref, bglu_ref, wpool_ref, pscale_ref,
                  ya_ref, ys_ref, yp_ref,
                  hb_ref, att_ref, s_ref, p_ref, kring_ref, vring_ref, us_ref, ut_ref, bu_ref, st_ref, yt_ref,
                  pext_ref, zp_ref,
                  *, nb, li):
    step = pl.program_id(0)
    tl = CHUNK
    m = nb * tl
    half = N_STATE // 4
    im0 = N_STATE + LANES
    hpk = N_HEADS // N_KV_HEADS
    rb = 2 * tl
    blk_heads = [(kv * hpk + e, kv * hpk + 2 + e) for kv in range(N_KV_HEADS) for e in range(2)]

    def chunk_index(c):
        return step * CHUNKS_PER_STEP + c

    def chunk_rows(c):
        return slice(c * tl, (c + 1) * tl)

    @pl.when(step == 0)
    def _init():
        kring_ref[...] = jnp.zeros_like(kring_ref)
        vring_ref[...] = jnp.zeros_like(vring_ref)
        st_ref[...] = jnp.zeros_like(st_ref)
        pext_ref[:, 0:POOL_HALO, :] = jnp.zeros((nb, POOL_HALO, POOL_W), F32)

    def norm(c):
        x = x_ref[:, chunk_rows(c), :]
        ms = jnp.mean(x * x, axis=-1, keepdims=True)
        gain = ng_ref[...] * (1.0 + mod_ref[1])
        h = x * lax.rsqrt(ms + EPS) * gain + mod_ref[0]
        hb_ref[...] = h.reshape(m, D_MODEL).astype(BF16)

    def att_project():
        hb = hb_ref[...]
        att_ref[:, :ATT_W] = jnp.dot(hb, wq_ref[0], preferred_element_type=F32) * (LOG2E / math.sqrt(HEAD_DIM))
        att_ref[:, ATT_W:ATT_W + 2 * KV_W] = jnp.dot(hb, wkv_ref[0], preferred_element_type=F32)
        att_ref[:, ATT_W + 2 * KV_W:] = jnp.dot(hb, wza_ref[0], preferred_element_type=F32)
        lane = lax.broadcasted_iota(jnp.int32, (m, LANES), 1)
        lo = lane < HEAD_DIM
        for ring, c0 in ((kring_ref, ATT_W), (vring_ref, ATT_W + KV_W)):
            new = att_ref[:, c0:c0 + KV_W]
            swp = pltpu.roll(new, HEAD_DIM, axis=1)
            dup = (jnp.where(lo, new, swp), jnp.where(lo, swp, new))
            for kv in range(N_KV_HEADS):
                for b in range(nb):
                    ring[kv, b, 0:NK - CHUNK, :] = ring[kv, b, CHUNK:NK, :]
                ring[kv, :, NK - CHUNK:NK, :] = dup[kv].reshape(nb, tl, LANES).astype(BF16)

    def att_scores():
        lo = lax.broadcasted_iota(jnp.int32, (tl, LANES), 1) < HEAD_DIM
        zero = jnp.zeros((tl, LANES), F32)
        for b in range(nb):
            for kv in range(N_KV_HEADS):
                pairs = [att_ref[b * tl:(b + 1) * tl, (2 * kv + j) * LANES:(2 * kv + j + 1) * LANES] for j in range(2)]
                qq = jnp.concatenate([jnp.where(lo, q, zero) for q in pairs] + [jnp.where(lo, zero, q) for q in pairs],
                                     axis=0).astype(BF16)
                s_ref[b, 2 * kv * rb:(2 * kv + 2) * rb, :] = lax.dot_general(
                    qq, kring_ref[kv, b], (((1,), (1,)), ((), ())), preferred_element_type=F32)

    def att_softmax(c, blk):
        ha, hb_ = blk_heads[blk]
        row = lax.broadcasted_iota(jnp.int32, (rb, NK), 0)
        kj = lax.broadcasted_iota(jnp.int32, (rb, NK), 1)
        qi = jnp.where(row < tl, row, row - tl)
        dist = jnp.abs(qi + WIN_CHUNKS * CHUNK - kj).astype(F32)
        valid = (chunk_index(c) * CHUNK + kj) >= WIN_CHUNKS * CHUNK
        first = lax.broadcasted_iota(jnp.int32, (rb, 1), 0) < tl
        slope = jnp.where(first, LOG2E * 2.0 ** (-(ha + 1)), LOG2E * 2.0 ** (-(hb_ + 1)))
        nbias = jnp.where(valid, -slope * dist, NEG_INF)
        sink = LOG2E * jnp.where(first, sink_ref[li, ha], sink_ref[li, hb_])
        for b in range(nb):
            s = s_ref[b, blk * rb:(blk + 1) * rb, :] + nbias
            mx = jnp.maximum(jnp.max(s, axis=-1, keepdims=True), sink)
            p = jnp.exp2(s - mx)
            den = jnp.sum(p, axis=-1, keepdims=True) + jnp.exp2(sink - mx)
            p_ref[b, blk * rb:(blk + 1) * rb, :] = (p * (1.0 / den)).astype(BF16)

    def att_values(c):
        lo = lax.broadcasted_iota(jnp.int32, (rb, LANES), 1) < HEAD_DIM
        for b in range(nb):
            for kv in range(N_KV_HEADS):
                t = jnp.dot(p_ref[b, 2 * kv * rb:(2 * kv + 2) * rb, :], vring_ref[kv, b],
                            preferred_element_type=F32)
                o = jnp.where(lo, t[:rb], t[rb:])
                for j in range(2):
                    c0 = (2 * kv + j) * LANES
                    z = att_ref[b * tl:(b + 1) * tl, ATT_W + 2 * KV_W + c0:ATT_W + 2 * KV_W + c0 + LANES]
                    ya_ref[b, chunk_rows(c), c0:c0 + LANES] = (o[j * tl:(j + 1) * tl] * _silu(z)).astype(BF16)

    def ssm_project():
        hb = hb_ref[...]
        us_ref[:, :SSM_W] = jnp.dot(hb, wus_ref[0], preferred_element_type=F32)
        us_ref[:, SSM_W:] = jnp.dot(hb, wzs_ref[0], preferred_element_type=F32)
        u = us_ref[:, :SSM_W].reshape(nb, tl, SSM_W)
        ut_ref[...] = jnp.swapaxes(u, 0, 1).reshape(m, SSM_W).astype(BF16)

    def ssm_expand(r):
        res = jnp.dot(ut_ref[:, r * LANES:(r + 1) * LANES], bmat_ref[r], preferred_element_type=F32)
        bu_ref[:, r * half:(r + 1) * half] = res[:, :half]
        bu_ref[:, im0 + r * half:im0 + (r + 1) * half] = res[:, half:]

    def ssm_scan(r):
        c0 = r * half
        a_re = jnp.broadcast_to(lam_ref[0:1, c0:c0 + half], (nb, half))
        a_im = jnp.broadcast_to(lam_ref[1:2, c0:c0 + half], (nb, half))
        s_re = st_ref[:, c0:c0 + half]
        s_im = st_ref[:, N_STATE + c0:N_STATE + c0 + half]
        for t in range(tl):
            r0 = t * nb
            n_re = a_re * s_re - a_im * s_im + bu_ref[r0:r0 + nb, c0:c0 + half]
            n_im = a_re * s_im + a_im * s_re + bu_ref[r0:r0 + nb, im0 + c0:im0 + c0 + half]
            bu_ref[r0:r0 + nb, c0:c0 + half] = n_re
            bu_ref[r0:r0 + nb, im0 + c0:im0 + c0 + half] = n_im
            s_re, s_im = n_re, n_im
        st_ref[:, c0:c0 + half] = s_re
        st_ref[:, N_STATE + c0:N_STATE + c0 + half] = s_im

    def ssm_contract(r):
        x_re = bu_ref[:, r * half:(r + 1) * half].astype(BF16)
        x_im = bu_ref[:, im0 + r * half:im0 + (r + 1) * half].astype(BF16)
        yt_ref[:, r * LANES:(r + 1) * LANES] = (jnp.dot(x_re, cre_ref[r], preferred_element_type=F32)
                                               + jnp.dot(x_im, cim_ref[r], preferred_element_type=F32))

    def ssm_output(c):
        y = jnp.swapaxes(yt_ref[...].reshape(tl, nb, SSM_W), 0, 1).reshape(m, SSM_W)
        y = _gelu_tanh(y + dskip_ref[...] * us_ref[:, :SSM_W])
        gl = jnp.dot(y.astype(BF16), wglu_ref[...], preferred_element_type=F32) + bglu_ref[...]
        y = y * _sigmoid(gl)
        ys_ref[:, chunk_rows(c), :] = (y * _silu(us_ref[:, SSM_W:])).reshape(nb, tl, SSM_W).astype(BF16)

    def pool_project():
        hb = hb_ref[...]
        zp_ref[...] = jnp.dot(hb, wzp_ref[0], preferred_element_type=F32)
        pext_ref[:, POOL_HALO:POOL_HALO + tl, :] = jnp.dot(hb, wup_ref[0], preferred_element_type=F32
                                                           ).reshape(nb, tl, POOL_W)

    def pool_mix(c):
        tpos = chunk_index(c) * tl + lax.broadcasted_iota(jnp.int32, (1, tl, 1), 1)
        groups = []
        for gi, w in enumerate(POOL_WINDOWS):
            cs = slice(gi * POOL_GW, (gi + 1) * POOL_GW)
            ext = pext_ref[:, :, cs]
            tok = ext[:, POOL_HALO:, :]
            ssum = ext
            k = 1
            while k < w:
                ssum = ssum + pltpu.roll(ssum, k, axis=1)
                k *= 2
            cnt = jnp.minimum(tpos + 1, w).astype(F32)
            pooled = ssum[:, POOL_HALO:, :] / cnt - tok
            groups.append(jnp.dot(pooled.reshape(m, POOL_GW).astype(BF16), wpool_ref[gi],
                                  preferred_element_type=F32))
        y = jnp.concatenate(groups, axis=1) * pscale_ref[...]
        yp_ref[:, chunk_rows(c), :] = (y * _silu(zp_ref[...])).reshape(nb, tl, POOL_W).astype(BF16)
        pext_ref[:, 0:POOL_HALO, :] = pext_ref[:, tl:tl + POOL_HALO, :]

    P = functools.partial
    for c in range(CHUNKS_PER_STEP):
        schedule = (P(norm, c), att_project, pool_project, att_scores, ssm_project,
                    P(att_softmax, c, 0), P(ssm_expand, 0), P(att_softmax, c, 1), P(ssm_expand, 1),
                    P(ssm_scan, 0), P(ssm_expand, 2), P(att_softmax, c, 2), P(ssm_scan, 1), P(ssm_expand, 3),
                    P(att_softmax, c, 3), P(ssm_contract, 0), P(ssm_scan, 2), P(ssm_contract, 1), P(pool_mix, c),
                    P(ssm_scan, 3), P(att_values, c), P(ssm_contract, 2), P(ssm_contract, 3), P(ssm_output, c))
        for phase in schedule:
            phase()


def _mixers(x, li, mod, norm_g, sinks, wts):
    nb, seq, d = x.shape
    tl = CHUNK
    m = nb * tl
    consts = (wts["bmat"], wts["cre"],
              wts["cim"], wts["lam"], wts["dskip"], wts["w_glu"], wts["b_glu"], wts["w_pool"], wts["pscale"])
    w_in = wts["w_in"]
    proj_cols = ((0, ATT_W), (O_K, 2 * KV_W), (O_ZA, ATT_W), (O_US, SSM_W), (O_ZS, SSM_W), (O_UP, POOL_W),
                 (O_ZP, POOL_W))
    step_rows = CHUNKS_PER_STEP * tl
    out_spec = pl.BlockSpec((nb, step_rows, SSM_W), lambda i: (0, i, 0))
    out_shape = jax.ShapeDtypeStruct((nb, seq, SSM_W), BF16)
    return pl.pallas_call(
        functools.partial(_mixer_kernel, nb=nb, li=li),
        grid=(seq // step_rows,),
        in_specs=[pl.BlockSpec(memory_space=pltpu.SMEM),
                  pl.BlockSpec((nb, step_rows, d), lambda i: (0, i, 0)),
                  _layer_spec(mod.shape, li),
                  _layer_spec(norm_g.shape, li)]
                 + [_column_spec(w_in.shape, li, c0, width) for c0, width in proj_cols]
                 + [_layer_spec(c.shape, li) for c in consts],
        out_specs=[out_spec] * 3,
        out_shape=[out_shape] * 3,
        scratch_shapes=[pltpu.VMEM((m, d), BF16),
                        pltpu.VMEM((m, 2 * ATT_W + 2 * KV_W), F32),
                        pltpu.VMEM((nb, N_HEADS * tl, NK), F32),
                        pltpu.VMEM((nb, N_HEADS * tl, NK), BF16),
                        pltpu.VMEM((N_KV_HEADS, nb, NK, LANES), BF16),
                        pltpu.VMEM((N_KV_HEADS, nb, NK, LANES), BF16),
                        pltpu.VMEM((m, 2 * SSM_W), F32),
                        pltpu.VMEM((m, SSM_W), BF16),
                        pltpu.VMEM((m, 2 * N_STATE + LANES), F32),
                        pltpu.VMEM((nb, 2 * N_STATE), F32),
                        pltpu.VMEM((m, SSM_W), F32),
                        pltpu.VMEM((nb, POOL_HALO + tl, POOL_W), F32),
                        pltpu.VMEM((m, POOL_W), F32)],
        compiler_params=pltpu.CompilerParams(dimension_semantics=("arbitrary",), vmem_limit_bytes=VMEM_LIMIT),
        name="mixers",
    )(sinks, x, mod, norm_g, *([w_in] * len(proj_cols)), *consts)


def _merge_kernel(x_ref, mod_ref, ng_ref, ya_ref, ys_ref, yp_ref, wg_ref, wba_ref, wbs_ref, wbp_ref, wout_ref,
                  fg_ref, o_ref, *, final):
    x = x_ref[0]
    ms = jnp.mean(x * x, axis=-1, keepdims=True)
    gain = ng_ref[...] * (1.0 + mod_ref[1, 0])
    hb = (x * lax.rsqrt(ms + EPS) * gain + mod_ref[0, 0]).astype(BF16)
    merged = None
    for j, (y_ref, wb_ref) in enumerate(((ya_ref, wba_ref), (ys_ref, wbs_ref), (yp_ref, wbp_ref))):
        g = jnp.dot(hb, wg_ref[0, :, j * D_MODEL:(j + 1) * D_MODEL], preferred_element_type=F32)
        p = jnp.dot(y_ref[0], wb_ref[...], preferred_element_type=F32)
        term = _sigmoid(g) * p
        merged = term if merged is None else merged + term
    out = jnp.dot(merged.astype(BF16), wout_ref[...], preferred_element_type=F32)
    xn = x + mod_ref[2, 0] * out
    if final:
        ms = jnp.mean(xn * xn, axis=-1, keepdims=True)
        xn = xn * lax.rsqrt(ms + EPS) * fg_ref[...]
    o_ref[0] = xn


def _merge(x, li, mod, norm_g, ya, ys, yp, wts, final_g, final):
    nb, seq, d = x.shape
    tm = MERGE_ROWS
    row = lambda w: pl.BlockSpec((1, tm, w), lambda b, i: (b, i, 0))
    return pl.pallas_call(
        functools.partial(_merge_kernel, final=final),
        grid=(nb, seq // tm),
        in_specs=[row(d),
                  pl.BlockSpec((None, 3, 1, 1, d), lambda b, i: (li, 0, b, 0, 0)),
                  _layer_spec(norm_g.shape, li),
                  row(ATT_W), row(SSM_W), row(POOL_W),
                  _column_spec(wts["w_in"].shape, li, O_G, 3 * d), _layer_spec(wts["w_br_att"].shape, li),
                  _layer_spec(wts["w_br_ssm"].shape, li), _layer_spec(wts["w_br_pool"].shape, li),
                  _layer_spec(wts["w_out"].shape, li), _const_spec(final_g.shape)],
        out_specs=row(d),
        out_shape=jax.ShapeDtypeStruct(x.shape, x.dtype),
        compiler_params=pltpu.CompilerParams(dimension_semantics=("arbitrary", "arbitrary"),
                                             vmem_limit_bytes=VMEM_LIMIT),
        name="merge",
    )(x, mod, norm_g, ya, ys, yp, wts["w_in"], wts["w_br_att"], wts["w_br_ssm"], wts["w_br_pool"], wts["w_out"],
      final_g)


def _block_diag(blocks):
    *lead, n, r, c = blocks.shape
    mask = jnp.asarray(np.kron(np.eye(n, dtype=np.float32), np.ones((r, c), np.float32)))
    rows = blocks.reshape(*lead, n * r, c)
    return jnp.tile(rows, (1,) * len(lead) + (1, n)) * mask


def _stacked_weights(w_in, lam_re, lam_im, coef_re, coef_im, ssm_b_re, ssm_b_im, ssm_c_re, ssm_c_im, ssm_d,
                     w_glu, b_glu, w_pool, pool_scale, w_br_att, w_br_ssm, w_br_pool, w_out):
    depth = w_in.shape[0]
    wts = {"w_in": w_in.astype(BF16)}
    cr = coef_re[..., None]
    ci = coef_im[..., None]
    bb_re = cr * ssm_b_re - ci * ssm_b_im
    bb_im = cr * ssm_b_im + ci * ssm_b_re
    gpb = LANES // SSM_GROUP
    nblk = SSM_GROUPS // gpb
    bre = jnp.swapaxes(bb_re, 2, 3).reshape(depth, nblk, gpb, SSM_GROUP, SSM_STATE)
    bim = jnp.swapaxes(bb_im, 2, 3).reshape(depth, nblk, gpb, SSM_GROUP, SSM_STATE)
    wts["bmat"] = jnp.concatenate([_block_diag(bre), _block_diag(bim)], axis=-1).astype(BF16)
    cre = jnp.swapaxes(ssm_c_re, 2, 3).reshape(depth, nblk, gpb, SSM_STATE, SSM_GROUP)
    cim = jnp.swapaxes(-ssm_c_im, 2, 3).reshape(depth, nblk, gpb, SSM_STATE, SSM_GROUP)
    wts["cre"] = _block_diag(cre).astype(BF16)
    wts["cim"] = _block_diag(cim).astype(BF16)
    wts["lam"] = jnp.stack([lam_re.reshape(depth, N_STATE), lam_im.reshape(depth, N_STATE)], axis=1)
    wts["dskip"] = ssm_d.reshape(depth, 1, SSM_W)
    wts["w_glu"] = w_glu.astype(BF16)
    wts["b_glu"] = b_glu.reshape(depth, 1, SSM_W)
    wts["w_pool"] = w_pool.astype(BF16)
    wts["pscale"] = pool_scale.reshape(depth, 1, POOL_W)
    wts["w_br_att"] = w_br_att.astype(BF16)
    wts["w_br_ssm"] = w_br_ssm.astype(BF16)
    wts["w_br_pool"] = w_br_pool.astype(BF16)
    wts["w_out"] = w_out.astype(BF16)
    return wts


def kernel(x, c, norm_g, w_ada, b_ada, w_in, attn_sinks, ssm_a_re, ssm_a_im, ssm_log_dt, ssm_b_re, ssm_b_im,
           ssm_c_re, ssm_c_im, ssm_d, w_glu, b_glu, w_pool, pool_scale, w_br_att, w_br_ssm, w_br_pool, w_out,
           final_g):
    nb, seq, d = x.shape
    depth = w_in.shape[0]
    assert d == D_MODEL and nb == SUBLANES and seq % MERGE_ROWS == 0
    mod_all = _modulation(c, w_ada, b_ada)
    lam_re, lam_im, coef_re, coef_im = _ssm_prep(ssm_a_re, ssm_a_im, ssm_log_dt)
    fg = final_g.reshape(1, d)
    mod = jnp.transpose(mod_all.reshape(depth, nb, 3, d), (0, 2, 1, 3)).reshape(depth, 3, nb, 1, d)
    ng = norm_g.reshape(depth, 1, d)
    wts = _stacked_weights(w_in, lam_re, lam_im, coef_re, coef_im, ssm_b_re, ssm_b_im, ssm_c_re, ssm_c_im, ssm_d,
                           w_glu, b_glu, w_pool, pool_scale, w_br_att, w_br_ssm, w_br_pool, w_out)
    for li in range(depth):
        ya, ys, yp = _mixers(x, li, mod, ng, attn_sinks, wts)
        x = _merge(x, li, mod, ng, ya, ys, yp, wts, fg, final=(li == depth - 1))
    return x
```

```python
import functools
import math

import numpy as np
import jax
import jax.numpy as jnp
from jax import lax
from jax.experimental import pallas as pl
from jax.experimental.pallas import tpu as pltpu

F32 = jnp.float32
BF16 = jnp.bfloat16

D_MODEL = 1024
CHUNK = 64
N_HEADS = 8
N_KV_HEADS = 2
HEAD_DIM = 64
WIN_CHUNKS = 2
NK = (WIN_CHUNKS + 1) * CHUNK
ATT_W = N_HEADS * HEAD_DIM
KV_W = N_KV_HEADS * HEAD_DIM
SSM_W = 512
SSM_GROUP = 16
SSM_GROUPS = 32
SSM_STATE = 64
N_STATE = SSM_GROUPS * SSM_STATE
POOL_W = 512
POOL_WINDOWS = (2, 4, 8, 16)
POOL_GW = 128
POOL_HALO = 16
O_K, O_US, O_UP, O_ZA, O_ZS, O_ZP, O_G = 512, 768, 1280, 1792, 2304, 2816, 3328
CHUNKS_PER_STEP = 1
MERGE_ROWS = 1024
MOD_COLS = 1024
EPS = 1e-6
NEG_INF = -1e30
LOG2E = math.log2(math.e)
LANES = 128
SUBLANES = 8
VMEM_LIMIT = 58 * 1024 * 1024


def _sigmoid(x):
    return 0.5 * jnp.tanh(0.5 * x) + 0.5


def _silu(x):
    return x * _sigmoid(x)


def _gelu_tanh(x):
    return 0.5 * x * (1.0 + jnp.tanh(math.sqrt(2.0 / math.pi) * (x + 0.044715 * (x * x * x))))


def _const_spec(shape):
    nd = len(shape)
    return pl.BlockSpec(shape, lambda *_: (0,) * nd, pipeline_mode=pl.Buffered(1))


def _column_spec(shape, li, c0, width):
    return pl.BlockSpec((pl.Element(1), pl.Element(shape[1]), pl.Element(width)), lambda *_: (li, 0, c0),
                        pipeline_mode=pl.Buffered(1))


def _layer_spec(shape, li):
    nd = len(shape)
    return pl.BlockSpec((None,) + tuple(shape[1:]), lambda *_: (li,) + (0,) * (nd - 1),
                        pipeline_mode=pl.Buffered(1))


def _mod_kernel(c_ref, w_ref, b_ref, o_ref):
    ca = _silu(c_ref[...])
    o_ref[0] = jnp.dot(ca.astype(BF16), w_ref[0].astype(BF16), preferred_element_type=F32) + b_ref[0]


def _modulation(c, w_ada, b_ada):
    depth, d, n3 = w_ada.shape
    b = c.shape[0]
    bw = MOD_COLS
    return pl.pallas_call(
        _mod_kernel,
        grid=(depth, n3 // bw),
        in_specs=[pl.BlockSpec((b, d), lambda l, j: (0, 0)),
                  pl.BlockSpec((1, d, bw), lambda l, j: (l, 0, j)),
                  pl.BlockSpec((1, 1, bw), lambda l, j: (l, 0, j))],
        out_specs=pl.BlockSpec((1, b, bw), lambda l, j: (l, 0, j)),
        out_shape=jax.ShapeDtypeStruct((depth, b, n3), F32),
        name="adaln_mod",
    )(c, w_ada, b_ada.reshape(depth, 1, n3))


def _ssm_prep_kernel(are_ref, aim_ref, ldt_ref, lre_ref, lim_ref, cre_ref, cim_ref):
    a_re = are_ref[0]
    a_im = aim_ref[0]
    dt = jnp.exp(ldt_ref[0])
    mag = jnp.exp(a_re * dt)
    ang = a_im * dt
    l_re = mag * jnp.cos(ang)
    l_im = mag * jnp.sin(ang)
    n_re = l_re - 1.0
    den = a_re * a_re + a_im * a_im
    lre_ref[0] = l_re
    lim_ref[0] = l_im
    cre_ref[0] = (n_re * a_re + l_im * a_im) / den
    cim_ref[0] = (l_im * a_re - n_re * a_im) / den


def _ssm_prep(a_re, a_im, log_dt):
    depth, g, p = a_re.shape
    spec = pl.BlockSpec((1, g, p), lambda l: (l, 0, 0))
    out = jax.ShapeDtypeStruct((depth, g, p), F32)
    return pl.pallas_call(
        _ssm_prep_kernel,
        grid=(depth,),
        in_specs=[spec, spec, pl.BlockSpec((1, g, 1), lambda l: (l, 0, 0))],
        out_specs=[spec] * 4,
        out_shape=[out] * 4,
        name="ssm_discretise",
    )(a_re, a_im, log_dt.reshape(depth, g, 1))


def _mixer_kernel(sink_ref, x_ref, mod_ref, ng_ref, wq32, wkv32, wza32, wus32, wzs32, wup32, wzp32,
                  bmat_ref, cre_ref, cim_ref, lam_ref, dskip_ref, wglu_ref, bglu_ref, wpool_ref, pscale_ref,
                  ya_ref, ys_ref, yp_ref,
                  hb_ref, att_ref, s_ref, p_ref, kring_ref, vring_ref, us_ref, ut_ref, bu_ref, st_ref, yt_ref,
                  pext_ref, zp_ref, wq_ref, wkv_ref, wza_ref, wus_ref, wzs_ref, wup_ref, wzp_ref,
                  *, nb, li):
    step = pl.program_id(0)
    tl = CHUNK
    m = nb * tl
    half = N_STATE // 4
    im0 = N_STATE + LANES
    hpk = N_HEADS // N_KV_HEADS
    rb = 2 * tl
    blk_heads = [(kv * hpk + e, kv * hpk + 2 + e) for kv in range(N_KV_HEADS) for e in range(2)]

    def chunk_index(c):
        return step * CHUNKS_PER_STEP + c

    def chunk_rows(c):
        return slice(c * tl, (c + 1) * tl)

    @pl.when(step == 0)
    def _init():
        kring_ref[...] = jnp.zeros_like(kring_ref)
        vring_ref[...] = jnp.zeros_like(vring_ref)
        st_ref[...] = jnp.zeros_like(st_ref)
        pext_ref[:, 0:POOL_HALO, :] = jnp.zeros((nb, POOL_HALO, POOL_W), F32)
        for src, dst in ((wq32, wq_ref), (wkv32, wkv_ref), (wza32, wza_ref), (wus32, wus_ref), (wzs32, wzs_ref),
                         (wup32, wup_ref), (wzp32, wzp_ref)):
            dst[...] = src[0].astype(BF16)

    def norm(c):
        x = x_ref[:, chunk_rows(c), :]
        ms = jnp.mean(x * x, axis=-1, keepdims=True)
        gain = ng_ref[...] * (1.0 + mod_ref[1])
        h = x * lax.rsqrt(ms + EPS) * gain + mod_ref[0]
        hb_ref[...] = h.reshape(m, D_MODEL).astype(BF16)

    def att_project():
        hb = hb_ref[...]
        att_ref[:, :ATT_W] = jnp.dot(hb, wq_ref[...], preferred_element_type=F32) * (LOG2E / math.sqrt(HEAD_DIM))
        att_ref[:, ATT_W:ATT_W + 2 * KV_W] = jnp.dot(hb, wkv_ref[...], preferred_element_type=F32)
        att_ref[:, ATT_W + 2 * KV_W:] = jnp.dot(hb, wza_ref[...], preferred_element_type=F32)
        lane = lax.broadcasted_iota(jnp.int32, (m, LANES), 1)
        lo = lane < HEAD_DIM
        for ring, c0 in ((kring_ref, ATT_W), (vring_ref, ATT_W + KV_W)):
            new = att_ref[:, c0:c0 + KV_W]
            swp = pltpu.roll(new, HEAD_DIM, axis=1)
            dup = (jnp.where(lo, new, swp), jnp.where(lo, swp, new))
            for kv in range(N_KV_HEADS):
                for b in range(nb):
                    ring[kv, b, 0:NK - CHUNK, :] = ring[kv, b, CHUNK:NK, :]
                ring[kv, :, NK - CHUNK:NK, :] = dup[kv].reshape(nb, tl, LANES).astype(BF16)

    def att_scores():
        lo = lax.broadcasted_iota(jnp.int32, (tl, LANES), 1) < HEAD_DIM
        zero = jnp.zeros((tl, LANES), F32)
        for b in range(nb):
            for kv in range(N_KV_HEADS):
                pairs = [att_ref[b * tl:(b + 1) * tl, (2 * kv + j) * LANES:(2 * kv + j + 1) * LANES] for j in range(2)]
                qq = jnp.concatenate([jnp.where(lo, q, zero) for q in pairs] + [jnp.where(lo, zero, q) for q in pairs],
                                     axis=0).astype(BF16)
                s_ref[b, 2 * kv * rb:(2 * kv + 2) * rb, :] = lax.dot_general(
                    qq, kring_ref[kv, b], (((1,), (1,)), ((), ())), preferred_element_type=F32)

    def att_softmax(c, blk):
        ha, hb_ = blk_heads[blk]
        row = lax.broadcasted_iota(jnp.int32, (rb, NK), 0)
        kj = lax.broadcasted_iota(jnp.int32, (rb, NK), 1)
        qi = jnp.where(row < tl, row, row - tl)
        dist = jnp.abs(qi + WIN_CHUNKS * CHUNK - kj).astype(F32)
        valid = (chunk_index(c) * CHUNK + kj) >= WIN_CHUNKS * CHUNK
        first = lax.broadcasted_iota(jnp.int32, (rb, 1), 0) < tl
        slope = jnp.where(first, LOG2E * 2.0 ** (-(ha + 1)), LOG2E * 2.0 ** (-(hb_ + 1)))
        nbias = jnp.where(valid, -slope * dist, NEG_INF)
        sink = LOG2E * jnp.where(first, sink_ref[li, ha], sink_ref[li, hb_])
        for b in range(nb):
            s = s_ref[b, blk * rb:(blk + 1) * rb, :] + nbias
            mx = jnp.maximum(jnp.max(s, axis=-1, keepdims=True), sink)
            p = jnp.exp2(s - mx)
            den = jnp.sum(p, axis=-1, keepdims=True) + jnp.exp2(sink - mx)
            p_ref[b, blk * rb:(blk + 1) * rb, :] = (p * (1.0 / den)).astype(BF16)

    def att_values(c):
        lo = lax.broadcasted_iota(jnp.int32, (rb, LANES), 1) < HEAD_DIM
        for b in range(nb):
            for kv in range(N_KV_HEADS):
                t = jnp.dot(p_ref[b, 2 * kv * rb:(2 * kv + 2) * rb, :], vring_ref[kv, b],
                            preferred_element_type=F32)
                o = jnp.where(lo, t[:rb], t[rb:])
                for j in range(2):
                    c0 = (2 * kv + j) * LANES
                    z = att_ref[b * tl:(b + 1) * tl, ATT_W + 2 * KV_W + c0:ATT_W + 2 * KV_W + c0 + LANES]
                    ya_ref[b, chunk_rows(c), c0:c0 + LANES] = (o[j * tl:(j + 1) * tl] * _silu(z)).astype(BF16)

    def ssm_project():
        hb = hb_ref[...]
        us_ref[:, :SSM_W] = jnp.dot(hb, wus_ref[...], preferred_element_type=F32)
        us_ref[:, SSM_W:] = jnp.dot(hb, wzs_ref[...], preferred_element_type=F32)
        u = us_ref[:, :SSM_W].reshape(nb, tl, SSM_W)
        ut_ref[...] = jnp.swapaxes(u, 0, 1).reshape(m, SSM_W).astype(BF16)

    def ssm_expand(r):
        res = jnp.dot(ut_ref[:, r * LANES:(r + 1) * LANES], bmat_ref[r], preferred_element_type=F32)
        bu_ref[:, r * half:(r + 1) * half] = res[:, :half]
        bu_ref[:, im0 + r * half:im0 + (r + 1) * half] = res[:, half:]

    def ssm_scan(r):
        c0 = r * half
        a_re = jnp.broadcast_to(lam_ref[0:1, c0:c0 + half], (nb, half))
        a_im = jnp.broadcast_to(lam_ref[1:2, c0:c0 + half], (nb, half))
        s_re = st_ref[:, c0:c0 + half]
        s_im = st_ref[:, N_STATE + c0:N_STATE + c0 + half]
        for t in range(tl):
            r0 = t * nb
            n_re = a_re * s_re - a_im * s_im + bu_ref[r0:r0 + nb, c0:c0 + half]
            n_im = a_re * s_im + a_im * s_re + bu_ref[r0:r0 + nb, im0 + c0:im0 + c0 + half]
            bu_ref[r0:r0 + nb, c0:c0 + half] = n_re
            bu_ref[r0:r0 + nb, im0 + c0:im0 + c0 + half] = n_im
            s_re, s_im = n_re, n_im
        st_ref[:, c0:c0 + half] = s_re
        st_ref[:, N_STATE + c0:N_STATE + c0 + half] = s_im

    def ssm_contract(r):
        x_re = bu_ref[:, r * half:(r + 1) * half].astype(BF16)
        x_im = bu_ref[:, im0 + r * half:im0 + (r + 1) * half].astype(BF16)
        yt_ref[:, r * LANES:(r + 1) * LANES] = (jnp.dot(x_re, cre_ref[r], preferred_element_type=F32)
                                               + jnp.dot(x_im, cim_ref[r], preferred_element_type=F32))

    def ssm_output(c):
        y = jnp.swapaxes(yt_ref[...].reshape(tl, nb, SSM_W), 0, 1).reshape(m, SSM_W)
        y = _gelu_tanh(y + dskip_ref[...] * us_ref[:, :SSM_W])
        gl = jnp.dot(y.astype(BF16), wglu_ref[...], preferred_element_type=F32) + bglu_ref[...]
        y = y * _sigmoid(gl)
        ys_ref[:, chunk_rows(c), :] = (y * _silu(us_ref[:, SSM_W:])).reshape(nb, tl, SSM_W).astype(BF16)

    def pool_project():
        hb = hb_ref[...]
        zp_ref[...] = jnp.dot(hb, wzp_ref[...], preferred_element_type=F32)
        pext_ref[:, POOL_HALO:POOL_HALO + tl, :] = jnp.dot(hb, wup_ref[...], preferred_element_type=F32
                                                           ).reshape(nb, tl, POOL_W)

    def pool_mix(c):
        tpos = chunk_index(c) * tl + lax.broadcasted_iota(jnp.int32, (1, tl, 1), 1)
        groups = []
        for gi, w in enumerate(POOL_WINDOWS):
            cs = slice(gi * POOL_GW, (gi + 1) * POOL_GW)
            ext = pext_ref[:, :, cs]
            tok = ext[:, POOL_HALO:, :]
            ssum = ext
            k = 1
            while k < w:
                ssum = ssum + pltpu.roll(ssum, k, axis=1)
                k *= 2
            cnt = jnp.minimum(tpos + 1, w).astype(F32)
            pooled = ssum[:, POOL_HALO:, :] / cnt - tok
            groups.append(jnp.dot(pooled.reshape(m, POOL_GW).astype(BF16), wpool_ref[gi],
                                  preferred_element_type=F32))
        y = jnp.concatenate(groups, axis=1) * pscale_ref[...]
        yp_ref[:, chunk_rows(c), :] = (y * _silu(zp_ref[...])).reshape(nb, tl, POOL_W).astype(BF16)
        pext_ref[:, 0:POOL_HALO, :] = pext_ref[:, tl:tl + POOL_HALO, :]

    P = functools.partial
    for c in range(CHUNKS_PER_STEP):
        schedule = (P(norm, c), att_project, pool_project, att_scores, ssm_project,
                    P(att_softmax, c, 0), P(ssm_expand, 0), P(att_softmax, c, 1), P(ssm_expand, 1),
                    P(ssm_scan, 0), P(ssm_expand, 2), P(att_softmax, c, 2), P(ssm_scan, 1), P(ssm_expand, 3),
                    P(att_softmax, c, 3), P(ssm_contract, 0), P(ssm_scan, 2), P(ssm_contract, 1), P(pool_mix, c),
                    P(ssm_scan, 3), P(att_values, c), P(ssm_contract, 2), P(ssm_contract, 3), P(ssm_output, c))
        for phase in schedule:
            phase()


def _mixers(x, li, mod, norm_g, sinks, wts):
    nb, seq, d = x.shape
    tl = CHUNK
    m = nb * tl
    consts = (wts["bmat"], wts["cre"],
              wts["cim"], wts["lam"], wts["dskip"], wts["w_glu"], wts["b_glu"], wts["w_pool"], wts["pscale"])
    w_in = wts["w_in"]
    proj_cols = ((0, ATT_W), (O_K, 2 * KV_W), (O_ZA, ATT_W), (O_US, SSM_W), (O_ZS, SSM_W), (O_UP, POOL_W),
                 (O_ZP, POOL_W))
    step_rows = CHUNKS_PER_STEP * tl
    out_spec = pl.BlockSpec((nb, step_rows, SSM_W), lambda i: (0, i, 0))
    out_shape = jax.ShapeDtypeStruct((nb, seq, SSM_W), BF16)
    return pl.pallas_call(
        functools.partial(_mixer_kernel, nb=nb, li=li),
        grid=(seq // step_rows,),
        in_specs=[pl.BlockSpec(memory_space=pltpu.SMEM),
                  pl.BlockSpec((nb, step_rows, d), lambda i: (0, i, 0)),
                  _layer_spec(mod.shape, li),
                  _layer_spec(norm_g.shape, li)]
                 + [_column_spec(w_in.shape, li, c0, width) for c0, width in proj_cols]
                 + [_layer_spec(c.shape, li) for c in consts],
        out_specs=[out_spec] * 3,
        out_shape=[out_shape] * 3,
        scratch_shapes=[pltpu.VMEM((m, d), BF16),
                        pltpu.VMEM((m, 2 * ATT_W + 2 * KV_W), F32),
                        pltpu.VMEM((nb, N_HEADS * tl, NK), F32),
                        pltpu.VMEM((nb, N_HEADS * tl, NK), BF16),
                        pltpu.VMEM((N_KV_HEADS, nb, NK, LANES), BF16),
                        pltpu.VMEM((N_KV_HEADS, nb, NK, LANES), BF16),
                        pltpu.VMEM((m, 2 * SSM_W), F32),
                        pltpu.VMEM((m, SSM_W), BF16),
                        pltpu.VMEM((m, 2 * N_STATE + LANES), F32),
                        pltpu.VMEM((nb, 2 * N_STATE), F32),
                        pltpu.VMEM((m, SSM_W), F32),
                        pltpu.VMEM((nb, POOL_HALO + tl, POOL_W), F32),
                        pltpu.VMEM((m, POOL_W), F32)]
                       + [pltpu.VMEM((d, width), BF16) for _, width in proj_cols],
        compiler_params=pltpu.CompilerParams(dimension_semantics=("arbitrary",), vmem_limit_bytes=VMEM_LIMIT),
        name="mixers",
    )(sinks, x, mod, norm_g, *([w_in] * len(proj_cols)), *consts)


def _merge_kernel(x_ref, mod_ref, ng_ref, ya_ref, ys_ref, yp_ref, wg32, wba_ref, wbs_ref, wbp_ref, wout_ref,
                  fg_ref, o_ref, wg_ref, *, final):
    @pl.when((pl.program_id(0) == 0) & (pl.program_id(1) == 0))
    def _cast_weights():
        wg_ref[...] = wg32[0].astype(BF16)

    x = x_ref[0]
    ms = jnp.mean(x * x, axis=-1, keepdims=True)
    gain = ng_ref[...] * (1.0 + mod_ref[1, 0])
    hb = (x * lax.rsqrt(ms + EPS) * gain + mod_ref[0, 0]).astype(BF16)
    merged = None
    for j, (y_ref, wb_ref) in enumerate(((ya_ref, wba_ref), (ys_ref, wbs_ref), (yp_ref, wbp_ref))):
        g = jnp.dot(hb, wg_ref[:, j * D_MODEL:(j + 1) * D_MODEL], preferred_element_type=F32)
        p = jnp.dot(y_ref[0], wb_ref[...], preferred_element_type=F32)
        term = _sigmoid(g) * p
        merged = term if merged is None else merged + term
    out = jnp.dot(merged.astype(BF16), wout_ref[...], preferred_element_type=F32)
    xn = x + mod_ref[2, 0] * out
    if final:
        ms = jnp.mean(xn * xn, axis=-1, keepdims=True)
        xn = xn * lax.rsqrt(ms + EPS) * fg_ref[...]
    o_ref[0] = xn


def _merge(x, li, mod, norm_g, ya, ys, yp, wts, final_g, final):
    nb, seq, d = x.shape
    tm = MERGE_ROWS
    row = lambda w: pl.BlockSpec((1, tm, w), lambda b, i: (b, i, 0))
    return pl.pallas_call(
        functools.partial(_merge_kernel, final=final),
        grid=(nb, seq // tm),
        in_specs=[row(d),
                  pl.BlockSpec((None, 3, 1, 1, d), lambda b, i: (li, 0, b, 0, 0)),
                  _layer_spec(norm_g.shape, li),
                  row(ATT_W), row(SSM_W), row(POOL_W),
                  _column_spec(wts["w_in"].shape, li, O_G, 3 * d), _layer_spec(wts["w_br_att"].shape, li),
                  _layer_spec(wts["w_br_ssm"].shape, li), _layer_spec(wts["w_br_pool"].shape, li),
                  _layer_spec(wts["w_out"].shape, li), _const_spec(final_g.shape)],
        out_specs=row(d),
        out_shape=jax.ShapeDtypeStruct(x.shape, x.dtype),
        scratch_shapes=[pltpu.VMEM((d, 3 * d), BF16)],
        compiler_params=pltpu.CompilerParams(dimension_semantics=("arbitrary", "arbitrary"),
                                             vmem_limit_bytes=VMEM_LIMIT),
        name="merge",
    )(x, mod, norm_g, ya, ys, yp, wts["w_in"], wts["w_br_att"], wts["w_br_ssm"], wts["w_br_pool"], wts["w_out"],
      final_g)


def _block_diag(blocks):
    *lead, n, r, c = blocks.shape
    mask = jnp.asarray(np.kron(np.eye(n, dtype=np.float32), np.ones((r, c), np.float32)))
    rows = blocks.reshape(*lead, n * r, c)
    return jnp.tile(rows, (1,) * len(lead) + (1, n)) * mask


def _stacked_weights(w_in, lam_re, lam_im, coef_re, coef_im, ssm_b_re, ssm_b_im, ssm_c_re, ssm_c_im, ssm_d,
                     w_glu, b_glu, w_pool, pool_scale, w_br_att, w_br_ssm, w_br_pool, w_out):
    depth = w_in.shape[0]
    wts = {"w_in": w_in}
    cr = coef_re[..., None]
    ci = coef_im[..., None]
    bb_re = cr * ssm_b_re - ci * ssm_b_im
    bb_im = cr * ssm_b_im + ci * ssm_b_re
    gpb = LANES // SSM_GROUP
    nblk = SSM_GROUPS // gpb
    bre = jnp.swapaxes(bb_re, 2, 3).reshape(depth, nblk, gpb, SSM_GROUP, SSM_STATE)
    bim = jnp.swapaxes(bb_im, 2, 3).reshape(depth, nblk, gpb, SSM_GROUP, SSM_STATE)
    wts["bmat"] = jnp.concatenate([_block_diag(bre), _block_diag(bim)], axis=-1).astype(BF16)
    cre = jnp.swapaxes(ssm_c_re, 2, 3).reshape(depth, nblk, gpb, SSM_STATE, SSM_GROUP)
    cim = jnp.swapaxes(-ssm_c_im, 2, 3).reshape(depth, nblk, gpb, SSM_STATE, SSM_GROUP)
    wts["cre"] = _block_diag(cre).astype(BF16)
    wts["cim"] = _block_diag(cim).astype(BF16)
    wts["lam"] = jnp.stack([lam_re.reshape(depth, N_STATE), lam_im.reshape(depth, N_STATE)], axis=1)
    wts["dskip"] = ssm_d.reshape(depth, 1, SSM_W)
    wts["w_glu"] = w_glu.astype(BF16)
    wts["b_glu"] = b_glu.reshape(depth, 1, SSM_W)
    wts["w_pool"] = w_pool.astype(BF16)
    wts["pscale"] = pool_scale.reshape(depth, 1, POOL_W)
    wts["w_br_att"] = w_br_att.astype(BF16)
    wts["w_br_ssm"] = w_br_ssm.astype(BF16)
    wts["w_br_pool"] = w_br_pool.astype(BF16)
    wts["w_out"] = w_out.astype(BF16)
    return wts


def kernel(x, c, norm_g, w_ada, b_ada, w_in, attn_sinks, ssm_a_re, ssm_a_im, ssm_log_dt, ssm_b_re, ssm_b_im,
           ssm_c_re, ssm_c_im, ssm_d, w_glu, b_glu, w_pool, pool_scale, w_br_att, w_br_ssm, w_br_pool, w_out,
           final_g):
    nb, seq, d = x.shape
    depth = w_in.shape[0]
    assert d == D_MODEL and nb == SUBLANES and seq % MERGE_ROWS == 0
    mod_all = _modulation(c, w_ada, b_ada)
    lam_re, lam_im, coef_re, coef_im = _ssm_prep(ssm_a_re, ssm_a_im, ssm_log_dt)
    fg = final_g.reshape(1, d)
    mod = jnp.transpose(mod_all.reshape(depth, nb, 3, d), (0, 2, 1, 3)).reshape(depth, 3, nb, 1, d)
    ng = norm_g.reshape(depth, 1, d)
    wts = _stacked_weights(w_in, lam_re, lam_im, coef_re, coef_im, ssm_b_re, ssm_b_im, ssm_c_re, ssm_c_im, ssm_d,
                           w_glu, b_glu, w_pool, pool_scale, w_br_att, w_br_ssm, w_br_pool, w_out)
    for li in range(depth):
        ya, ys, yp = _mixers(x, li, mod, ng, attn_sinks, wts)
        x = _merge(x, li, mod, ng, ya, ys, yp, wts, fg, final=(li == depth - 1))
    return x
```

```python
import functools
import math

import numpy as np
import jax
import jax.numpy as jnp
from jax import lax
from jax.experimental import pallas as pl
from jax.experimental.pallas import tpu as pltpu

F32 = jnp.float32
BF16 = jnp.bfloat16

D_MODEL = 1024
CHUNK = 64
N_HEADS = 8
N_KV_HEADS = 2
HEAD_DIM = 64
WIN_CHUNKS = 2
NK = (WIN_CHUNKS + 1) * CHUNK
ATT_W = N_HEADS * HEAD_DIM
KV_W = N_KV_HEADS * HEAD_DIM
SSM_W = 512
SSM_GROUP = 16
SSM_GROUPS = 32
SSM_STATE = 64
N_STATE = SSM_GROUPS * SSM_STATE
POOL_W = 512
POOL_WINDOWS = (2, 4, 8, 16)
POOL_GW = 128
POOL_HALO = 16
O_K, O_US, O_UP, O_ZA, O_ZS, O_ZP, O_G = 512, 768, 1280, 1792, 2304, 2816, 3328
CHUNKS_PER_STEP = 1
MERGE_ROWS = 1024
MOD_COLS = 1024
EPS = 1e-6
NEG_INF = -1e30
LOG2E = math.log2(math.e)
LANES = 128
SUBLANES = 8
VMEM_LIMIT = 58 * 1024 * 1024


def _sigmoid(x):
    return 0.5 * jnp.tanh(0.5 * x) + 0.5


def _silu(x):
    return x * _sigmoid(x)


def _gelu_tanh(x):
    return 0.5 * x * (1.0 + jnp.tanh(math.sqrt(2.0 / math.pi) * (x + 0.044715 * (x * x * x))))


def _const_spec(shape):
    nd = len(shape)
    return pl.BlockSpec(shape, lambda *_: (0,) * nd, pipeline_mode=pl.Buffered(1))


def _column_spec(shape, li, c0, width):
    return pl.BlockSpec((pl.Element(1), pl.Element(shape[1]), pl.Element(width)), lambda *_: (li, 0, c0),
                        pipeline_mode=pl.Buffered(1))


def _layer_spec(shape, li):
    nd = len(shape)
    return pl.BlockSpec((None,) + tuple(shape[1:]), lambda *_: (li,) + (0,) * (nd - 1),
                        pipeline_mode=pl.Buffered(1))


def _mod_kernel(c_ref, w_ref, b_ref, o_ref):
    ca = _silu(c_ref[...])
    o_ref[0] = jnp.dot(ca.astype(BF16), w_ref[0].astype(BF16), preferred_element_type=F32) + b_ref[0]


def _modulation(c, w_ada, b_ada):
    depth, d, n3 = w_ada.shape
    b = c.shape[0]
    bw = MOD_COLS
    return pl.pallas_call(
        _mod_kernel,
        grid=(depth, n3 // bw),
        in_specs=[pl.BlockSpec((b, d), lambda l, j: (0, 0)),
                  pl.BlockSpec((1, d, bw), lambda l, j: (l, 0, j)),
                  pl.BlockSpec((1, 1, bw), lambda l, j: (l, 0, j))],
        out_specs=pl.BlockSpec((1, b, bw), lambda l, j: (l, 0, j)),
        out_shape=jax.ShapeDtypeStruct((depth, b, n3), F32),
        name="adaln_mod",
    )(c, w_ada, b_ada.reshape(depth, 1, n3))


def _ssm_prep_kernel(are_ref, aim_ref, ldt_ref, lre_ref, lim_ref, cre_ref, cim_ref):
    a_re = are_ref[0]
    a_im = aim_ref[0]
    dt = jnp.exp(ldt_ref[0])
    mag = jnp.exp(a_re * dt)
    ang = a_im * dt
    l_re = mag * jnp.cos(ang)
    l_im = mag * jnp.sin(ang)
    n_re = l_re - 1.0
    den = a_re * a_re + a_im * a_im
    lre_ref[0] = l_re
    lim_ref[0] = l_im
    cre_ref[0] = (n_re * a_re + l_im * a_im) / den
    cim_ref[0] = (l_im * a_re - n_re * a_im) / den


def _ssm_prep(a_re, a_im, log_dt):
    depth, g, p = a_re.shape
    spec = pl.BlockSpec((1, g, p), lambda l: (l, 0, 0))
    out = jax.ShapeDtypeStruct((depth, g, p), F32)
    return pl.pallas_call(
        _ssm_prep_kernel,
        grid=(depth,),
        in_specs=[spec, spec, pl.BlockSpec((1, g, 1), lambda l: (l, 0, 0))],
        out_specs=[spec] * 4,
        out_shape=[out] * 4,
        name="ssm_discretise",
    )(a_re, a_im, log_dt.reshape(depth, g, 1))


def _mixer_kernel(sink_ref, x_ref, mod_ref, ng_ref, wq32, wkv32, wza32, wus32, wzs32, wup32, wzp32,
                  bmat_ref, cre_ref, cim_ref, lam_ref, dskip_ref, wglu32, bglu_ref, wpool32, pscale_ref,
                  ya_ref, ys_ref, yp_ref,
                  hb_ref, att_ref, s_ref, p_ref, kring_ref, vring_ref, us_ref, ut_ref, bu_ref, st_ref, yt_ref,
                  pext_ref, zp_ref, wq_ref, wkv_ref, wza_ref, wus_ref, wzs_ref, wup_ref, wzp_ref, wglu_ref, wpool_ref,
                  *, nb, li):
    step = pl.program_id(0)
    tl = CHUNK
    m = nb * tl
    half = N_STATE // 4
    im0 = N_STATE + LANES
    hpk = N_HEADS // N_KV_HEADS
    rb = 2 * tl
    blk_heads = [(kv * hpk + e, kv * hpk + 2 + e) for kv in range(N_KV_HEADS) for e in range(2)]

    def chunk_index(c):
        return step * CHUNKS_PER_STEP + c

    def chunk_rows(c):
        return slice(c * tl, (c + 1) * tl)

    @pl.when(step == 0)
    def _init():
        kring_ref[...] = jnp.zeros_like(kring_ref)
        vring_ref[...] = jnp.zeros_like(vring_ref)
        st_ref[...] = jnp.zeros_like(st_ref)
        pext_ref[:, 0:POOL_HALO, :] = jnp.zeros((nb, POOL_HALO, POOL_W), F32)
        for src, dst in ((wq32, wq_ref), (wkv32, wkv_ref), (wza32, wza_ref), (wus32, wus_ref), (wzs32, wzs_ref),
                         (wup32, wup_ref), (wzp32, wzp_ref)):
            dst[...] = src[0].astype(BF16)
        wglu_ref[...] = wglu32[...].astype(BF16)
        wpool_ref[...] = wpool32[...].astype(BF16)

    def norm(c):
        x = x_ref[:, chunk_rows(c), :]
        ms = jnp.mean(x * x, axis=-1, keepdims=True)
        gain = ng_ref[...] * (1.0 + mod_ref[1])
        h = x * lax.rsqrt(ms + EPS) * gain + mod_ref[0]
        hb_ref[...] = h.reshape(m, D_MODEL).astype(BF16)

    def att_project():
        hb = hb_ref[...]
        att_ref[:, :ATT_W] = jnp.dot(hb, wq_ref[...], preferred_element_type=F32) * (LOG2E / math.sqrt(HEAD_DIM))
        att_ref[:, ATT_W:ATT_W + 2 * KV_W] = jnp.dot(hb, wkv_ref[...], preferred_element_type=F32)
        att_ref[:, ATT_W + 2 * KV_W:] = jnp.dot(hb, wza_ref[...], preferred_element_type=F32)
        lane = lax.broadcasted_iota(jnp.int32, (m, LANES), 1)
        lo = lane < HEAD_DIM
        for ring, c0 in ((kring_ref, ATT_W), (vring_ref, ATT_W + KV_W)):
            new = att_ref[:, c0:c0 + KV_W]
            swp = pltpu.roll(new, HEAD_DIM, axis=1)
            dup = (jnp.where(lo, new, swp), jnp.where(lo, swp, new))
            for kv in range(N_KV_HEADS):
                for b in range(nb):
                    ring[kv, b, 0:NK - CHUNK, :] = ring[kv, b, CHUNK:NK, :]
                ring[kv, :, NK - CHUNK:NK, :] = dup[kv].reshape(nb, tl, LANES).astype(BF16)

    def att_scores():
        lo = lax.broadcasted_iota(jnp.int32, (tl, LANES), 1) < HEAD_DIM
        zero = jnp.zeros((tl, LANES), F32)
        for b in range(nb):
            for kv in range(N_KV_HEADS):
                pairs = [att_ref[b * tl:(b + 1) * tl, (2 * kv + j) * LANES:(2 * kv + j + 1) * LANES] for j in range(2)]
                qq = jnp.concatenate([jnp.where(lo, q, zero) for q in pairs] + [jnp.where(lo, zero, q) for q in pairs],
                                     axis=0).astype(BF16)
                s_ref[b, 2 * kv * rb:(2 * kv + 2) * rb, :] = lax.dot_general(
                    qq, kring_ref[kv, b], (((1,), (1,)), ((), ())), preferred_element_type=F32)

    def att_softmax(c, blk):
        ha, hb_ = blk_heads[blk]
        row = lax.broadcasted_iota(jnp.int32, (rb, NK), 0)
        kj = lax.broadcasted_iota(jnp.int32, (rb, NK), 1)
        qi = jnp.where(row < tl, row, row - tl)
        dist = jnp.abs(qi + WIN_CHUNKS * CHUNK - kj).astype(F32)
        valid = (chunk_index(c) * CHUNK + kj) >= WIN_CHUNKS * CHUNK
        first = lax.broadcasted_iota(jnp.int32, (rb, 1), 0) < tl
        slope = jnp.where(first, LOG2E * 2.0 ** (-(ha + 1)), LOG2E * 2.0 ** (-(hb_ + 1)))
        nbias = jnp.where(valid, -slope * dist, NEG_INF)
        sink = LOG2E * jnp.where(first, sink_ref[li, ha], sink_ref[li, hb_])
        for b in range(nb):
            s = s_ref[b, blk * rb:(blk + 1) * rb, :] + nbias
            mx = jnp.maximum(jnp.max(s, axis=-1, keepdims=True), sink)
            p = jnp.exp2(s - mx)
            den = jnp.sum(p, axis=-1, keepdims=True) + jnp.exp2(sink - mx)
            p_ref[b, blk * rb:(blk + 1) * rb, :] = (p * (1.0 / den)).astype(BF16)

    def att_values(c):
        lo = lax.broadcasted_iota(jnp.int32, (rb, LANES), 1) < HEAD_DIM
        for b in range(nb):
            for kv in range(N_KV_HEADS):
                t = jnp.dot(p_ref[b, 2 * kv * rb:(2 * kv + 2) * rb, :], vring_ref[kv, b],
                            preferred_element_type=F32)
                o = jnp.where(lo, t[:rb], t[rb:])
                for j in range(2):
                    c0 = (2 * kv + j) * LANES
                    z = att_ref[b * tl:(b + 1) * tl, ATT_W + 2 * KV_W + c0:ATT_W + 2 * KV_W + c0 + LANES]
                    ya_ref[b, chunk_rows(c), c0:c0 + LANES] = (o[j * tl:(j + 1) * tl] * _silu(z)).astype(BF16)

    def ssm_project():
        hb = hb_ref[...]
        us_ref[:, :SSM_W] = jnp.dot(hb, wus_ref[...], preferred_element_type=F32)
        us_ref[:, SSM_W:] = jnp.dot(hb, wzs_ref[...], preferred_element_type=F32)
        u = us_ref[:, :SSM_W].reshape(nb, tl, SSM_W)
        ut_ref[...] = jnp.swapaxes(u, 0, 1).reshape(m, SSM_W).astype(BF16)

    def ssm_expand(r):
        res = jnp.dot(ut_ref[:, r * LANES:(r + 1) * LANES], bmat_ref[r], preferred_element_type=F32)
        bu_ref[:, r * half:(r + 1) * half] = res[:, :half]
        bu_ref[:, im0 + r * half:im0 + (r + 1) * half] = res[:, half:]

    def ssm_scan(r):
        c0 = r * half
        a_re = jnp.broadcast_to(lam_ref[0:1, c0:c0 + half], (nb, half))
        a_im = jnp.broadcast_to(lam_ref[1:2, c0:c0 + half], (nb, half))
        s_re = st_ref[:, c0:c0 + half]
        s_im = st_ref[:, N_STATE + c0:N_STATE + c0 + half]
        for t in range(tl):
            r0 = t * nb
            n_re = a_re * s_re - a_im * s_im + bu_ref[r0:r0 + nb, c0:c0 + half]
            n_im = a_re * s_im + a_im * s_re + bu_ref[r0:r0 + nb, im0 + c0:im0 + c0 + half]
            bu_ref[r0:r0 + nb, c0:c0 + half] = n_re
            bu_ref[r0:r0 + nb, im0 + c0:im0 + c0 + half] = n_im
            s_re, s_im = n_re, n_im
        st_ref[:, c0:c0 + half] = s_re
        st_ref[:, N_STATE + c0:N_STATE + c0 + half] = s_im

    def ssm_contract(r):
        x_re = bu_ref[:, r * half:(r + 1) * half].astype(BF16)
        x_im = bu_ref[:, im0 + r * half:im0 + (r + 1) * half].astype(BF16)
        yt_ref[:, r * LANES:(r + 1) * LANES] = (jnp.dot(x_re, cre_ref[r], preferred_element_type=F32)
                                               + jnp.dot(x_im, cim_ref[r], preferred_element_type=F32))

    def ssm_output(c):
        y = jnp.swapaxes(yt_ref[...].reshape(tl, nb, SSM_W), 0, 1).reshape(m, SSM_W)
        y = _gelu_tanh(y + dskip_ref[...] * us_ref[:, :SSM_W])
        gl = jnp.dot(y.astype(BF16), wglu_ref[...], preferred_element_type=F32) + bglu_ref[...]
        y = y * _sigmoid(gl)
        ys_ref[:, chunk_rows(c), :] = (y * _silu(us_ref[:, SSM_W:])).reshape(nb, tl, SSM_W).astype(BF16)

    def pool_project():
        hb = hb_ref[...]
        zp_ref[...] = jnp.dot(hb, wzp_ref[...], preferred_element_type=F32)
        pext_ref[:, POOL_HALO:POOL_HALO + tl, :] = jnp.dot(hb, wup_ref[...], preferred_element_type=F32
                                                           ).reshape(nb, tl, POOL_W)

    def pool_mix(c):
        tpos = chunk_index(c) * tl + lax.broadcasted_iota(jnp.int32, (1, tl, 1), 1)
        groups = []
        for gi, w in enumerate(POOL_WINDOWS):
            cs = slice(gi * POOL_GW, (gi + 1) * POOL_GW)
            ext = pext_ref[:, :, cs]
            tok = ext[:, POOL_HALO:, :]
            ssum = ext
            k = 1
            while k < w:
                ssum = ssum + pltpu.roll(ssum, k, axis=1)
                k *= 2
            cnt = jnp.minimum(tpos + 1, w).astype(F32)
            pooled = ssum[:, POOL_HALO:, :] / cnt - tok
            groups.append(jnp.dot(pooled.reshape(m, POOL_GW).astype(BF16), wpool_ref[gi],
                                  preferred_element_type=F32))
        y = jnp.concatenate(groups, axis=1) * pscale_ref[...]
        yp_ref[:, chunk_rows(c), :] = (y * _silu(zp_ref[...])).reshape(nb, tl, POOL_W).astype(BF16)
        pext_ref[:, 0:POOL_HALO, :] = pext_ref[:, tl:tl + POOL_HALO, :]

    P = functools.partial
    for c in range(CHUNKS_PER_STEP):
        schedule = (P(norm, c), att_project, pool_project, att_scores, ssm_project,
                    P(att_softmax, c, 0), P(ssm_expand, 0), P(att_softmax, c, 1), P(ssm_expand, 1),
                    P(ssm_scan, 0), P(ssm_expand, 2), P(att_softmax, c, 2), P(ssm_scan, 1), P(ssm_expand, 3),
                    P(att_softmax, c, 3), P(ssm_contract, 0), P(ssm_scan, 2), P(ssm_contract, 1), P(pool_mix, c),
                    P(ssm_scan, 3), P(att_values, c), P(ssm_contract, 2), P(ssm_contract, 3), P(ssm_output, c))
        for phase in schedule:
            phase()


def _mixers(x, li, mod, norm_g, sinks, wts):
    nb, seq, d = x.shape
    tl = CHUNK
    m = nb * tl
    consts = (wts["bmat"], wts["cre"],
              wts["cim"], wts["lam"], wts["dskip"], wts["w_glu"], wts["b_glu"], wts["w_pool"], wts["pscale"])
    w_in = wts["w_in"]
    proj_cols = ((0, ATT_W), (O_K, 2 * KV_W), (O_ZA, ATT_W), (O_US, SSM_W), (O_ZS, SSM_W), (O_UP, POOL_W),
                 (O_ZP, POOL_W))
    step_rows = CHUNKS_PER_STEP * tl
    out_spec = pl.BlockSpec((nb, step_rows, SSM_W), lambda i: (0, i, 0))
    out_shape = jax.ShapeDtypeStruct((nb, seq, SSM_W), BF16)
    return pl.pallas_call(
        functools.partial(_mixer_kernel, nb=nb, li=li),
        grid=(seq // step_rows,),
        in_specs=[pl.BlockSpec(memory_space=pltpu.SMEM),
                  pl.BlockSpec((nb, step_rows, d), lambda i: (0, i, 0)),
                  _layer_spec(mod.shape, li),
                  _layer_spec(norm_g.shape, li)]
                 + [_column_spec(w_in.shape, li, c0, width) for c0, width in proj_cols]
                 + [_layer_spec(c.shape, li) for c in consts],
        out_specs=[out_spec] * 3,
        out_shape=[out_shape] * 3,
        scratch_shapes=[pltpu.VMEM((m, d), BF16),
                        pltpu.VMEM((m, 2 * ATT_W + 2 * KV_W), F32),
                        pltpu.VMEM((nb, N_HEADS * tl, NK), F32),
                        pltpu.VMEM((nb, N_HEADS * tl, NK), BF16),
                        pltpu.VMEM((N_KV_HEADS, nb, NK, LANES), BF16),
                        pltpu.VMEM((N_KV_HEADS, nb, NK, LANES), BF16),
                        pltpu.VMEM((m, 2 * SSM_W), F32),
                        pltpu.VMEM((m, SSM_W), BF16),
                        pltpu.VMEM((m, 2 * N_STATE + LANES), F32),
                        pltpu.VMEM((nb, 2 * N_STATE), F32),
                        pltpu.VMEM((m, SSM_W), F32),
                        pltpu.VMEM((nb, POOL_HALO + tl, POOL_W), F32),
                        pltpu.VMEM((m, POOL_W), F32)]
                       + [pltpu.VMEM((d, width), BF16) for _, width in proj_cols]
                       + [pltpu.VMEM(wts[k].shape[1:], BF16) for k in ("w_glu", "w_pool")],
        compiler_params=pltpu.CompilerParams(dimension_semantics=("arbitrary",), vmem_limit_bytes=VMEM_LIMIT),
        name="mixers",
    )(sinks, x, mod, norm_g, *([w_in] * len(proj_cols)), *consts)


def _merge_kernel(x_ref, mod_ref, ng_ref, ya_ref, ys_ref, yp_ref, wg32, wba_ref, wbs_ref, wbp_ref, wout_ref,
                  fg_ref, o_ref, wg_ref, *, final):
    @pl.when((pl.program_id(0) == 0) & (pl.program_id(1) == 0))
    def _cast_weights():
        wg_ref[...] = wg32[0].astype(BF16)

    x = x_ref[0]
    ms = jnp.mean(x * x, axis=-1, keepdims=True)
    gain = ng_ref[...] * (1.0 + mod_ref[1, 0])
    hb = (x * lax.rsqrt(ms + EPS) * gain + mod_ref[0, 0]).astype(BF16)
    merged = None
    for j, (y_ref, wb_ref) in enumerate(((ya_ref, wba_ref), (ys_ref, wbs_ref), (yp_ref, wbp_ref))):
        g = jnp.dot(hb, wg_ref[:, j * D_MODEL:(j + 1) * D_MODEL], preferred_element_type=F32)
        p = jnp.dot(y_ref[0], wb_ref[...], preferred_element_type=F32)
        term = _sigmoid(g) * p
        merged = term if merged is None else merged + term
    out = jnp.dot(merged.astype(BF16), wout_ref[...], preferred_element_type=F32)
    xn = x + mod_ref[2, 0] * out
    if final:
        ms = jnp.mean(xn * xn, axis=-1, keepdims=True)
        xn = xn * lax.rsqrt(ms + EPS) * fg_ref[...]
    o_ref[0] = xn


def _merge(x, li, mod, norm_g, ya, ys, yp, wts, final_g, final):
    nb, seq, d = x.shape
    tm = MERGE_ROWS
    row = lambda w: pl.BlockSpec((1, tm, w), lambda b, i: (b, i, 0))
    return pl.pallas_call(
        functools.partial(_merge_kernel, final=final),
        grid=(nb, seq // tm),
        in_specs=[row(d),
                  pl.BlockSpec((None, 3, 1, 1, d), lambda b, i: (li, 0, b, 0, 0)),
                  _layer_spec(norm_g.shape, li),
                  row(ATT_W), row(SSM_W), row(POOL_W),
                  _column_spec(wts["w_in"].shape, li, O_G, 3 * d), _layer_spec(wts["w_br_att"].shape, li),
                  _layer_spec(wts["w_br_ssm"].shape, li), _layer_spec(wts["w_br_pool"].shape, li),
                  _layer_spec(wts["w_out"].shape, li), _const_spec(final_g.shape)],
        out_specs=row(d),
        out_shape=jax.ShapeDtypeStruct(x.shape, x.dtype),
        scratch_shapes=[pltpu.VMEM((d, 3 * d), BF16)],
        compiler_params=pltpu.CompilerParams(dimension_semantics=("arbitrary", "arbitrary"),
                                             vmem_limit_bytes=VMEM_LIMIT),
        name="merge",
    )(x, mod, norm_g, ya, ys, yp, wts["w_in"], wts["w_br_att"], wts["w_br_ssm"], wts["w_br_pool"], wts["w_out"],
      final_g)


def _block_diag(blocks):
    *lead, n, r, c = blocks.shape
    mask = jnp.asarray(np.kron(np.eye(n, dtype=np.float32), np.ones((r, c), np.float32)))
    rows = blocks.reshape(*lead, n * r, c)
    return jnp.tile(rows, (1,) * len(lead) + (1, n)) * mask


def _stacked_weights(w_in, lam_re, lam_im, coef_re, coef_im, ssm_b_re, ssm_b_im, ssm_c_re, ssm_c_im, ssm_d,
                     w_glu, b_glu, w_pool, pool_scale, w_br_att, w_br_ssm, w_br_pool, w_out):
    depth = w_in.shape[0]
    wts = {"w_in": w_in}
    cr = coef_re[..., None]
    ci = coef_im[..., None]
    bb_re = cr * ssm_b_re - ci * ssm_b_im
    bb_im = cr * ssm_b_im + ci * ssm_b_re
    gpb = LANES // SSM_GROUP
    nblk = SSM_GROUPS // gpb
    bre = jnp.swapaxes(bb_re, 2, 3).reshape(depth, nblk, gpb, SSM_GROUP, SSM_STATE)
    bim = jnp.swapaxes(bb_im, 2, 3).reshape(depth, nblk, gpb, SSM_GROUP, SSM_STATE)
    wts["bmat"] = jnp.concatenate([_block_diag(bre), _block_diag(bim)], axis=-1).astype(BF16)
    cre = jnp.swapaxes(ssm_c_re, 2, 3).reshape(depth, nblk, gpb, SSM_STATE, SSM_GROUP)
    cim = jnp.swapaxes(-ssm_c_im, 2, 3).reshape(depth, nblk, gpb, SSM_STATE, SSM_GROUP)
    wts["cre"] = _block_diag(cre).astype(BF16)
    wts["cim"] = _block_diag(cim).astype(BF16)
    wts["lam"] = jnp.stack([lam_re.reshape(depth, N_STATE), lam_im.reshape(depth, N_STATE)], axis=1)
    wts["dskip"] = ssm_d.reshape(depth, 1, SSM_W)
    wts["b_glu"] = b_glu.reshape(depth, 1, SSM_W)
    wts["pscale"] = pool_scale.reshape(depth, 1, POOL_W)
    wts.update(w_glu=w_glu, w_pool=w_pool)
    for name, w in (("w_br_att", w_br_att), ("w_br_ssm", w_br_ssm), ("w_br_pool", w_br_pool), ("w_out", w_out)):
        wts[name] = w.astype(BF16)
    return wts


def kernel(x, c, norm_g, w_ada, b_ada, w_in, attn_sinks, ssm_a_re, ssm_a_im, ssm_log_dt, ssm_b_re, ssm_b_im,
           ssm_c_re, ssm_c_im, ssm_d, w_glu, b_glu, w_pool, pool_scale, w_br_att, w_br_ssm, w_br_pool, w_out,
           final_g):
    nb, seq, d = x.shape
    depth = w_in.shape[0]
    assert d == D_MODEL and nb == SUBLANES and seq % MERGE_ROWS == 0
    mod_all = _modulation(c, w_ada, b_ada)
    lam_re, lam_im, coef_re, coef_im = _ssm_prep(ssm_a_re, ssm_a_im, ssm_log_dt)
    fg = final_g.reshape(1, d)
    mod = jnp.transpose(mod_all.reshape(depth, nb, 3, d), (0, 2, 1, 3)).reshape(depth, 3, nb, 1, d)
    ng = norm_g.reshape(depth, 1, d)
    wts = _stacked_weights(w_in, lam_re, lam_im, coef_re, coef_im, ssm_b_re, ssm_b_im, ssm_c_re, ssm_c_im, ssm_d,
                           w_glu, b_glu, w_pool, pool_scale, w_br_att, w_br_ssm, w_br_pool, w_out)
    for li in range(depth):
        ya, ys, yp = _mixers(x, li, mod, ng, attn_sinks, wts)
        x = _merge(x, li, mod, ng, ya, ys, yp, wts, fg, final=(li == depth - 1))
    return x
```

```python
import functools
import math

import numpy as np
import jax
import jax.numpy as jnp
from jax import lax
from jax.experimental import pallas as pl
from jax.experimental.pallas import tpu as pltpu

F32 = jnp.float32
BF16 = jnp.bfloat16

D_MODEL = 1024
CHUNK = 64
N_HEADS = 8
N_KV_HEADS = 2
HEAD_DIM = 64
WIN_CHUNKS = 2
NK = (WIN_CHUNKS + 1) * CHUNK
ATT_W = N_HEADS * HEAD_DIM
KV_W = N_KV_HEADS * HEAD_DIM
SSM_W = 512
SSM_GROUP = 16
SSM_GROUPS = 32
SSM_STATE = 64
N_STATE = SSM_GROUPS * SSM_STATE
POOL_W = 512
POOL_WINDOWS = (2, 4, 8, 16)
POOL_GW = 128
POOL_HALO = 16
O_K, O_US, O_UP, O_ZA, O_ZS, O_ZP, O_G = 512, 768, 1280, 1792, 2304, 2816, 3328
CHUNKS_PER_STEP = 1
MERGE_ROWS = 1024
MOD_COLS = 1024
EPS = 1e-6
NEG_INF = -1e30
LOG2E = math.log2(math.e)
LANES = 128
SUBLANES = 8
VMEM_LIMIT = 58 * 1024 * 1024


def _sigmoid(x):
    return 0.5 * jnp.tanh(0.5 * x) + 0.5


def _silu(x):
    return x * _sigmoid(x)


def _gelu_tanh(x):
    return 0.5 * x * (1.0 + jnp.tanh(math.sqrt(2.0 / math.pi) * (x + 0.044715 * (x * x * x))))


def _const_spec(shape):
    nd = len(shape)
    return pl.BlockSpec(shape, lambda *_: (0,) * nd, pipeline_mode=pl.Buffered(1))


def _column_spec(shape, li, c0, width):
    return pl.BlockSpec((pl.Element(1), pl.Element(shape[1]), pl.Element(width)), lambda *_: (li, 0, c0),
                        pipeline_mode=pl.Buffered(1))


def _layer_spec(shape, li):
    nd = len(shape)
    return pl.BlockSpec((None,) + tuple(shape[1:]), lambda *_: (li,) + (0,) * (nd - 1),
                        pipeline_mode=pl.Buffered(1))


def _mod_kernel(c_ref, w_ref, b_ref, o_ref):
    ca = _silu(c_ref[...])
    o_ref[0] = jnp.dot(ca.astype(BF16), w_ref[0].astype(BF16), preferred_element_type=F32) + b_ref[0]


def _modulation(c, w_ada, b_ada):
    depth, d, n3 = w_ada.shape
    b = c.shape[0]
    bw = MOD_COLS
    return pl.pallas_call(
        _mod_kernel,
        grid=(depth, n3 // bw),
        in_specs=[pl.BlockSpec((b, d), lambda l, j: (0, 0)),
                  pl.BlockSpec((1, d, bw), lambda l, j: (l, 0, j)),
                  pl.BlockSpec((1, 1, bw), lambda l, j: (l, 0, j))],
        out_specs=pl.BlockSpec((1, b, bw), lambda l, j: (l, 0, j)),
        out_shape=jax.ShapeDtypeStruct((depth, b, n3), F32),
        name="adaln_mod",
    )(c, w_ada, b_ada.reshape(depth, 1, n3))


def _ssm_prep_kernel(are_ref, aim_ref, ldt_ref, lre_ref, lim_ref, cre_ref, cim_ref):
    a_re = are_ref[0]
    a_im = aim_ref[0]
    dt = jnp.exp(ldt_ref[0])
    mag = jnp.exp(a_re * dt)
    ang = a_im * dt
    l_re = mag * jnp.cos(ang)
    l_im = mag * jnp.sin(ang)
    n_re = l_re - 1.0
    den = a_re * a_re + a_im * a_im
    lre_ref[0] = l_re
    lim_ref[0] = l_im
    cre_ref[0] = (n_re * a_re + l_im * a_im) / den
    cim_ref[0] = (l_im * a_re - n_re * a_im) / den


def _ssm_prep(a_re, a_im, log_dt):
    depth, g, p = a_re.shape
    spec = pl.BlockSpec((1, g, p), lambda l: (l, 0, 0))
    out = jax.ShapeDtypeStruct((depth, g, p), F32)
    return pl.pallas_call(
        _ssm_prep_kernel,
        grid=(depth,),
        in_specs=[spec, spec, pl.BlockSpec((1, g, 1), lambda l: (l, 0, 0))],
        out_specs=[spec] * 4,
        out_shape=[out] * 4,
        name="ssm_discretise",
    )(a_re, a_im, log_dt.reshape(depth, g, 1))


def _mixer_kernel(sink_ref, x_ref, mod_ref, ng_ref, wq32, wkv32, wza32, wus32, wzs32, wup32, wzp32,
                  bmat_ref, cre_ref, cim_ref, lam_ref, dskip_ref, wglu32, bglu_ref, wpool32, pscale_ref,
                  ya_ref, ys_ref, yp_ref,
                  hb_ref, att_ref, s_ref, p_ref, kring_ref, vring_ref, us_ref, ut_ref, bu_ref, xs_ref, st_ref, yt_ref,
                  pext_ref, zp_ref, wq_ref, wkv_ref, wza_ref, wus_ref, wzs_ref, wup_ref, wzp_ref, wglu_ref, wpool_ref,
                  *, nb, li):
    step = pl.program_id(0)
    tl = CHUNK
    m = nb * tl
    half = N_STATE // 4
    im0 = N_STATE + LANES
    hpk = N_HEADS // N_KV_HEADS
    rb = 2 * tl
    blk_heads = [(kv * hpk + e, kv * hpk + 2 + e) for kv in range(N_KV_HEADS) for e in range(2)]

    def chunk_index(c):
        return step * CHUNKS_PER_STEP + c

    def chunk_rows(c):
        return slice(c * tl, (c + 1) * tl)

    @pl.when(step == 0)
    def _init():
        kring_ref[...] = jnp.zeros_like(kring_ref)
        vring_ref[...] = jnp.zeros_like(vring_ref)
        st_ref[...] = jnp.zeros_like(st_ref)
        pext_ref[:, 0:POOL_HALO, :] = jnp.zeros((nb, POOL_HALO, POOL_W), F32)
        for src, dst in ((wq32, wq_ref), (wkv32, wkv_ref), (wza32, wza_ref), (wus32, wus_ref), (wzs32, wzs_ref),
                         (wup32, wup_ref), (wzp32, wzp_ref)):
            dst[...] = src[0].astype(BF16)
        wglu_ref[...] = wglu32[...].astype(BF16)
        wpool_ref[...] = wpool32[...].astype(BF16)

    def norm(c):
        x = x_ref[:, chunk_rows(c), :]
        ms = jnp.mean(x * x, axis=-1, keepdims=True)
        gain = ng_ref[...] * (1.0 + mod_ref[1])
        h = x * lax.rsqrt(ms + EPS) * gain + mod_ref[0]
        hb_ref[...] = h.reshape(m, D_MODEL).astype(BF16)

    def att_project():
        hb = hb_ref[...]
        att_ref[:, :ATT_W] = jnp.dot(hb, wq_ref[...], preferred_element_type=F32) * (LOG2E / math.sqrt(HEAD_DIM))
        att_ref[:, ATT_W:ATT_W + 2 * KV_W] = jnp.dot(hb, wkv_ref[...], preferred_element_type=F32)
        att_ref[:, ATT_W + 2 * KV_W:] = jnp.dot(hb, wza_ref[...], preferred_element_type=F32)
        lane = lax.broadcasted_iota(jnp.int32, (m, LANES), 1)
        lo = lane < HEAD_DIM
        for ring, c0 in ((kring_ref, ATT_W), (vring_ref, ATT_W + KV_W)):
            new = att_ref[:, c0:c0 + KV_W]
            swp = pltpu.roll(new, HEAD_DIM, axis=1)
            dup = (jnp.where(lo, new, swp), jnp.where(lo, swp, new))
            for kv in range(N_KV_HEADS):
                for b in range(nb):
                    ring[kv, b, 0:NK - CHUNK, :] = ring[kv, b, CHUNK:NK, :]
                ring[kv, :, NK - CHUNK:NK, :] = dup[kv].reshape(nb, tl, LANES).astype(BF16)

    def att_scores():
        lo = lax.broadcasted_iota(jnp.int32, (tl, LANES), 1) < HEAD_DIM
        zero = jnp.zeros((tl, LANES), F32)
        for b in range(nb):
            for kv in range(N_KV_HEADS):
                pairs = [att_ref[b * tl:(b + 1) * tl, (2 * kv + j) * LANES:(2 * kv + j + 1) * LANES] for j in range(2)]
                qq = jnp.concatenate([jnp.where(lo, q, zero) for q in pairs] + [jnp.where(lo, zero, q) for q in pairs],
                                     axis=0).astype(BF16)
                s_ref[b, 2 * kv * rb:(2 * kv + 2) * rb, :] = lax.dot_general(
                    qq, kring_ref[kv, b], (((1,), (1,)), ((), ())), preferred_element_type=F32)

    def att_softmax(c, blk):
        ha, hb_ = blk_heads[blk]
        row = lax.broadcasted_iota(jnp.int32, (rb, NK), 0)
        kj = lax.broadcasted_iota(jnp.int32, (rb, NK), 1)
        qi = jnp.where(row < tl, row, row - tl)
        dist = jnp.abs(qi + WIN_CHUNKS * CHUNK - kj).astype(F32)
        valid = (chunk_index(c) * CHUNK + kj) >= WIN_CHUNKS * CHUNK
        first = lax.broadcasted_iota(jnp.int32, (rb, 1), 0) < tl
        slope = jnp.where(first, LOG2E * 2.0 ** (-(ha + 1)), LOG2E * 2.0 ** (-(hb_ + 1)))
        nbias = jnp.where(valid, -slope * dist, NEG_INF)
        sink = LOG2E * jnp.where(first, sink_ref[li, ha], sink_ref[li, hb_])
        for b in range(nb):
            s = s_ref[b, blk * rb:(blk + 1) * rb, :] + nbias
            mx = jnp.maximum(jnp.max(s, axis=-1, keepdims=True), sink)
            p = jnp.exp2(s - mx)
            den = jnp.sum(p, axis=-1, keepdims=True) + jnp.exp2(sink - mx)
            p_ref[b, blk * rb:(blk + 1) * rb, :] = (p * (1.0 / den)).astype(BF16)

    def att_values(c):
        lo = lax.broadcasted_iota(jnp.int32, (rb, LANES), 1) < HEAD_DIM
        for b in range(nb):
            for kv in range(N_KV_HEADS):
                t = jnp.dot(p_ref[b, 2 * kv * rb:(2 * kv + 2) * rb, :], vring_ref[kv, b],
                            preferred_element_type=F32)
                o = jnp.where(lo, t[:rb], t[rb:])
                for j in range(2):
                    c0 = (2 * kv + j) * LANES
                    z = att_ref[b * tl:(b + 1) * tl, ATT_W + 2 * KV_W + c0:ATT_W + 2 * KV_W + c0 + LANES]
                    ya_ref[b, chunk_rows(c), c0:c0 + LANES] = (o[j * tl:(j + 1) * tl] * _silu(z)).astype(BF16)

    def ssm_project():
        hb = hb_ref[...]
        us_ref[:, :SSM_W] = jnp.dot(hb, wus_ref[...], preferred_element_type=F32)
        us_ref[:, SSM_W:] = jnp.dot(hb, wzs_ref[...], preferred_element_type=F32)
        u = us_ref[:, :SSM_W].reshape(nb, tl, SSM_W)
        ut_ref[...] = jnp.swapaxes(u, 0, 1).reshape(m, SSM_W).astype(BF16)

    def ssm_expand(r):
        res = jnp.dot(ut_ref[:, r * LANES:(r + 1) * LANES], bmat_ref[r], preferred_element_type=F32)
        bu_ref[:, r * half:(r + 1) * half] = res[:, :half]
        bu_ref[:, im0 + r * half:im0 + (r + 1) * half] = res[:, half:]

    def ssm_scan(r):
        c0 = r * half
        a_re = jnp.broadcast_to(lam_ref[0:1, c0:c0 + half], (nb, half))
        a_im = jnp.broadcast_to(lam_ref[1:2, c0:c0 + half], (nb, half))
        s_re = st_ref[:, c0:c0 + half]
        s_im = st_ref[:, N_STATE + c0:N_STATE + c0 + half]
        for t in range(0, tl, 2):
            r0 = t * nb
            pair_re, pair_im = [], []
            for dt in range(2):
                q0 = r0 + dt * nb
                n_re = a_re * s_re - a_im * s_im + bu_ref[q0:q0 + nb, c0:c0 + half]
                n_im = a_re * s_im + a_im * s_re + bu_ref[q0:q0 + nb, im0 + c0:im0 + c0 + half]
                pair_re.append(n_re)
                pair_im.append(n_im)
                s_re, s_im = n_re, n_im
            xs_ref[r0:r0 + 2 * nb, c0:c0 + half] = jnp.concatenate(pair_re, axis=0).astype(BF16)
            xs_ref[r0:r0 + 2 * nb, N_STATE + c0:N_STATE + c0 + half] = jnp.concatenate(pair_im, axis=0).astype(BF16)
        st_ref[:, c0:c0 + half] = s_re
        st_ref[:, N_STATE + c0:N_STATE + c0 + half] = s_im

    def ssm_contract(r):
        x_re = xs_ref[:, r * half:(r + 1) * half]
        x_im = xs_ref[:, N_STATE + r * half:N_STATE + (r + 1) * half]
        yt_ref[:, r * LANES:(r + 1) * LANES] = (jnp.dot(x_re, cre_ref[r], preferred_element_type=F32)
                                               + jnp.dot(x_im, cim_ref[r], preferred_element_type=F32))

    def ssm_output(c):
        y = jnp.swapaxes(yt_ref[...].reshape(tl, nb, SSM_W), 0, 1).reshape(m, SSM_W)
        y = _gelu_tanh(y + dskip_ref[...] * us_ref[:, :SSM_W])
        gl = jnp.dot(y.astype(BF16), wglu_ref[...], preferred_element_type=F32) + bglu_ref[...]
        y = y * _sigmoid(gl)
        ys_ref[:, chunk_rows(c), :] = (y * _silu(us_ref[:, SSM_W:])).reshape(nb, tl, SSM_W).astype(BF16)

    def pool_project():
        hb = hb_ref[...]
        zp_ref[...] = jnp.dot(hb, wzp_ref[...], preferred_element_type=F32)
        pext_ref[:, POOL_HALO:POOL_HALO + tl, :] = jnp.dot(hb, wup_ref[...], preferred_element_type=F32
                                                           ).reshape(nb, tl, POOL_W)

    def pool_mix(c):
        tpos = chunk_index(c) * tl + lax.broadcasted_iota(jnp.int32, (1, tl, 1), 1)
        groups = []
        for gi, w in enumerate(POOL_WINDOWS):
            cs = slice(gi * POOL_GW, (gi + 1) * POOL_GW)
            ext = pext_ref[:, :, cs]
            tok = ext[:, POOL_HALO:, :]
            ssum = ext
            k = 1
            while k < w:
                ssum = ssum + pltpu.roll(ssum, k, axis=1)
                k *= 2
            cnt = jnp.minimum(tpos + 1, w).astype(F32)
            pooled = ssum[:, POOL_HALO:, :] / cnt - tok
            groups.append(jnp.dot(pooled.reshape(m, POOL_GW).astype(BF16), wpool_ref[gi],
                                  preferred_element_type=F32))
        y = jnp.concatenate(groups, axis=1) * pscale_ref[...]
        yp_ref[:, chunk_rows(c), :] = (y * _silu(zp_ref[...])).reshape(nb, tl, POOL_W).astype(BF16)
        pext_ref[:, 0:POOL_HALO, :] = pext_ref[:, tl:tl + POOL_HALO, :]

    P = functools.partial
    for c in range(CHUNKS_PER_STEP):
        schedule = (P(norm, c), att_project, pool_project, att_scores, ssm_project,
                    P(att_softmax, c, 0), P(ssm_expand, 0), P(att_softmax, c, 1), P(ssm_expand, 1),
                    P(ssm_scan, 0), P(ssm_expand, 2), P(att_softmax, c, 2), P(ssm_scan, 1), P(ssm_expand, 3),
                    P(att_softmax, c, 3), P(ssm_contract, 0), P(ssm_scan, 2), P(ssm_contract, 1), P(pool_mix, c),
                    P(ssm_scan, 3), P(att_values, c), P(ssm_contract, 2), P(ssm_contract, 3), P(ssm_output, c))
        for phase in schedule:
            phase()


def _mixers(x, li, mod, norm_g, sinks, wts):
    nb, seq, d = x.shape
    tl = CHUNK
    m = nb * tl
    consts = (wts["bmat"], wts["cre"],
              wts["cim"], wts["lam"], wts["dskip"], wts["w_glu"], wts["b_glu"], wts["w_pool"], wts["pscale"])
    w_in = wts["w_in"]
    proj_cols = ((0, ATT_W), (O_K, 2 * KV_W), (O_ZA, ATT_W), (O_US, SSM_W), (O_ZS, SSM_W), (O_UP, POOL_W),
                 (O_ZP, POOL_W))
    step_rows = CHUNKS_PER_STEP * tl
    out_spec = pl.BlockSpec((nb, step_rows, SSM_W), lambda i: (0, i, 0))
    out_shape = jax.ShapeDtypeStruct((nb, seq, SSM_W), BF16)
    return pl.pallas_call(
        functools.partial(_mixer_kernel, nb=nb, li=li),
        grid=(seq // step_rows,),
        in_specs=[pl.BlockSpec(memory_space=pltpu.SMEM),
                  pl.BlockSpec((nb, step_rows, d), lambda i: (0, i, 0)),
                  _layer_spec(mod.shape, li),
                  _layer_spec(norm_g.shape, li)]
                 + [_column_spec(w_in.shape, li, c0, width) for c0, width in proj_cols]
                 + [_layer_spec(c.shape, li) for c in consts],
        out_specs=[out_spec] * 3,
        out_shape=[out_shape] * 3,
        scratch_shapes=[pltpu.VMEM((m, d), BF16),
                        pltpu.VMEM((m, 2 * ATT_W + 2 * KV_W), F32),
                        pltpu.VMEM((nb, N_HEADS * tl, NK), F32),
                        pltpu.VMEM((nb, N_HEADS * tl, NK), BF16),
                        pltpu.VMEM((N_KV_HEADS, nb, NK, LANES), BF16),
                        pltpu.VMEM((N_KV_HEADS, nb, NK, LANES), BF16),
                        pltpu.VMEM((m, 2 * SSM_W), F32),
                        pltpu.VMEM((m, SSM_W), BF16),
                        pltpu.VMEM((m, 2 * N_STATE + LANES), F32),
                        pltpu.VMEM((m, 2 * N_STATE), BF16),
                        pltpu.VMEM((nb, 2 * N_STATE), F32),
                        pltpu.VMEM((m, SSM_W), F32),
                        pltpu.VMEM((nb, POOL_HALO + tl, POOL_W), F32),
                        pltpu.VMEM((m, POOL_W), F32)]
                       + [pltpu.VMEM((d, width), BF16) for _, width in proj_cols]
                       + [pltpu.VMEM(wts[k].shape[1:], BF16) for k in ("w_glu", "w_pool")],
        compiler_params=pltpu.CompilerParams(dimension_semantics=("arbitrary",), vmem_limit_bytes=VMEM_LIMIT),
        name="mixers",
    )(sinks, x, mod, norm_g, *([w_in] * len(proj_cols)), *consts)


def _merge_kernel(x_ref, mod_ref, ng_ref, ya_ref, ys_ref, yp_ref, wg32, wba_ref, wbs_ref, wbp_ref, wout_ref,
                  fg_ref, o_ref, wg_ref, *, final):
    @pl.when((pl.program_id(0) == 0) & (pl.program_id(1) == 0))
    def _cast_weights():
        wg_ref[...] = wg32[0].astype(BF16)

    x = x_ref[0]
    ms = jnp.mean(x * x, axis=-1, keepdims=True)
    gain = ng_ref[...] * (1.0 + mod_ref[1, 0])
    hb = (x * lax.rsqrt(ms + EPS) * gain + mod_ref[0, 0]).astype(BF16)
    merged = None
    for j, (y_ref, wb_ref) in enumerate(((ya_ref, wba_ref), (ys_ref, wbs_ref), (yp_ref, wbp_ref))):
        g = jnp.dot(hb, wg_ref[:, j * D_MODEL:(j + 1) * D_MODEL], preferred_element_type=F32)
        p = jnp.dot(y_ref[0], wb_ref[...], preferred_element_type=F32)
        term = _sigmoid(g) * p
        merged = term if merged is None else merged + term
    out = jnp.dot(merged.astype(BF16), wout_ref[...], preferred_element_type=F32)
    xn = x + mod_ref[2, 0] * out
    if final:
        ms = jnp.mean(xn * xn, axis=-1, keepdims=True)
        xn = xn * lax.rsqrt(ms + EPS) * fg_ref[...]
    o_ref[0] = xn


def _merge(x, li, mod, norm_g, ya, ys, yp, wts, final_g, final):
    nb, seq, d = x.shape
    tm = MERGE_ROWS
    row = lambda w: pl.BlockSpec((1, tm, w), lambda b, i: (b, i, 0))
    return pl.pallas_call(
        functools.partial(_merge_kernel, final=final),
        grid=(nb, seq // tm),
        in_specs=[row(d),
                  pl.BlockSpec((None, 3, 1, 1, d), lambda b, i: (li, 0, b, 0, 0)),
                  _layer_spec(norm_g.shape, li),
                  row(ATT_W), row(SSM_W), row(POOL_W),
                  _column_spec(wts["w_in"].shape, li, O_G, 3 * d), _layer_spec(wts["w_br_att"].shape, li),
                  _layer_spec(wts["w_br_ssm"].shape, li), _layer_spec(wts["w_br_pool"].shape, li),
                  _layer_spec(wts["w_out"].shape, li), _const_spec(final_g.shape)],
        out_specs=row(d),
        out_shape=jax.ShapeDtypeStruct(x.shape, x.dtype),
        scratch_shapes=[pltpu.VMEM((d, 3 * d), BF16)],
        compiler_params=pltpu.CompilerParams(dimension_semantics=("arbitrary", "arbitrary"),
                                             vmem_limit_bytes=VMEM_LIMIT),
        name="merge",
    )(x, mod, norm_g, ya, ys, yp, wts["w_in"], wts["w_br_att"], wts["w_br_ssm"], wts["w_br_pool"], wts["w_out"],
      final_g)


def _block_diag(blocks):
    *lead, n, r, c = blocks.shape
    mask = jnp.asarray(np.kron(np.eye(n, dtype=np.float32), np.ones((r, c), np.float32)))
    rows = blocks.reshape(*lead, n * r, c)
    return jnp.tile(rows, (1,) * len(lead) + (1, n)) * mask


def _stacked_weights(w_in, lam_re, lam_im, coef_re, coef_im, ssm_b_re, ssm_b_im, ssm_c_re, ssm_c_im, ssm_d,
                     w_glu, b_glu, w_pool, pool_scale, w_br_att, w_br_ssm, w_br_pool, w_out):
    depth = w_in.shape[0]
    wts = {"w_in": w_in}
    cr = coef_re[..., None]
    ci = coef_im[..., None]
    bb_re = cr * ssm_b_re - ci * ssm_b_im
    bb_im = cr * ssm_b_im + ci * ssm_b_re
    gpb = LANES // SSM_GROUP
    nblk = SSM_GROUPS // gpb
    bre = jnp.swapaxes(bb_re, 2, 3).reshape(depth, nblk, gpb, SSM_GROUP, SSM_STATE)
    bim = jnp.swapaxes(bb_im, 2, 3).reshape(depth, nblk, gpb, SSM_GROUP, SSM_STATE)
    wts["bmat"] = jnp.concatenate([_block_diag(bre), _block_diag(bim)], axis=-1).astype(BF16)
    cre = jnp.swapaxes(ssm_c_re, 2, 3).reshape(depth, nblk, gpb, SSM_STATE, SSM_GROUP)
    cim = jnp.swapaxes(-ssm_c_im, 2, 3).reshape(depth, nblk, gpb, SSM_STATE, SSM_GROUP)
    wts["cre"] = _block_diag(cre).astype(BF16)
    wts["cim"] = _block_diag(cim).astype(BF16)
    wts["lam"] = jnp.stack([lam_re.reshape(depth, N_STATE), lam_im.reshape(depth, N_STATE)], axis=1)
    wts["dskip"] = ssm_d.reshape(depth, 1, SSM_W)
    wts["b_glu"] = b_glu.reshape(depth, 1, SSM_W)
    wts["pscale"] = pool_scale.reshape(depth, 1, POOL_W)
    wts.update(w_glu=w_glu, w_pool=w_pool)
    for name, w in (("w_br_att", w_br_att), ("w_br_ssm", w_br_ssm), ("w_br_pool", w_br_pool), ("w_out", w_out)):
        wts[name] = w.astype(BF16)
    return wts


def kernel(x, c, norm_g, w_ada, b_ada, w_in, attn_sinks, ssm_a_re, ssm_a_im, ssm_log_dt, ssm_b_re, ssm_b_im,
           ssm_c_re, ssm_c_im, ssm_d, w_glu, b_glu, w_pool, pool_scale, w_br_att, w_br_ssm, w_br_pool, w_out,
           final_g):
    nb, seq, d = x.shape
    depth = w_in.shape[0]
    assert d == D_MODEL and nb == SUBLANES and seq % MERGE_ROWS == 0
    mod_all = _modulation(c, w_ada, b_ada)
    lam_re, lam_im, coef_re, coef_im = _ssm_prep(ssm_a_re, ssm_a_im, ssm_log_dt)
    fg = final_g.reshape(1, d)
    mod = jnp.transpose(mod_all.reshape(depth, nb, 3, d), (0, 2, 1, 3)).reshape(depth, 3, nb, 1, d)
    ng = norm_g.reshape(depth, 1, d)
    wts = _stacked_weights(w_in, lam_re, lam_im, coef_re, coef_im, ssm_b_re, ssm_b_im, ssm_c_re, ssm_c_im, ssm_d,
                           w_glu, b_glu, w_pool, pool_scale, w_br_att, w_br_ssm, w_br_pool, w_out)
    for li in range(depth):
        ya, ys, yp = _mixers(x, li, mod, ng, attn_sinks, wts)
        x = _merge(x, li, mod, ng, ya, ys, yp, wts, fg, final=(li == depth - 1))
    return x
```

```python
import functools
import math

import numpy as np
import jax
import jax.numpy as jnp
from jax import lax
from jax.experimental import pallas as pl
from jax.experimental.pallas import tpu as pltpu

F32 = jnp.float32
BF16 = jnp.bfloat16

D_MODEL = 1024
CHUNK = 64
N_HEADS = 8
N_KV_HEADS = 2
HEAD_DIM = 64
WIN_CHUNKS = 2
NK = (WIN_CHUNKS + 1) * CHUNK
ATT_W = N_HEADS * HEAD_DIM
KV_W = N_KV_HEADS * HEAD_DIM
SSM_W = 512
SSM_GROUP = 16
SSM_GROUPS = 32
SSM_STATE = 64
N_STATE = SSM_GROUPS * SSM_STATE
POOL_W = 512
POOL_WINDOWS = (2, 4, 8, 16)
POOL_GW = 128
POOL_HALO = 16
O_K, O_US, O_UP, O_ZA, O_ZS, O_ZP, O_G = 512, 768, 1280, 1792, 2304, 2816, 3328
CHUNKS_PER_STEP = 1
MERGE_ROWS = 1024
MOD_COLS = 1024
EPS = 1e-6
NEG_INF = -1e30
LOG2E = math.log2(math.e)
LANES = 128
SUBLANES = 8
VMEM_LIMIT = 58 * 1024 * 1024


def _sigmoid(x):
    return 0.5 * jnp.tanh(0.5 * x) + 0.5


def _silu(x):
    return x * _sigmoid(x)


def _gelu_tanh(x):
    return 0.5 * x * (1.0 + jnp.tanh(math.sqrt(2.0 / math.pi) * (x + 0.044715 * (x * x * x))))


def _const_spec(shape):
    nd = len(shape)
    return pl.BlockSpec(shape, lambda *_: (0,) * nd, pipeline_mode=pl.Buffered(1))


def _column_spec(shape, li, c0, width):
    return pl.BlockSpec((pl.Element(1), pl.Element(shape[1]), pl.Element(width)), lambda *_: (li, 0, c0),
                        pipeline_mode=pl.Buffered(1))


def _layer_spec(shape, li):
    nd = len(shape)
    return pl.BlockSpec((None,) + tuple(shape[1:]), lambda *_: (li,) + (0,) * (nd - 1),
                        pipeline_mode=pl.Buffered(1))


def _mod_kernel(c_ref, w_ref, b_ref, o_ref):
    ca = _silu(c_ref[...])
    o_ref[0] = jnp.dot(ca.astype(BF16), w_ref[0].astype(BF16), preferred_element_type=F32) + b_ref[0]


def _modulation(c, w_ada, b_ada):
    depth, d, n3 = w_ada.shape
    b = c.shape[0]
    bw = MOD_COLS
    return pl.pallas_call(
        _mod_kernel,
        grid=(depth, n3 // bw),
        in_specs=[pl.BlockSpec((b, d), lambda l, j: (0, 0)),
                  pl.BlockSpec((1, d, bw), lambda l, j: (l, 0, j)),
                  pl.BlockSpec((1, 1, bw), lambda l, j: (l, 0, j))],
        out_specs=pl.BlockSpec((1, b, bw), lambda l, j: (l, 0, j)),
        out_shape=jax.ShapeDtypeStruct((depth, b, n3), F32),
        name="adaln_mod",
    )(c, w_ada, b_ada.reshape(depth, 1, n3))


def _ssm_prep_kernel(are_ref, aim_ref, ldt_ref, lre_ref, lim_ref, cre_ref, cim_ref):
    a_re = are_ref[0]
    a_im = aim_ref[0]
    dt = jnp.exp(ldt_ref[0])
    mag = jnp.exp(a_re * dt)
    ang = a_im * dt
    l_re = mag * jnp.cos(ang)
    l_im = mag * jnp.sin(ang)
    n_re = l_re - 1.0
    den = a_re * a_re + a_im * a_im
    lre_ref[0] = l_re
    lim_ref[0] = l_im
    cre_ref[0] = (n_re * a_re + l_im * a_im) / den
    cim_ref[0] = (l_im * a_re - n_re * a_im) / den


def _ssm_prep(a_re, a_im, log_dt):
    depth, g, p = a_re.shape
    spec = pl.BlockSpec((1, g, p), lambda l: (l, 0, 0))
    out = jax.ShapeDtypeStruct((depth, g, p), F32)
    return pl.pallas_call(
        _ssm_prep_kernel,
        grid=(depth,),
        in_specs=[spec, spec, pl.BlockSpec((1, g, 1), lambda l: (l, 0, 0))],
        out_specs=[spec] * 4,
        out_shape=[out] * 4,
        name="ssm_discretise",
    )(a_re, a_im, log_dt.reshape(depth, g, 1))


def _mixer_kernel(sink_ref, x_ref, mod_ref, ng_ref, wq32, wkv32, wza32, wus32, wzs32, wup32, wzp32,
                  bmat_ref, cre_ref, cim_ref, lam_ref, dskip_ref, wglu32, bglu_ref, wpool32, pscale_ref,
                  ya_ref, ys_ref, yp_ref,
                  hb_ref, att_ref, s_ref, p_ref, kring_ref, vring_ref, us_ref, ut_ref, bu_ref, st_ref, yt_ref,
                  pext_ref, zp_ref, wq_ref, wkv_ref, wza_ref, wus_ref, wzs_ref, wup_ref, wzp_ref, wglu_ref, wpool_ref,
                  *, nb, li):
    step = pl.program_id(0)
    tl = CHUNK
    m = nb * tl
    half = N_STATE // 4
    im0 = N_STATE + LANES
    hpk = N_HEADS // N_KV_HEADS
    rb = 2 * tl
    blk_heads = [(kv * hpk + e, kv * hpk + 2 + e) for kv in range(N_KV_HEADS) for e in range(2)]

    def chunk_index(c):
        return step * CHUNKS_PER_STEP + c

    def chunk_rows(c):
        return slice(c * tl, (c + 1) * tl)

    @pl.when(step == 0)
    def _init():
        kring_ref[...] = jnp.zeros_like(kring_ref)
        vring_ref[...] = jnp.zeros_like(vring_ref)
        st_ref[...] = jnp.zeros_like(st_ref)
        pext_ref[:, 0:POOL_HALO, :] = jnp.zeros((nb, POOL_HALO, POOL_W), F32)
        for src, dst in ((wq32, wq_ref), (wkv32, wkv_ref), (wza32, wza_ref), (wus32, wus_ref), (wzs32, wzs_ref),
                         (wup32, wup_ref), (wzp32, wzp_ref)):
            dst[...] = src[0].astype(BF16)
        wglu_ref[...] = wglu32[...].astype(BF16)
        wpool_ref[...] = wpool32[...].astype(BF16)

    def norm(c):
        x = x_ref[:, chunk_rows(c), :]
        ms = jnp.mean(x * x, axis=-1, keepdims=True)
        gain = ng_ref[...] * (1.0 + mod_ref[1])
        h = x * lax.rsqrt(ms + EPS) * gain + mod_ref[0]
        hb_ref[...] = h.reshape(m, D_MODEL).astype(BF16)

    def att_project():
        hb = hb_ref[...]
        att_ref[:, :ATT_W] = jnp.dot(hb, wq_ref[...], preferred_element_type=F32) * (LOG2E / math.sqrt(HEAD_DIM))
        att_ref[:, ATT_W:ATT_W + 2 * KV_W] = jnp.dot(hb, wkv_ref[...], preferred_element_type=F32)
        att_ref[:, ATT_W + 2 * KV_W:] = jnp.dot(hb, wza_ref[...], preferred_element_type=F32)
        for ring, c0 in ((kring_ref, ATT_W), (vring_ref, ATT_W + KV_W)):
            for b in range(nb):
                ring[b, 0:NK - CHUNK, :] = ring[b, CHUNK:NK, :]
            ring[:, NK - CHUNK:NK, :] = att_ref[:, c0:c0 + KV_W].reshape(nb, tl, LANES).astype(BF16)

    def att_scores():
        lo = lax.broadcasted_iota(jnp.int32, (tl, LANES), 1) < HEAD_DIM
        zero = jnp.zeros((tl, LANES), F32)
        for b in range(nb):
            pairs = [att_ref[b * tl:(b + 1) * tl, j * LANES:(j + 1) * LANES] for j in range(N_HEADS // 2)]
            swapped = [pltpu.roll(q, HEAD_DIM, axis=1) for q in pairs]
            rows = []
            for ha, hb_ in blk_heads:
                for h in (ha, hb_):
                    kv, e = h // hpk, h % 2
                    src = pairs[h // 2] if e == kv else swapped[h // 2]
                    rows.append(jnp.where(lo, src, zero) if kv == 0 else jnp.where(lo, zero, src))
            qq = jnp.concatenate(rows, axis=0).astype(BF16)
            s_ref[b] = lax.dot_general(qq, kring_ref[b], (((1,), (1,)), ((), ())), preferred_element_type=F32)

    def att_softmax(c, blk):
        ha, hb_ = blk_heads[blk]
        row = lax.broadcasted_iota(jnp.int32, (rb, NK), 0)
        kj = lax.broadcasted_iota(jnp.int32, (rb, NK), 1)
        qi = jnp.where(row < tl, row, row - tl)
        dist = jnp.abs(qi + WIN_CHUNKS * CHUNK - kj).astype(F32)
        valid = (chunk_index(c) * CHUNK + kj) >= WIN_CHUNKS * CHUNK
        first = lax.broadcasted_iota(jnp.int32, (rb, 1), 0) < tl
        slope = jnp.where(first, LOG2E * 2.0 ** (-(ha + 1)), LOG2E * 2.0 ** (-(hb_ + 1)))
        nbias = jnp.where(valid, -slope * dist, NEG_INF)
        sink = LOG2E * jnp.where(first, sink_ref[li, ha], sink_ref[li, hb_])
        for b in range(nb):
            s = s_ref[b, blk * rb:(blk + 1) * rb, :] + nbias
            mx = jnp.maximum(jnp.max(s, axis=-1, keepdims=True), sink)
            p = jnp.exp2(s - mx)
            den = jnp.sum(p, axis=-1, keepdims=True) + jnp.exp2(sink - mx)
            p_ref[b, blk * rb:(blk + 1) * rb, :] = (p * (1.0 / den)).astype(BF16)

    def att_values(c):
        lo = lax.broadcasted_iota(jnp.int32, (rb, LANES), 1) < HEAD_DIM
        for b in range(nb):
            t = jnp.dot(p_ref[b], vring_ref[b], preferred_element_type=F32)
            for kv in range(N_KV_HEADS):
                te = t[2 * kv * rb:(2 * kv + 1) * rb]
                to = t[(2 * kv + 1) * rb:(2 * kv + 2) * rb]
                if kv == 0:
                    o = jnp.where(lo, te, pltpu.roll(to, HEAD_DIM, axis=1))
                else:
                    o = jnp.where(lo, pltpu.roll(te, HEAD_DIM, axis=1), to)
                for j in range(2):
                    c0 = (2 * kv + j) * LANES
                    z = att_ref[b * tl:(b + 1) * tl, ATT_W + 2 * KV_W + c0:ATT_W + 2 * KV_W + c0 + LANES]
                    ya_ref[b, chunk_rows(c), c0:c0 + LANES] = (o[j * tl:(j + 1) * tl] * _silu(z)).astype(BF16)

    def ssm_project():
        hb = hb_ref[...]
        us_ref[:, :SSM_W] = jnp.dot(hb, wus_ref[...], preferred_element_type=F32)
        us_ref[:, SSM_W:] = jnp.dot(hb, wzs_ref[...], preferred_element_type=F32)
        u = us_ref[:, :SSM_W].reshape(nb, tl, SSM_W)
        ut_ref[...] = jnp.swapaxes(u, 0, 1).reshape(m, SSM_W).astype(BF16)

    def ssm_expand(r):
        res = jnp.dot(ut_ref[:, r * LANES:(r + 1) * LANES], bmat_ref[r], preferred_element_type=F32)
        bu_ref[:, r * half:(r + 1) * half] = res[:, :half]
        bu_ref[:, im0 + r * half:im0 + (r + 1) * half] = res[:, half:]

    def ssm_scan(r):
        c0 = r * half
        a_re = jnp.broadcast_to(lam_ref[0:1, c0:c0 + half], (nb, half))
        a_im = jnp.broadcast_to(lam_ref[1:2, c0:c0 + half], (nb, half))
        s_re = st_ref[:, c0:c0 + half]
        s_im = st_ref[:, N_STATE + c0:N_STATE + c0 + half]
        for t in range(tl):
            r0 = t * nb
            n_re = a_re * s_re - a_im * s_im + bu_ref[r0:r0 + nb, c0:c0 + half]
            n_im = a_re * s_im + a_im * s_re + bu_ref[r0:r0 + nb, im0 + c0:im0 + c0 + half]
            bu_ref[r0:r0 + nb, c0:c0 + half] = n_re
            bu_ref[r0:r0 + nb, im0 + c0:im0 + c0 + half] = n_im
            s_re, s_im = n_re, n_im
        st_ref[:, c0:c0 + half] = s_re
        st_ref[:, N_STATE + c0:N_STATE + c0 + half] = s_im

    def ssm_contract(r):
        x_re = bu_ref[:, r * half:(r + 1) * half].astype(BF16)
        x_im = bu_ref[:, im0 + r * half:im0 + (r + 1) * half].astype(BF16)
        yt_ref[:, r * LANES:(r + 1) * LANES] = (jnp.dot(x_re, cre_ref[r], preferred_element_type=F32)
                                               + jnp.dot(x_im, cim_ref[r], preferred_element_type=F32))

    def ssm_output(c):
        y = jnp.swapaxes(yt_ref[...].reshape(tl, nb, SSM_W), 0, 1).reshape(m, SSM_W)
        y = _gelu_tanh(y + dskip_ref[...] * us_ref[:, :SSM_W])
        gl = jnp.dot(y.astype(BF16), wglu_ref[...], preferred_element_type=F32) + bglu_ref[...]
        y = y * _sigmoid(gl)
        ys_ref[:, chunk_rows(c), :] = (y * _silu(us_ref[:, SSM_W:])).reshape(nb, tl, SSM_W).astype(BF16)

    def pool_project():
        hb = hb_ref[...]
        zp_ref[...] = jnp.dot(hb, wzp_ref[...], preferred_element_type=F32)
        pext_ref[:, POOL_HALO:POOL_HALO + tl, :] = jnp.dot(hb, wup_ref[...], preferred_element_type=F32
                                                           ).reshape(nb, tl, POOL_W)

    def pool_mix(c):
        tpos = chunk_index(c) * tl + lax.broadcasted_iota(jnp.int32, (1, tl, 1), 1)
        groups = []
        for gi, w in enumerate(POOL_WINDOWS):
            cs = slice(gi * POOL_GW, (gi + 1) * POOL_GW)
            ext = pext_ref[:, :, cs]
            tok = ext[:, POOL_HALO:, :]
            ssum = ext
            k = 1
            while k < w:
                ssum = ssum + pltpu.roll(ssum, k, axis=1)
                k *= 2
            cnt = jnp.minimum(tpos + 1, w).astype(F32)
            pooled = ssum[:, POOL_HALO:, :] / cnt - tok
            groups.append(jnp.dot(pooled.reshape(m, POOL_GW).astype(BF16), wpool_ref[gi],
                                  preferred_element_type=F32))
        y = jnp.concatenate(groups, axis=1) * pscale_ref[...]
        yp_ref[:, chunk_rows(c), :] = (y * _silu(zp_ref[...])).reshape(nb, tl, POOL_W).astype(BF16)
        pext_ref[:, 0:POOL_HALO, :] = pext_ref[:, tl:tl + POOL_HALO, :]

    P = functools.partial
    for c in range(CHUNKS_PER_STEP):
        schedule = (P(norm, c), att_project, pool_project, att_scores, ssm_project,
                    P(att_softmax, c, 0), P(ssm_expand, 0), P(att_softmax, c, 1), P(ssm_expand, 1),
                    P(ssm_scan, 0), P(ssm_expand, 2), P(att_softmax, c, 2), P(ssm_scan, 1), P(ssm_expand, 3),
                    P(att_softmax, c, 3), P(ssm_contract, 0), P(ssm_scan, 2), P(ssm_contract, 1), P(pool_mix, c),
                    P(ssm_scan, 3), P(att_values, c), P(ssm_contract, 2), P(ssm_contract, 3), P(ssm_output, c))
        for phase in schedule:
            phase()


def _mixers(x, li, mod, norm_g, sinks, wts):
    nb, seq, d = x.shape
    tl = CHUNK
    m = nb * tl
    consts = (wts["bmat"], wts["cre"],
              wts["cim"], wts["lam"], wts["dskip"], wts["w_glu"], wts["b_glu"], wts["w_pool"], wts["pscale"])
    w_in = wts["w_in"]
    proj_cols = ((0, ATT_W), (O_K, 2 * KV_W), (O_ZA, ATT_W), (O_US, SSM_W), (O_ZS, SSM_W), (O_UP, POOL_W),
                 (O_ZP, POOL_W))
    step_rows = CHUNKS_PER_STEP * tl
    out_spec = pl.BlockSpec((nb, step_rows, SSM_W), lambda i: (0, i, 0))
    out_shape = jax.ShapeDtypeStruct((nb, seq, SSM_W), BF16)
    return pl.pallas_call(
        functools.partial(_mixer_kernel, nb=nb, li=li),
        grid=(seq // step_rows,),
        in_specs=[pl.BlockSpec(memory_space=pltpu.SMEM),
                  pl.BlockSpec((nb, step_rows, d), lambda i: (0, i, 0)),
                  _layer_spec(mod.shape, li),
                  _layer_spec(norm_g.shape, li)]
                 + [_column_spec(w_in.shape, li, c0, width) for c0, width in proj_cols]
                 + [_layer_spec(c.shape, li) for c in consts],
        out_specs=[out_spec] * 3,
        out_shape=[out_shape] * 3,
        scratch_shapes=[pltpu.VMEM((m, d), BF16),
                        pltpu.VMEM((m, 2 * ATT_W + 2 * KV_W), F32),
                        pltpu.VMEM((nb, N_HEADS * tl, NK), F32),
                        pltpu.VMEM((nb, N_HEADS * tl, NK), BF16),
                        pltpu.VMEM((nb, NK, LANES), BF16),
                        pltpu.VMEM((nb, NK, LANES), BF16),
                        pltpu.VMEM((m, 2 * SSM_W), F32),
                        pltpu.VMEM((m, SSM_W), BF16),
                        pltpu.VMEM((m, 2 * N_STATE + LANES), F32),
                        pltpu.VMEM((nb, 2 * N_STATE), F32),
                        pltpu.VMEM((m, SSM_W), F32),
                        pltpu.VMEM((nb, POOL_HALO + tl, POOL_W), F32),
                        pltpu.VMEM((m, POOL_W), F32)]
                       + [pltpu.VMEM((d, width), BF16) for _, width in proj_cols]
                       + [pltpu.VMEM(wts[k].shape[1:], BF16) for k in ("w_glu", "w_pool")],
        compiler_params=pltpu.CompilerParams(dimension_semantics=("arbitrary",), vmem_limit_bytes=VMEM_LIMIT),
        name="mixers",
    )(sinks, x, mod, norm_g, *([w_in] * len(proj_cols)), *consts)


def _merge_kernel(x_ref, mod_ref, ng_ref, ya_ref, ys_ref, yp_ref, wg32, wba_ref, wbs_ref, wbp_ref, wout_ref,
                  fg_ref, o_ref, wg_ref, *, final):
    @pl.when((pl.program_id(0) == 0) & (pl.program_id(1) == 0))
    def _cast_weights():
        wg_ref[...] = wg32[0].astype(BF16)

    x = x_ref[0]
    ms = jnp.mean(x * x, axis=-1, keepdims=True)
    gain = ng_ref[...] * (1.0 + mod_ref[1, 0])
    hb = (x * lax.rsqrt(ms + EPS) * gain + mod_ref[0, 0]).astype(BF16)
    merged = None
    for j, (y_ref, wb_ref) in enumerate(((ya_ref, wba_ref), (ys_ref, wbs_ref), (yp_ref, wbp_ref))):
        g = jnp.dot(hb, wg_ref[:, j * D_MODEL:(j + 1) * D_MODEL], preferred_element_type=F32)
        p = jnp.dot(y_ref[0], wb_ref[...], preferred_element_type=F32)
        term = _sigmoid(g) * p
        merged = term if merged is None else merged + term
    out = jnp.dot(merged.astype(BF16), wout_ref[...], preferred_element_type=F32)
    xn = x + mod_ref[2, 0] * out
    if final:
        ms = jnp.mean(xn * xn, axis=-1, keepdims=True)
        xn = xn * lax.rsqrt(ms + EPS) * fg_ref[...]
    o_ref[0] = xn


def _merge(x, li, mod, norm_g, ya, ys, yp, wts, final_g, final):
    nb, seq, d = x.shape
    tm = MERGE_ROWS
    row = lambda w: pl.BlockSpec((1, tm, w), lambda b, i: (b, i, 0))
    return pl.pallas_call(
        functools.partial(_merge_kernel, final=final),
        grid=(nb, seq // tm),
        in_specs=[row(d),
                  pl.BlockSpec((None, 3, 1, 1, d), lambda b, i: (li, 0, b, 0, 0)),
                  _layer_spec(norm_g.shape, li),
                  row(ATT_W), row(SSM_W), row(POOL_W),
                  _column_spec(wts["w_in"].shape, li, O_G, 3 * d), _layer_spec(wts["w_br_att"].shape, li),
                  _layer_spec(wts["w_br_ssm"].shape, li), _layer_spec(wts["w_br_pool"].shape, li),
                  _layer_spec(wts["w_out"].shape, li), _const_spec(final_g.shape)],
        out_specs=row(d),
        out_shape=jax.ShapeDtypeStruct(x.shape, x.dtype),
        scratch_shapes=[pltpu.VMEM((d, 3 * d), BF16)],
        compiler_params=pltpu.CompilerParams(dimension_semantics=("arbitrary", "arbitrary"),
                                             vmem_limit_bytes=VMEM_LIMIT),
        name="merge",
    )(x, mod, norm_g, ya, ys, yp, wts["w_in"], wts["w_br_att"], wts["w_br_ssm"], wts["w_br_pool"], wts["w_out"],
      final_g)


def _block_diag(blocks):
    *lead, n, r, c = blocks.shape
    mask = jnp.asarray(np.kron(np.eye(n, dtype=np.float32), np.ones((r, c), np.float32)))
    rows = blocks.reshape(*lead, n * r, c)
    return jnp.tile(rows, (1,) * len(lead) + (1, n)) * mask


def _stacked_weights(w_in, lam_re, lam_im, coef_re, coef_im, ssm_b_re, ssm_b_im, ssm_c_re, ssm_c_im, ssm_d,
                     w_glu, b_glu, w_pool, pool_scale, w_br_att, w_br_ssm, w_br_pool, w_out):
    depth = w_in.shape[0]
    wts = {"w_in": w_in}
    cr = coef_re[..., None]
    ci = coef_im[..., None]
    bb_re = cr * ssm_b_re - ci * ssm_b_im
    bb_im = cr * ssm_b_im + ci * ssm_b_re
    gpb = LANES // SSM_GROUP
    nblk = SSM_GROUPS // gpb
    bre = jnp.swapaxes(bb_re, 2, 3).reshape(depth, nblk, gpb, SSM_GROUP, SSM_STATE)
    bim = jnp.swapaxes(bb_im, 2, 3).reshape(depth, nblk, gpb, SSM_GROUP, SSM_STATE)
    wts["bmat"] = jnp.concatenate([_block_diag(bre), _block_diag(bim)], axis=-1).astype(BF16)
    cre = jnp.swapaxes(ssm_c_re, 2, 3).reshape(depth, nblk, gpb, SSM_STATE, SSM_GROUP)
    cim = jnp.swapaxes(-ssm_c_im, 2, 3).reshape(depth, nblk, gpb, SSM_STATE, SSM_GROUP)
    wts["cre"] = _block_diag(cre).astype(BF16)
    wts["cim"] = _block_diag(cim).astype(BF16)
    wts["lam"] = jnp.stack([lam_re.reshape(depth, N_STATE), lam_im.reshape(depth, N_STATE)], axis=1)
    wts["dskip"] = ssm_d.reshape(depth, 1, SSM_W)
    wts["b_glu"] = b_glu.reshape(depth, 1, SSM_W)
    wts["pscale"] = pool_scale.reshape(depth, 1, POOL_W)
    wts.update(w_glu=w_glu, w_pool=w_pool)
    for name, w in (("w_br_att", w_br_att), ("w_br_ssm", w_br_ssm), ("w_br_pool", w_br_pool), ("w_out", w_out)):
        wts[name] = w.astype(BF16)
    return wts


def kernel(x, c, norm_g, w_ada, b_ada, w_in, attn_sinks, ssm_a_re, ssm_a_im, ssm_log_dt, ssm_b_re, ssm_b_im,
           ssm_c_re, ssm_c_im, ssm_d, w_glu, b_glu, w_pool, pool_scale, w_br_att, w_br_ssm, w_br_pool, w_out,
           final_g):
    nb, seq, d = x.shape
    depth = w_in.shape[0]
    assert d == D_MODEL and nb == SUBLANES and seq % MERGE_ROWS == 0
    mod_all = _modulation(c, w_ada, b_ada)
    lam_re, lam_im, coef_re, coef_im = _ssm_prep(ssm_a_re, ssm_a_im, ssm_log_dt)
    fg = final_g.reshape(1, d)
    mod = jnp.transpose(mod_all.reshape(depth, nb, 3, d), (0, 2, 1, 3)).reshape(depth, 3, nb, 1, d)
    ng = norm_g.reshape(depth, 1, d)
    wts = _stacked_weights(w_in, lam_re, lam_im, coef_re, coef_im, ssm_b_re, ssm_b_im, ssm_c_re, ssm_c_im, ssm_d,
                           w_glu, b_glu, w_pool, pool_scale, w_br_att, w_br_ssm, w_br_pool, w_out)
    for li in range(depth):
        ya, ys, yp = _mixers(x, li, mod, ng, attn_sinks, wts)
        x = _merge(x, li, mod, ng, ya, ys, yp, wts, fg, final=(li == depth - 1))
    return x
```

```python
import functools
import math

import numpy as np
import jax
import jax.numpy as jnp
from jax import lax
from jax.experimental import pallas as pl
from jax.experimental.pallas import tpu as pltpu

F32 = jnp.float32
BF16 = jnp.bfloat16

D_MODEL = 1024
CHUNK = 64
N_HEADS = 8
N_KV_HEADS = 2
HEAD_DIM = 64
WIN_CHUNKS = 2
NK = (WIN_CHUNKS + 1) * CHUNK
ATT_W = N_HEADS * HEAD_DIM
KV_W = N_KV_HEADS * HEAD_DIM
SSM_W = 512
SSM_GROUP = 16
SSM_GROUPS = 32
SSM_STATE = 64
N_STATE = SSM_GROUPS * SSM_STATE
POOL_W = 512
POOL_WINDOWS = (2, 4, 8, 16)
POOL_GW = 128
POOL_HALO = 16
O_K, O_US, O_UP, O_ZA, O_ZS, O_ZP, O_G = 512, 768, 1280, 1792, 2304, 2816, 3328
CHUNKS_PER_STEP = 1
N_GATED = ATT_W + SSM_W + POOL_W
MERGE_ROWS = 512
MOD_COLS = 1024
EPS = 1e-6
NEG_INF = -1e30
LOG2E = math.log2(math.e)
LANES = 128
SUBLANES = 8
VMEM_LIMIT = 58 * 1024 * 1024


def _sigmoid(x):
    return 0.5 * jnp.tanh(0.5 * x) + 0.5


def _silu(x):
    return x * _sigmoid(x)


def _gelu_tanh(x):
    return 0.5 * x * (1.0 + jnp.tanh(math.sqrt(2.0 / math.pi) * (x + 0.044715 * (x * x * x))))


def _const_spec(shape):
    nd = len(shape)
    return pl.BlockSpec(shape, lambda *_: (0,) * nd, pipeline_mode=pl.Buffered(1))


def _column_spec(shape, li, c0, width):
    return pl.BlockSpec((pl.Element(1), pl.Element(shape[1]), pl.Element(width)), lambda *_: (li, 0, c0),
                        pipeline_mode=pl.Buffered(1))


def _layer_spec(shape, li):
    nd = len(shape)
    return pl.BlockSpec((None,) + tuple(shape[1:]), lambda *_: (li,) + (0,) * (nd - 1),
                        pipeline_mode=pl.Buffered(1))


def _mod_kernel(c_ref, w_ref, b_ref, o_ref):
    ca = _silu(c_ref[...])
    o_ref[0] = jnp.dot(ca.astype(BF16), w_ref[0].astype(BF16), preferred_element_type=F32) + b_ref[0]


def _modulation(c, w_ada, b_ada):
    depth, d, n3 = w_ada.shape
    b = c.shape[0]
    bw = MOD_COLS
    return pl.pallas_call(
        _mod_kernel,
        grid=(depth, n3 // bw),
        in_specs=[pl.BlockSpec((b, d), lambda l, j: (0, 0)),
                  pl.BlockSpec((1, d, bw), lambda l, j: (l, 0, j)),
                  pl.BlockSpec((1, 1, bw), lambda l, j: (l, 0, j))],
        out_specs=pl.BlockSpec((1, b, bw), lambda l, j: (l, 0, j)),
        out_shape=jax.ShapeDtypeStruct((depth, b, n3), F32),
        name="adaln_mod",
    )(c, w_ada, b_ada.reshape(depth, 1, n3))


def _ssm_prep_kernel(are_ref, aim_ref, ldt_ref, lre_ref, lim_ref, cre_ref, cim_ref):
    a_re = are_ref[0]
    a_im = aim_ref[0]
    dt = jnp.exp(ldt_ref[0])
    mag = jnp.exp(a_re * dt)
    ang = a_im * dt
    l_re = mag * jnp.cos(ang)
    l_im = mag * jnp.sin(ang)
    n_re = l_re - 1.0
    den = a_re * a_re + a_im * a_im
    lre_ref[0] = l_re
    lim_ref[0] = l_im
    cre_ref[0] = (n_re * a_re + l_im * a_im) / den
    cim_ref[0] = (l_im * a_re - n_re * a_im) / den


def _ssm_prep(a_re, a_im, log_dt):
    depth, g, p = a_re.shape
    spec = pl.BlockSpec((1, g, p), lambda l: (l, 0, 0))
    out = jax.ShapeDtypeStruct((depth, g, p), F32)
    return pl.pallas_call(
        _ssm_prep_kernel,
        grid=(depth,),
        in_specs=[spec, spec, pl.BlockSpec((1, g, 1), lambda l: (l, 0, 0))],
        out_specs=[spec] * 4,
        out_shape=[out] * 4,
        name="ssm_discretise",
    )(a_re, a_im, log_dt.reshape(depth, g, 1))


def _mixer_kernel(sink_ref, x_ref, mod_ref, ng_ref, wq32, wkv32, wus32, wup32,
                  bmat_ref, cre_ref, cim_ref, lam_ref, dskip_ref, wglu32, bglu_ref, wpool32, pscale_ref,
                  ya_ref, ys_ref, yp_ref,
                  hb_ref, att_ref, s_ref, p_ref, kring_ref, vring_ref, us_ref, ut_ref, bu_ref, st_ref, yt_ref,
                  pext_ref, wq_ref, wkv_ref, wus_ref, wup_ref, wglu_ref, wpool_ref,
                  *, nb, li):
    step = pl.program_id(0)
    tl = CHUNK
    m = nb * tl
    half = N_STATE // 4
    im0 = N_STATE + LANES
    hpk = N_HEADS // N_KV_HEADS
    rb = 2 * tl
    blk_heads = [(kv * hpk + e, kv * hpk + 2 + e) for kv in range(N_KV_HEADS) for e in range(2)]

    def chunk_index(c):
        return step * CHUNKS_PER_STEP + c

    def chunk_rows(c):
        return slice(c * tl, (c + 1) * tl)

    @pl.when(step == 0)
    def _init():
        kring_ref[...] = jnp.zeros_like(kring_ref)
        vring_ref[...] = jnp.zeros_like(vring_ref)
        st_ref[...] = jnp.zeros_like(st_ref)
        pext_ref[:, 0:POOL_HALO, :] = jnp.zeros((nb, POOL_HALO, POOL_W), F32)
        for src, dst in ((wq32, wq_ref), (wkv32, wkv_ref), (wus32, wus_ref), (wup32, wup_ref)):
            dst[...] = src[0].astype(BF16)
        wglu_ref[...] = wglu32[...].astype(BF16)
        wpool_ref[...] = wpool32[...].astype(BF16)

    def norm(c):
        x = x_ref[:, chunk_rows(c), :]
        ms = jnp.mean(x * x, axis=-1, keepdims=True)
        gain = ng_ref[...] * (1.0 + mod_ref[1])
        h = x * lax.rsqrt(ms + EPS) * gain + mod_ref[0]
        hb_ref[...] = h.reshape(m, D_MODEL).astype(BF16)

    def att_project():
        hb = hb_ref[...]
        att_ref[:, :ATT_W] = jnp.dot(hb, wq_ref[...], preferred_element_type=F32) * (LOG2E / math.sqrt(HEAD_DIM))
        att_ref[:, ATT_W:] = jnp.dot(hb, wkv_ref[...], preferred_element_type=F32)
        for ring, c0 in ((kring_ref, ATT_W), (vring_ref, ATT_W + KV_W)):
            for b in range(nb):
                ring[b, 0:NK - CHUNK, :] = ring[b, CHUNK:NK, :]
            ring[:, NK - CHUNK:NK, :] = att_ref[:, c0:c0 + KV_W].reshape(nb, tl, LANES).astype(BF16)

    def att_scores():
        lo = lax.broadcasted_iota(jnp.int32, (tl, LANES), 1) < HEAD_DIM
        zero = jnp.zeros((tl, LANES), F32)
        for b in range(nb):
            pairs = [att_ref[b * tl:(b + 1) * tl, j * LANES:(j + 1) * LANES] for j in range(N_HEADS // 2)]
            swapped = [pltpu.roll(q, HEAD_DIM, axis=1) for q in pairs]
            rows = []
            for ha, hb_ in blk_heads:
                for h in (ha, hb_):
                    kv, e = h // hpk, h % 2
                    src = pairs[h // 2] if e == kv else swapped[h // 2]
                    rows.append(jnp.where(lo, src, zero) if kv == 0 else jnp.where(lo, zero, src))
            qq = jnp.concatenate(rows, axis=0).astype(BF16)
            s_ref[b] = lax.dot_general(qq, kring_ref[b], (((1,), (1,)), ((), ())), preferred_element_type=F32)

    def att_softmax(c, blk):
        ha, hb_ = blk_heads[blk]
        row = lax.broadcasted_iota(jnp.int32, (rb, NK), 0)
        kj = lax.broadcasted_iota(jnp.int32, (rb, NK), 1)
        qi = jnp.where(row < tl, row, row - tl)
        dist = jnp.abs(qi + WIN_CHUNKS * CHUNK - kj).astype(F32)
        valid = (chunk_index(c) * CHUNK + kj) >= WIN_CHUNKS * CHUNK
        first = lax.broadcasted_iota(jnp.int32, (rb, 1), 0) < tl
        slope = jnp.where(first, LOG2E * 2.0 ** (-(ha + 1)), LOG2E * 2.0 ** (-(hb_ + 1)))
        nbias = jnp.where(valid, -slope * dist, NEG_INF)
        sink = LOG2E * jnp.where(first, sink_ref[li, ha], sink_ref[li, hb_])
        for b in range(nb):
            s = s_ref[b, blk * rb:(blk + 1) * rb, :] + nbias
            mx = jnp.maximum(jnp.max(s, axis=-1, keepdims=True), sink)
            p = jnp.exp2(s - mx)
            den = jnp.sum(p, axis=-1, keepdims=True) + jnp.exp2(sink - mx)
            p_ref[b, blk * rb:(blk + 1) * rb, :] = (p * (1.0 / den)).astype(BF16)

    def att_values(c):
        lo = lax.broadcasted_iota(jnp.int32, (rb, LANES), 1) < HEAD_DIM
        for b in range(nb):
            t = jnp.dot(p_ref[b], vring_ref[b], preferred_element_type=F32)
            for kv in range(N_KV_HEADS):
                te = t[2 * kv * rb:(2 * kv + 1) * rb]
                to = t[(2 * kv + 1) * rb:(2 * kv + 2) * rb]
                if kv == 0:
                    o = jnp.where(lo, te, pltpu.roll(to, HEAD_DIM, axis=1))
                else:
                    o = jnp.where(lo, pltpu.roll(te, HEAD_DIM, axis=1), to)
                for j in range(2):
                    c0 = (2 * kv + j) * LANES
                    ya_ref[b, chunk_rows(c), c0:c0 + LANES] = o[j * tl:(j + 1) * tl].astype(BF16)

    def ssm_project():
        us_ref[...] = jnp.dot(hb_ref[...], wus_ref[...], preferred_element_type=F32)
        u = us_ref[...].reshape(nb, tl, SSM_W)
        ut_ref[...] = jnp.swapaxes(u, 0, 1).reshape(m, SSM_W).astype(BF16)

    def ssm_expand(r):
        res = jnp.dot(ut_ref[:, r * LANES:(r + 1) * LANES], bmat_ref[r], preferred_element_type=F32)
        bu_ref[:, r * half:(r + 1) * half] = res[:, :half]
        bu_ref[:, im0 + r * half:im0 + (r + 1) * half] = res[:, half:]

    def ssm_scan(r):
        c0 = r * half
        a_re = jnp.broadcast_to(lam_ref[0:1, c0:c0 + half], (nb, half))
        a_im = jnp.broadcast_to(lam_ref[1:2, c0:c0 + half], (nb, half))
        s_re = st_ref[:, c0:c0 + half]
        s_im = st_ref[:, N_STATE + c0:N_STATE + c0 + half]
        for t in range(tl):
            r0 = t * nb
            n_re = a_re * s_re - a_im * s_im + bu_ref[r0:r0 + nb, c0:c0 + half]
            n_im = a_re * s_im + a_im * s_re + bu_ref[r0:r0 + nb, im0 + c0:im0 + c0 + half]
            bu_ref[r0:r0 + nb, c0:c0 + half] = n_re
            bu_ref[r0:r0 + nb, im0 + c0:im0 + c0 + half] = n_im
            s_re, s_im = n_re, n_im
        st_ref[:, c0:c0 + half] = s_re
        st_ref[:, N_STATE + c0:N_STATE + c0 + half] = s_im

    def ssm_contract(r):
        x_re = bu_ref[:, r * half:(r + 1) * half].astype(BF16)
        x_im = bu_ref[:, im0 + r * half:im0 + (r + 1) * half].astype(BF16)
        yt_ref[:, r * LANES:(r + 1) * LANES] = (jnp.dot(x_re, cre_ref[r], preferred_element_type=F32)
                                               + jnp.dot(x_im, cim_ref[r], preferred_element_type=F32))

    def ssm_output(c):
        y = jnp.swapaxes(yt_ref[...].reshape(tl, nb, SSM_W), 0, 1).reshape(m, SSM_W)
        y = _gelu_tanh(y + dskip_ref[...] * us_ref[...])
        gl = jnp.dot(y.astype(BF16), wglu_ref[...], preferred_element_type=F32) + bglu_ref[...]
        y = y * _sigmoid(gl)
        ys_ref[:, chunk_rows(c), :] = y.reshape(nb, tl, SSM_W).astype(BF16)

    def pool_project():
        pext_ref[:, POOL_HALO:POOL_HALO + tl, :] = jnp.dot(hb_ref[...], wup_ref[...], preferred_element_type=F32
                                                           ).reshape(nb, tl, POOL_W)

    def pool_mix(c):
        tpos = chunk_index(c) * tl + lax.broadcasted_iota(jnp.int32, (1, tl, 1), 1)
        groups = []
        for gi, w in enumerate(POOL_WINDOWS):
            cs = slice(gi * POOL_GW, (gi + 1) * POOL_GW)
            ext = pext_ref[:, :, cs]
            tok = ext[:, POOL_HALO:, :]
            ssum = ext
            k = 1
            while k < w:
                ssum = ssum + pltpu.roll(ssum, k, axis=1)
                k *= 2
            cnt = jnp.minimum(tpos + 1, w).astype(F32)
            pooled = ssum[:, POOL_HALO:, :] / cnt - tok
            groups.append(jnp.dot(pooled.reshape(m, POOL_GW).astype(BF16), wpool_ref[gi],
                                  preferred_element_type=F32))
        y = jnp.concatenate(groups, axis=1) * pscale_ref[...]
        yp_ref[:, chunk_rows(c), :] = y.reshape(nb, tl, POOL_W).astype(BF16)
        pext_ref[:, 0:POOL_HALO, :] = pext_ref[:, tl:tl + POOL_HALO, :]

    P = functools.partial
    for c in range(CHUNKS_PER_STEP):
        schedule = (P(norm, c), att_project, pool_project, att_scores, ssm_project,
                    P(att_softmax, c, 0), P(ssm_expand, 0), P(att_softmax, c, 1), P(ssm_expand, 1),
                    P(ssm_scan, 0), P(ssm_expand, 2), P(att_softmax, c, 2), P(ssm_scan, 1), P(ssm_expand, 3),
                    P(att_softmax, c, 3), P(ssm_contract, 0), P(ssm_scan, 2), P(ssm_contract, 1), P(pool_mix, c),
                    P(ssm_scan, 3), P(att_values, c), P(ssm_contract, 2), P(ssm_contract, 3), P(ssm_output, c))
        for phase in schedule:
            phase()


def _mixers(x, li, mod, norm_g, sinks, wts):
    nb, seq, d = x.shape
    tl = CHUNK
    m = nb * tl
    consts = (wts["bmat"], wts["cre"],
              wts["cim"], wts["lam"], wts["dskip"], wts["w_glu"], wts["b_glu"], wts["w_pool"], wts["pscale"])
    w_in = wts["w_in"]
    proj_cols = ((0, ATT_W), (O_K, 2 * KV_W), (O_US, SSM_W), (O_UP, POOL_W))
    step_rows = CHUNKS_PER_STEP * tl
    out_spec = pl.BlockSpec((nb, step_rows, SSM_W), lambda i: (0, i, 0))
    out_shape = jax.ShapeDtypeStruct((nb, seq, SSM_W), BF16)
    return pl.pallas_call(
        functools.partial(_mixer_kernel, nb=nb, li=li),
        grid=(seq // step_rows,),
        in_specs=[pl.BlockSpec(memory_space=pltpu.SMEM),
                  pl.BlockSpec((nb, step_rows, d), lambda i: (0, i, 0)),
                  _layer_spec(mod.shape, li),
                  _layer_spec(norm_g.shape, li)]
                 + [_column_spec(w_in.shape, li, c0, width) for c0, width in proj_cols]
                 + [_layer_spec(c.shape, li) for c in consts],
        out_specs=[out_spec] * 3,
        out_shape=[out_shape] * 3,
        scratch_shapes=[pltpu.VMEM((m, d), BF16),
                        pltpu.VMEM((m, ATT_W + 2 * KV_W), F32),
                        pltpu.VMEM((nb, N_HEADS * tl, NK), F32),
                        pltpu.VMEM((nb, N_HEADS * tl, NK), BF16),
                        pltpu.VMEM((nb, NK, LANES), BF16),
                        pltpu.VMEM((nb, NK, LANES), BF16),
                        pltpu.VMEM((m, SSM_W), F32),
                        pltpu.VMEM((m, SSM_W), BF16),
                        pltpu.VMEM((m, 2 * N_STATE + LANES), F32),
                        pltpu.VMEM((nb, 2 * N_STATE), F32),
                        pltpu.VMEM((m, SSM_W), F32),
                        pltpu.VMEM((nb, POOL_HALO + tl, POOL_W), F32)]
                       + [pltpu.VMEM((d, width), BF16) for _, width in proj_cols]
                       + [pltpu.VMEM(wts[k].shape[1:], BF16) for k in ("w_glu", "w_pool")],
        compiler_params=pltpu.CompilerParams(dimension_semantics=("arbitrary",), vmem_limit_bytes=VMEM_LIMIT),
        name="mixers",
    )(sinks, x, mod, norm_g, *([w_in] * len(proj_cols)), *consts)


def _merge_kernel(x_ref, mod_ref, ng_ref, ya_ref, ys_ref, yp_ref, wg32, wba_ref, wbs_ref, wbp_ref, wout_ref,
                  fg_ref, o_ref, wg_ref, *, final):
    @pl.when((pl.program_id(0) == 0) & (pl.program_id(1) == 0))
    def _cast_weights():
        wg_ref[...] = wg32[0].astype(BF16)

    x = x_ref[0]
    ms = jnp.mean(x * x, axis=-1, keepdims=True)
    gain = ng_ref[...] * (1.0 + mod_ref[1, 0])
    hb = (x * lax.rsqrt(ms + EPS) * gain + mod_ref[0, 0]).astype(BF16)
    merged = None
    z0 = 0
    for j, (y_ref, wb_ref) in enumerate(((ya_ref, wba_ref), (ys_ref, wbs_ref), (yp_ref, wbp_ref))):
        g = jnp.dot(hb, wg_ref[:, N_GATED + j * D_MODEL:N_GATED + (j + 1) * D_MODEL], preferred_element_type=F32)
        width = y_ref.shape[-1]
        z = jnp.dot(hb, wg_ref[:, z0:z0 + width], preferred_element_type=F32)
        z0 += width
        p = jnp.dot((y_ref[0].astype(F32) * _silu(z)).astype(BF16), wb_ref[...], preferred_element_type=F32)
        term = _sigmoid(g) * p
        merged = term if merged is None else merged + term
    out = jnp.dot(merged.astype(BF16), wout_ref[...], preferred_element_type=F32)
    xn = x + mod_ref[2, 0] * out
    if final:
        ms = jnp.mean(xn * xn, axis=-1, keepdims=True)
        xn = xn * lax.rsqrt(ms + EPS) * fg_ref[...]
    o_ref[0] = xn


def _merge(x, li, mod, norm_g, ya, ys, yp, wts, final_g, final):
    nb, seq, d = x.shape
    tm = MERGE_ROWS
    row = lambda w: pl.BlockSpec((1, tm, w), lambda b, i: (b, i, 0))
    return pl.pallas_call(
        functools.partial(_merge_kernel, final=final),
        grid=(nb, seq // tm),
        in_specs=[row(d),
                  pl.BlockSpec((None, 3, 1, 1, d), lambda b, i: (li, 0, b, 0, 0)),
                  _layer_spec(norm_g.shape, li),
                  row(ATT_W), row(SSM_W), row(POOL_W),
                  _column_spec(wts["w_in"].shape, li, O_ZA, N_GATED + 3 * d),
                  _layer_spec(wts["w_br_att"].shape, li),
                  _layer_spec(wts["w_br_ssm"].shape, li), _layer_spec(wts["w_br_pool"].shape, li),
                  _layer_spec(wts["w_out"].shape, li), _const_spec(final_g.shape)],
        out_specs=row(d),
        out_shape=jax.ShapeDtypeStruct(x.shape, x.dtype),
        scratch_shapes=[pltpu.VMEM((d, N_GATED + 3 * d), BF16)],
        compiler_params=pltpu.CompilerParams(dimension_semantics=("arbitrary", "arbitrary"),
                                             vmem_limit_bytes=VMEM_LIMIT),
        name="merge",
    )(x, mod, norm_g, ya, ys, yp, wts["w_in"], wts["w_br_att"], wts["w_br_ssm"], wts["w_br_pool"], wts["w_out"],
      final_g)


def _block_diag(blocks):
    *lead, n, r, c = blocks.shape
    mask = jnp.asarray(np.kron(np.eye(n, dtype=np.float32), np.ones((r, c), np.float32)))
    rows = blocks.reshape(*lead, n * r, c)
    return jnp.tile(rows, (1,) * len(lead) + (1, n)) * mask


def _stacked_weights(w_in, lam_re, lam_im, coef_re, coef_im, ssm_b_re, ssm_b_im, ssm_c_re, ssm_c_im, ssm_d,
                     w_glu, b_glu, w_pool, pool_scale, w_br_att, w_br_ssm, w_br_pool, w_out):
    depth = w_in.shape[0]
    wts = {"w_in": w_in}
    cr = coef_re[..., None]
    ci = coef_im[..., None]
    bb_re = cr * ssm_b_re - ci * ssm_b_im
    bb_im = cr * ssm_b_im + ci * ssm_b_re
    gpb = LANES // SSM_GROUP
    nblk = SSM_GROUPS // gpb
    bre = jnp.swapaxes(bb_re, 2, 3).reshape(depth, nblk, gpb, SSM_GROUP, SSM_STATE)
    bim = jnp.swapaxes(bb_im, 2, 3).reshape(depth, nblk, gpb, SSM_GROUP, SSM_STATE)
    wts["bmat"] = jnp.concatenate([_block_diag(bre), _block_diag(bim)], axis=-1).astype(BF16)
    cre = jnp.swapaxes(ssm_c_re, 2, 3).reshape(depth, nblk, gpb, SSM_STATE, SSM_GROUP)
    cim = jnp.swapaxes(-ssm_c_im, 2, 3).reshape(depth, nblk, gpb, SSM_STATE, SSM_GROUP)
    wts["cre"] = _block_diag(cre).astype(BF16)
    wts["cim"] = _block_diag(cim).astype(BF16)
    wts["lam"] = jnp.stack([lam_re.reshape(depth, N_STATE), lam_im.reshape(depth, N_STATE)], axis=1)
    wts["dskip"] = ssm_d.reshape(depth, 1, SSM_W)
    wts["b_glu"] = b_glu.reshape(depth, 1, SSM_W)
    wts["pscale"] = pool_scale.reshape(depth, 1, POOL_W)
    wts.update(w_glu=w_glu, w_pool=w_pool)
    for name, w in (("w_br_att", w_br_att), ("w_br_ssm", w_br_ssm), ("w_br_pool", w_br_pool), ("w_out", w_out)):
        wts[name] = w.astype(BF16)
    return wts


def kernel(x, c, norm_g, w_ada, b_ada, w_in, attn_sinks, ssm_a_re, ssm_a_im, ssm_log_dt, ssm_b_re, ssm_b_im,
           ssm_c_re, ssm_c_im, ssm_d, w_glu, b_glu, w_pool, pool_scale, w_br_att, w_br_ssm, w_br_pool, w_out,
           final_g):
    nb, seq, d = x.shape
    depth = w_in.shape[0]
    assert d == D_MODEL and nb == SUBLANES and seq % MERGE_ROWS == 0
    mod_all = _modulation(c, w_ada, b_ada)
    lam_re, lam_im, coef_re, coef_im = _ssm_prep(ssm_a_re, ssm_a_im, ssm_log_dt)
    fg = final_g.reshape(1, d)
    mod = jnp.transpose(mod_all.reshape(depth, nb, 3, d), (0, 2, 1, 3)).reshape(depth, 3, nb, 1, d)
    ng = norm_g.reshape(depth, 1, d)
    wts = _stacked_weights(w_in, lam_re, lam_im, coef_re, coef_im, ssm_b_re, ssm_b_im, ssm_c_re, ssm_c_im, ssm_d,
                           w_glu, b_glu, w_pool, pool_scale, w_br_att, w_br_ssm, w_br_pool, w_out)
    for li in range(depth):
        ya, ys, yp = _mixers(x, li, mod, ng, attn_sinks, wts)
        x = _merge(x, li, mod, ng, ya, ys, yp, wts, fg, final=(li == depth - 1))
    return x
```

```python
import functools
import math

import numpy as np
import jax
import jax.numpy as jnp
from jax import lax
from jax.experimental import pallas as pl
from jax.experimental.pallas import tpu as pltpu

F32 = jnp.float32
BF16 = jnp.bfloat16

D_MODEL = 1024
CHUNK = 64
N_HEADS = 8
N_KV_HEADS = 2
HEAD_DIM = 64
WIN_CHUNKS = 2
NK = (WIN_CHUNKS + 1) * CHUNK
ATT_W = N_HEADS * HEAD_DIM
KV_W = N_KV_HEADS * HEAD_DIM
SSM_W = 512
SSM_GROUP = 16
SSM_GROUPS = 32
SSM_STATE = 64
N_STATE = SSM_GROUPS * SSM_STATE
POOL_W = 512
POOL_WINDOWS = (2, 4, 8, 16)
POOL_GW = 128
POOL_HALO = 16
O_K, O_US, O_UP, O_ZA, O_ZS, O_ZP, O_G = 512, 768, 1280, 1792, 2304, 2816, 3328
CHUNKS_PER_STEP = 1
MERGE_ROWS = 1024
MOD_COLS = 1024
EPS = 1e-6
NEG_INF = -1e30
LOG2E = math.log2(math.e)
LANES = 128
SUBLANES = 8
VMEM_LIMIT = 58 * 1024 * 1024


def _sigmoid(x):
    return 0.5 * jnp.tanh(0.5 * x) + 0.5


def _silu(x):
    return x * _sigmoid(x)


def _gelu_tanh(x):
    return 0.5 * x * (1.0 + jnp.tanh(math.sqrt(2.0 / math.pi) * (x + 0.044715 * (x * x * x))))


def _const_spec(shape):
    nd = len(shape)
    return pl.BlockSpec(shape, lambda *_: (0,) * nd, pipeline_mode=pl.Buffered(1))


def _column_spec(shape, li, c0, width):
    return pl.BlockSpec((pl.Element(1), pl.Element(shape[1]), pl.Element(width)), lambda *_: (li, 0, c0),
                        pipeline_mode=pl.Buffered(1))


def _layer_spec(shape, li):
    nd = len(shape)
    return pl.BlockSpec((None,) + tuple(shape[1:]), lambda *_: (li,) + (0,) * (nd - 1),
                        pipeline_mode=pl.Buffered(1))


def _mod_kernel(c_ref, w_ref, b_ref, o_ref):
    ca = _silu(c_ref[...])
    o_ref[0] = jnp.dot(ca.astype(BF16), w_ref[0].astype(BF16), preferred_element_type=F32) + b_ref[0]


def _modulation(c, w_ada, b_ada):
    depth, d, n3 = w_ada.shape
    b = c.shape[0]
    bw = MOD_COLS
    return pl.pallas_call(
        _mod_kernel,
        grid=(depth, n3 // bw),
        in_specs=[pl.BlockSpec((b, d), lambda l, j: (0, 0)),
                  pl.BlockSpec((1, d, bw), lambda l, j: (l, 0, j)),
                  pl.BlockSpec((1, 1, bw), lambda l, j: (l, 0, j))],
        out_specs=pl.BlockSpec((1, b, bw), lambda l, j: (l, 0, j)),
        out_shape=jax.ShapeDtypeStruct((depth, b, n3), F32),
        name="adaln_mod",
    )(c, w_ada, b_ada.reshape(depth, 1, n3))


def _ssm_prep_kernel(are_ref, aim_ref, ldt_ref, lre_ref, lim_ref, cre_ref, cim_ref):
    a_re = are_ref[0]
    a_im = aim_ref[0]
    dt = jnp.exp(ldt_ref[0])
    mag = jnp.exp(a_re * dt)
    ang = a_im * dt
    l_re = mag * jnp.cos(ang)
    l_im = mag * jnp.sin(ang)
    n_re = l_re - 1.0
    den = a_re * a_re + a_im * a_im
    lre_ref[0] = l_re
    lim_ref[0] = l_im
    cre_ref[0] = (n_re * a_re + l_im * a_im) / den
    cim_ref[0] = (l_im * a_re - n_re * a_im) / den


def _ssm_prep(a_re, a_im, log_dt):
    depth, g, p = a_re.shape
    spec = pl.BlockSpec((1, g, p), lambda l: (l, 0, 0))
    out = jax.ShapeDtypeStruct((depth, g, p), F32)
    return pl.pallas_call(
        _ssm_prep_kernel,
        grid=(depth,),
        in_specs=[spec, spec, pl.BlockSpec((1, g, 1), lambda l: (l, 0, 0))],
        out_specs=[spec] * 4,
        out_shape=[out] * 4,
        name="ssm_discretise",
    )(a_re, a_im, log_dt.reshape(depth, g, 1))


def _mixer_kernel(sink_ref, x_ref, mod_ref, ng_ref, wq32, wkv32, wza32, wus32, wzs32, wup32, wzp32,
                  bmat_ref, cre_ref, cim_ref, lam_ref, dskip_ref, wglu32, bglu_ref, wpool32, pscale_ref,
                  ya_ref, ys_ref, yp_ref,
                  hb_ref, att_ref, s_ref, p_ref, kring_ref, vring_ref, us_ref, ut_ref, bu_ref, st_ref, yt_ref,
                  pext_ref, zp_ref, wq_ref, wkv_ref, wza_ref, wus_ref, wzs_ref, wup_ref, wzp_ref, wglu_ref, wpool_ref,
                  *, nb, li):
    step = pl.program_id(0)
    tl = CHUNK
    m = nb * tl
    half = N_STATE // 4
    im0 = N_STATE + LANES
    hpk = N_HEADS // N_KV_HEADS
    rb = 2 * tl
    blk_heads = [(kv * hpk + e, kv * hpk + 2 + e) for kv in range(N_KV_HEADS) for e in range(2)]

    def chunk_index(c):
        return step * CHUNKS_PER_STEP + c

    def chunk_rows(c):
        return slice(c * tl, (c + 1) * tl)

    @pl.when(step == 0)
    def _init():
        kring_ref[...] = jnp.zeros_like(kring_ref)
        vring_ref[...] = jnp.zeros_like(vring_ref)
        st_ref[...] = jnp.zeros_like(st_ref)
        pext_ref[:, 0:POOL_HALO, :] = jnp.zeros((nb, POOL_HALO, POOL_W), F32)
        for src, dst in ((wq32, wq_ref), (wkv32, wkv_ref), (wza32, wza_ref), (wus32, wus_ref), (wzs32, wzs_ref),
                         (wup32, wup_ref), (wzp32, wzp_ref)):
            dst[...] = src[0].astype(BF16)
        wglu_ref[...] = wglu32[...].astype(BF16)
        wpool_ref[...] = wpool32[...].astype(BF16)

    def norm(c):
        x = x_ref[:, chunk_rows(c), :]
        ms = jnp.mean(x * x, axis=-1, keepdims=True)
        gain = ng_ref[...] * (1.0 + mod_ref[1])
        h = x * lax.rsqrt(ms + EPS) * gain + mod_ref[0]
        hb_ref[...] = h.reshape(m, D_MODEL).astype(BF16)

    def att_project():
        hb = hb_ref[...]
        att_ref[:, :ATT_W] = jnp.dot(hb, wq_ref[...], preferred_element_type=F32) * (LOG2E / math.sqrt(HEAD_DIM))
        att_ref[:, ATT_W:ATT_W + 2 * KV_W] = jnp.dot(hb, wkv_ref[...], preferred_element_type=F32)
        att_ref[:, ATT_W + 2 * KV_W:] = jnp.dot(hb, wza_ref[...], preferred_element_type=F32)
        for ring, c0 in ((kring_ref, ATT_W), (vring_ref, ATT_W + KV_W)):
            for b in range(nb):
                ring[b, 0:NK - CHUNK, :] = ring[b, CHUNK:NK, :]
            ring[:, NK - CHUNK:NK, :] = att_ref[:, c0:c0 + KV_W].reshape(nb, tl, LANES).astype(BF16)

    def att_scores():
        lo = lax.broadcasted_iota(jnp.int32, (tl, LANES), 1) < HEAD_DIM
        zero = jnp.zeros((tl, LANES), F32)
        for b in range(nb):
            pairs = [att_ref[b * tl:(b + 1) * tl, j * LANES:(j + 1) * LANES] for j in range(N_HEADS // 2)]
            swapped = [pltpu.roll(q, HEAD_DIM, axis=1) for q in pairs]
            rows = []
            for ha, hb_ in blk_heads:
                for h in (ha, hb_):
                    kv, e = h // hpk, h % 2
                    src = pairs[h // 2] if e == kv else swapped[h // 2]
                    rows.append(jnp.where(lo, src, zero) if kv == 0 else jnp.where(lo, zero, src))
            qq = jnp.concatenate(rows, axis=0).astype(BF16)
            s_ref[b] = lax.dot_general(qq, kring_ref[b], (((1,), (1,)), ((), ())), preferred_element_type=F32)

    def att_softmax(c, blk):
        ha, hb_ = blk_heads[blk]
        row = lax.broadcasted_iota(jnp.int32, (rb, NK), 0)
        kj = lax.broadcasted_iota(jnp.int32, (rb, NK), 1)
        qi = jnp.where(row < tl, row, row - tl)
        dist = jnp.abs(qi + WIN_CHUNKS * CHUNK - kj).astype(F32)
        valid = (chunk_index(c) * CHUNK + kj) >= WIN_CHUNKS * CHUNK
        first = lax.broadcasted_iota(jnp.int32, (rb, 1), 0) < tl
        slope = jnp.where(first, LOG2E * 2.0 ** (-(ha + 1)), LOG2E * 2.0 ** (-(hb_ + 1)))
        nbias = jnp.where(valid, -slope * dist, NEG_INF)
        sink = LOG2E * jnp.where(first, sink_ref[li, ha], sink_ref[li, hb_])
        for b in range(nb):
            s = s_ref[b, blk * rb:(blk + 1) * rb, :] + nbias
            mx = jnp.maximum(jnp.max(s, axis=-1, keepdims=True), sink)
            p = jnp.exp2(s - mx)
            den = jnp.sum(p, axis=-1, keepdims=True) + jnp.exp2(sink - mx)
            p_ref[b, blk * rb:(blk + 1) * rb, :] = (p * (1.0 / den)).astype(BF16)

    def att_values(c):
        lo = lax.broadcasted_iota(jnp.int32, (rb, LANES), 1) < HEAD_DIM
        for b in range(nb):
            t = jnp.dot(p_ref[b], vring_ref[b], preferred_element_type=F32)
            for kv in range(N_KV_HEADS):
                te = t[2 * kv * rb:(2 * kv + 1) * rb]
                to = t[(2 * kv + 1) * rb:(2 * kv + 2) * rb]
                if kv == 0:
                    o = jnp.where(lo, te, pltpu.roll(to, HEAD_DIM, axis=1))
                else:
                    o = jnp.where(lo, pltpu.roll(te, HEAD_DIM, axis=1), to)
                for j in range(2):
                    c0 = (2 * kv + j) * LANES
                    z = att_ref[b * tl:(b + 1) * tl, ATT_W + 2 * KV_W + c0:ATT_W + 2 * KV_W + c0 + LANES]
                    ya_ref[b, chunk_rows(c), c0:c0 + LANES] = (o[j * tl:(j + 1) * tl] * _silu(z)).astype(BF16)

    def ssm_project():
        hb = hb_ref[...]
        us_ref[:, :SSM_W] = jnp.dot(hb, wus_ref[...], preferred_element_type=F32)
        us_ref[:, SSM_W:] = jnp.dot(hb, wzs_ref[...], preferred_element_type=F32)
        u = us_ref[:, :SSM_W].reshape(nb, tl, SSM_W)
        ut_ref[...] = jnp.swapaxes(u, 0, 1).reshape(m, SSM_W).astype(BF16)

    def ssm_expand(r):
        res = jnp.dot(ut_ref[:, r * LANES:(r + 1) * LANES], bmat_ref[r], preferred_element_type=F32)
        bu_ref[:, r * half:(r + 1) * half] = res[:, :half]
        bu_ref[:, im0 + r * half:im0 + (r + 1) * half] = res[:, half:]

    def ssm_scan(r):
        c0 = r * half
        a_re = jnp.broadcast_to(lam_ref[0:1, c0:c0 + half], (nb, half))
        a_im = jnp.broadcast_to(lam_ref[1:2, c0:c0 + half], (nb, half))
        s_re = st_ref[:, c0:c0 + half]
        s_im = st_ref[:, N_STATE + c0:N_STATE + c0 + half]
        for t in range(tl):
            r0 = t * nb
            n_re = a_re * s_re - a_im * s_im + bu_ref[r0:r0 + nb, c0:c0 + half]
            n_im = a_re * s_im + a_im * s_re + bu_ref[r0:r0 + nb, im0 + c0:im0 + c0 + half]
            bu_ref[r0:r0 + nb, c0:c0 + half] = n_re
            bu_ref[r0:r0 + nb, im0 + c0:im0 + c0 + half] = n_im
            s_re, s_im = n_re, n_im
        st_ref[:, c0:c0 + half] = s_re
        st_ref[:, N_STATE + c0:N_STATE + c0 + half] = s_im

    def ssm_contract(r):
        x_re = bu_ref[:, r * half:(r + 1) * half].astype(BF16)
        x_im = bu_ref[:, im0 + r * half:im0 + (r + 1) * half].astype(BF16)
        yt_ref[:, r * LANES:(r + 1) * LANES] = (jnp.dot(x_re, cre_ref[r], preferred_element_type=F32)
                                               + jnp.dot(x_im, cim_ref[r], preferred_element_type=F32))

    def ssm_output(c):
        y = jnp.swapaxes(yt_ref[...].reshape(tl, nb, SSM_W), 0, 1).reshape(m, SSM_W)
        y = _gelu_tanh(y + dskip_ref[...] * us_ref[:, :SSM_W])
        gl = jnp.dot(y.astype(BF16), wglu_ref[...], preferred_element_type=F32) + bglu_ref[...]
        y = y * _sigmoid(gl)
        ys_ref[:, chunk_rows(c), :] = (y * _silu(us_ref[:, SSM_W:])).reshape(nb, tl, SSM_W).astype(BF16)

    def pool_project():
        hb = hb_ref[...]
        zp_ref[...] = jnp.dot(hb, wzp_ref[...], preferred_element_type=F32)
        pext_ref[:, POOL_HALO:POOL_HALO + tl, :] = jnp.dot(hb, wup_ref[...], preferred_element_type=F32
                                                           ).reshape(nb, tl, POOL_W)

    def pool_mix(c):
        tpos = chunk_index(c) * tl + lax.broadcasted_iota(jnp.int32, (1, tl, 1), 1)
        groups = []
        for gi, w in enumerate(POOL_WINDOWS):
            cs = slice(gi * POOL_GW, (gi + 1) * POOL_GW)
            ext = pext_ref[:, :, cs]
            tok = ext[:, POOL_HALO:, :]
            ssum = ext
            k = 1
            while k < w:
                ssum = ssum + pltpu.roll(ssum, k, axis=1)
                k *= 2
            cnt = jnp.minimum(tpos + 1, w).astype(F32)
            pooled = ssum[:, POOL_HALO:, :] / cnt - tok
            groups.append(jnp.dot(pooled.reshape(m, POOL_GW).astype(BF16), wpool_ref[gi],
                                  preferred_element_type=F32))
        y = jnp.concatenate(groups, axis=1) * pscale_ref[...]
        yp_ref[:, chunk_rows(c), :] = (y * _silu(zp_ref[...])).reshape(nb, tl, POOL_W).astype(BF16)
        pext_ref[:, 0:POOL_HALO, :] = pext_ref[:, tl:tl + POOL_HALO, :]

    P = functools.partial
    for c in range(CHUNKS_PER_STEP):
        schedule = (P(norm, c), att_project, pool_project, att_scores, ssm_project,
                    P(att_softmax, c, 0), P(ssm_expand, 0), P(att_softmax, c, 1), P(ssm_expand, 1),
                    P(ssm_scan, 0), P(ssm_expand, 2), P(att_softmax, c, 2), P(ssm_scan, 1), P(ssm_expand, 3),
                    P(att_softmax, c, 3), P(ssm_contract, 0), P(ssm_scan, 2), P(ssm_contract, 1), P(pool_mix, c),
                    P(ssm_scan, 3), P(att_values, c), P(ssm_contract, 2), P(ssm_contract, 3), P(ssm_output, c))
        for phase in schedule:
            phase()


def _mixers(x, li, mod, norm_g, sinks, wts):
    nb, seq, d = x.shape
    tl = CHUNK
    m = nb * tl
    consts = (wts["bmat"], wts["cre"],
              wts["cim"], wts["lam"], wts["dskip"], wts["w_glu"], wts["b_glu"], wts["w_pool"], wts["pscale"])
    w_in = wts["w_in"]
    proj_cols = ((0, ATT_W), (O_K, 2 * KV_W), (O_ZA, ATT_W), (O_US, SSM_W), (O_ZS, SSM_W), (O_UP, POOL_W),
                 (O_ZP, POOL_W))
    step_rows = CHUNKS_PER_STEP * tl
    out_spec = pl.BlockSpec((nb, step_rows, SSM_W), lambda i: (0, i, 0))
    out_shape = jax.ShapeDtypeStruct((nb, seq, SSM_W), BF16)
    return pl.pallas_call(
        functools.partial(_mixer_kernel, nb=nb, li=li),
        grid=(seq // step_rows,),
        in_specs=[pl.BlockSpec(memory_space=pltpu.SMEM),
                  pl.BlockSpec((nb, step_rows, d), lambda i: (0, i, 0)),
                  _layer_spec(mod.shape, li),
                  _layer_spec(norm_g.shape, li)]
                 + [_column_spec(w_in.shape, li, c0, width) for c0, width in proj_cols]
                 + [_layer_spec(c.shape, li) for c in consts],
        out_specs=[out_spec] * 3,
        out_shape=[out_shape] * 3,
        scratch_shapes=[pltpu.VMEM((m, d), BF16),
                        pltpu.VMEM((m, 2 * ATT_W + 2 * KV_W), F32),
                        pltpu.VMEM((nb, N_HEADS * tl, NK), F32),
                        pltpu.VMEM((nb, N_HEADS * tl, NK), BF16),
                        pltpu.VMEM((nb, NK, LANES), BF16),
                        pltpu.VMEM((nb, NK, LANES), BF16),
                        pltpu.VMEM((m, 2 * SSM_W), F32),
                        pltpu.VMEM((m, SSM_W), BF16),
                        pltpu.VMEM((m, 2 * N_STATE + LANES), F32),
                        pltpu.VMEM((nb, 2 * N_STATE), F32),
                        pltpu.VMEM((m, SSM_W), F32),
                        pltpu.VMEM((nb, POOL_HALO + tl, POOL_W), F32),
                        pltpu.VMEM((m, POOL_W), F32)]
                       + [pltpu.VMEM((d, width), BF16) for _, width in proj_cols]
                       + [pltpu.VMEM(wts[k].shape[1:], BF16) for k in ("w_glu", "w_pool")],
        compiler_params=pltpu.CompilerParams(dimension_semantics=("arbitrary",), vmem_limit_bytes=VMEM_LIMIT),
        name="mixers",
    )(sinks, x, mod, norm_g, *([w_in] * len(proj_cols)), *consts)


def _merge_kernel(x_ref, mod_ref, ng_ref, ya_ref, ys_ref, yp_ref, wg32, wba_ref, wbs_ref, wbp_ref, wout_ref,
                  fg_ref, o_ref, wg_ref, *, final):
    @pl.when((pl.program_id(0) == 0) & (pl.program_id(1) == 0))
    def _cast_weights():
        wg_ref[...] = wg32[0].astype(BF16)

    tm = x_ref.shape[1]
    for r0 in range(0, tm, tm // 2):
        rows = slice(r0, r0 + tm // 2)
        x = x_ref[0, rows, :]
        ms = jnp.mean(x * x, axis=-1, keepdims=True)
        gain = ng_ref[...] * (1.0 + mod_ref[1, 0])
        hb = (x * lax.rsqrt(ms + EPS) * gain + mod_ref[0, 0]).astype(BF16)
        merged = None
        for j, (y_ref, wb_ref) in enumerate(((ya_ref, wba_ref), (ys_ref, wbs_ref), (yp_ref, wbp_ref))):
            g = jnp.dot(hb, wg_ref[:, j * D_MODEL:(j + 1) * D_MODEL], preferred_element_type=F32)
            p = jnp.dot(y_ref[0, rows, :], wb_ref[...], preferred_element_type=F32)
            term = _sigmoid(g) * p
            merged = term if merged is None else merged + term
        out = jnp.dot(merged.astype(BF16), wout_ref[...], preferred_element_type=F32)
        xn = x + mod_ref[2, 0] * out
        if final:
            ms = jnp.mean(xn * xn, axis=-1, keepdims=True)
            xn = xn * lax.rsqrt(ms + EPS) * fg_ref[...]
        o_ref[0, rows, :] = xn


def _merge(x, li, mod, norm_g, ya, ys, yp, wts, final_g, final):
    nb, seq, d = x.shape
    tm = MERGE_ROWS
    row = lambda w: pl.BlockSpec((1, tm, w), lambda b, i: (b, i, 0))
    return pl.pallas_call(
        functools.partial(_merge_kernel, final=final),
        grid=(nb, seq // tm),
        in_specs=[row(d),
                  pl.BlockSpec((None, 3, 1, 1, d), lambda b, i: (li, 0, b, 0, 0)),
                  _layer_spec(norm_g.shape, li),
                  row(ATT_W), row(SSM_W), row(POOL_W),
                  _column_spec(wts["w_in"].shape, li, O_G, 3 * d), _layer_spec(wts["w_br_att"].shape, li),
                  _layer_spec(wts["w_br_ssm"].shape, li), _layer_spec(wts["w_br_pool"].shape, li),
                  _layer_spec(wts["w_out"].shape, li), _const_spec(final_g.shape)],
        out_specs=row(d),
        out_shape=jax.ShapeDtypeStruct(x.shape, x.dtype),
        scratch_shapes=[pltpu.VMEM((d, 3 * d), BF16)],
        compiler_params=pltpu.CompilerParams(dimension_semantics=("arbitrary", "arbitrary"),
                                             vmem_limit_bytes=VMEM_LIMIT),
        name="merge",
    )(x, mod, norm_g, ya, ys, yp, wts["w_in"], wts["w_br_att"], wts["w_br_ssm"], wts["w_br_pool"], wts["w_out"],
      final_g)


def _block_diag(blocks):
    *lead, n, r, c = blocks.shape
    mask = jnp.asarray(np.kron(np.eye(n, dtype=np.float32), np.ones((r, c), np.float32)))
    rows = blocks.reshape(*lead, n * r, c)
    return jnp.tile(rows, (1,) * len(lead) + (1, n)) * mask


def _stacked_weights(w_in, lam_re, lam_im, coef_re, coef_im, ssm_b_re, ssm_b_im, ssm_c_re, ssm_c_im, ssm_d,
                     w_glu, b_glu, w_pool, pool_scale, w_br_att, w_br_ssm, w_br_pool, w_out):
    depth = w_in.shape[0]
    wts = {"w_in": w_in}
    cr = coef_re[..., None]
    ci = coef_im[..., None]
    bb_re = cr * ssm_b_re - ci * ssm_b_im
    bb_im = cr * ssm_b_im + ci * ssm_b_re
    gpb = LANES // SSM_GROUP
    nblk = SSM_GROUPS // gpb
    bre = jnp.swapaxes(bb_re, 2, 3).reshape(depth, nblk, gpb, SSM_GROUP, SSM_STATE)
    bim = jnp.swapaxes(bb_im, 2, 3).reshape(depth, nblk, gpb, SSM_GROUP, SSM_STATE)
    wts["bmat"] = jnp.concatenate([_block_diag(bre), _block_diag(bim)], axis=-1).astype(BF16)
    cre = jnp.swapaxes(ssm_c_re, 2, 3).reshape(depth, nblk, gpb, SSM_STATE, SSM_GROUP)
    cim = jnp.swapaxes(-ssm_c_im, 2, 3).reshape(depth, nblk, gpb, SSM_STATE, SSM_GROUP)
    wts["cre"] = _block_diag(cre).astype(BF16)
    wts["cim"] = _block_diag(cim).astype(BF16)
    wts["lam"] = jnp.stack([lam_re.reshape(depth, N_STATE), lam_im.reshape(depth, N_STATE)], axis=1)
    wts["dskip"] = ssm_d.reshape(depth, 1, SSM_W)
    wts["b_glu"] = b_glu.reshape(depth, 1, SSM_W)
    wts["pscale"] = pool_scale.reshape(depth, 1, POOL_W)
    wts.update(w_glu=w_glu, w_pool=w_pool)
    for name, w in (("w_br_att", w_br_att), ("w_br_ssm", w_br_ssm), ("w_br_pool", w_br_pool), ("w_out", w_out)):
        wts[name] = w.astype(BF16)
    return wts


def kernel(x, c, norm_g, w_ada, b_ada, w_in, attn_sinks, ssm_a_re, ssm_a_im, ssm_log_dt, ssm_b_re, ssm_b_im,
           ssm_c_re, ssm_c_im, ssm_d, w_glu, b_glu, w_pool, pool_scale, w_br_att, w_br_ssm, w_br_pool, w_out,
           final_g):
    nb, seq, d = x.shape
    depth = w_in.shape[0]
    assert d == D_MODEL and nb == SUBLANES and seq % MERGE_ROWS == 0
    mod_all = _modulation(c, w_ada, b_ada)
    lam_re, lam_im, coef_re, coef_im = _ssm_prep(ssm_a_re, ssm_a_im, ssm_log_dt)
    fg = final_g.reshape(1, d)
    mod = jnp.transpose(mod_all.reshape(depth, nb, 3, d), (0, 2, 1, 3)).reshape(depth, 3, nb, 1, d)
    ng = norm_g.reshape(depth, 1, d)
    wts = _stacked_weights(w_in, lam_re, lam_im, coef_re, coef_im, ssm_b_re, ssm_b_im, ssm_c_re, ssm_c_im, ssm_d,
                           w_glu, b_glu, w_pool, pool_scale, w_br_att, w_br_ssm, w_br_pool, w_out)
    for li in range(depth):
        ya, ys, yp = _mixers(x, li, mod, ng, attn_sinks, wts)
        x = _merge(x, li, mod, ng, ya, ys, yp, wts, fg, final=(li == depth - 1))
    return x
```

```python
import functools
import math

import numpy as np
import jax
import jax.numpy as jnp
from jax import lax
from jax.experimental import pallas as pl
from jax.experimental.pallas import tpu as pltpu

F32 = jnp.float32
BF16 = jnp.bfloat16

D_MODEL = 1024
CHUNK = 64
N_HEADS = 8
N_KV_HEADS = 2
HEAD_DIM = 64
WIN_CHUNKS = 2
NK = (WIN_CHUNKS + 1) * CHUNK
ATT_W = N_HEADS * HEAD_DIM
KV_W = N_KV_HEADS * HEAD_DIM
SSM_W = 512
SSM_GROUP = 16
SSM_GROUPS = 32
SSM_STATE = 64
N_STATE = SSM_GROUPS * SSM_STATE
POOL_W = 512
POOL_WINDOWS = (2, 4, 8, 16)
POOL_GW = 128
POOL_HALO = 16
O_K, O_US, O_UP, O_ZA, O_ZS, O_ZP, O_G = 512, 768, 1280, 1792, 2304, 2816, 3328
CHUNKS_PER_STEP = 1
MERGE_ROWS = 1024
MOD_COLS = 1024
EPS = 1e-6
NEG_INF = -1e30
LOG2E = math.log2(math.e)
LANES = 128
SUBLANES = 8
VMEM_LIMIT = 58 * 1024 * 1024


def _sigmoid(x):
    return 0.5 * jnp.tanh(0.5 * x) + 0.5


def _silu(x):
    return x * _sigmoid(x)


def _gelu_tanh(x):
    return 0.5 * x * (1.0 + jnp.tanh(math.sqrt(2.0 / math.pi) * (x + 0.044715 * (x * x * x))))


def _const_spec(shape):
    nd = len(shape)
    return pl.BlockSpec(shape, lambda *_: (0,) * nd, pipeline_mode=pl.Buffered(1))


def _column_spec(shape, li, c0, width):
    return pl.BlockSpec((pl.Element(1), pl.Element(shape[1]), pl.Element(width)), lambda *_: (li, 0, c0),
                        pipeline_mode=pl.Buffered(1))


def _layer_spec(shape, li):
    nd = len(shape)
    return pl.BlockSpec((None,) + tuple(shape[1:]), lambda *_: (li,) + (0,) * (nd - 1),
                        pipeline_mode=pl.Buffered(1))


def _mod_kernel(c_ref, w_ref, b_ref, o_ref):
    ca = _silu(c_ref[...])
    o_ref[0] = jnp.dot(ca.astype(BF16), w_ref[0].astype(BF16), preferred_element_type=F32) + b_ref[0]


def _modulation(c, w_ada, b_ada):
    depth, d, n3 = w_ada.shape
    b = c.shape[0]
    bw = MOD_COLS
    return pl.pallas_call(
        _mod_kernel,
        grid=(depth, n3 // bw),
        in_specs=[pl.BlockSpec((b, d), lambda l, j: (0, 0)),
                  pl.BlockSpec((1, d, bw), lambda l, j: (l, 0, j)),
                  pl.BlockSpec((1, 1, bw), lambda l, j: (l, 0, j))],
        out_specs=pl.BlockSpec((1, b, bw), lambda l, j: (l, 0, j)),
        out_shape=jax.ShapeDtypeStruct((depth, b, n3), F32),
        name="adaln_mod",
    )(c, w_ada, b_ada.reshape(depth, 1, n3))


def _ssm_prep_kernel(are_ref, aim_ref, ldt_ref, lre_ref, lim_ref, cre_ref, cim_ref):
    a_re = are_ref[0]
    a_im = aim_ref[0]
    dt = jnp.exp(ldt_ref[0])
    mag = jnp.exp(a_re * dt)
    ang = a_im * dt
    l_re = mag * jnp.cos(ang)
    l_im = mag * jnp.sin(ang)
    n_re = l_re - 1.0
    den = a_re * a_re + a_im * a_im
    lre_ref[0] = l_re
    lim_ref[0] = l_im
    cre_ref[0] = (n_re * a_re + l_im * a_im) / den
    cim_ref[0] = (l_im * a_re - n_re * a_im) / den


def _ssm_prep(a_re, a_im, log_dt):
    depth, g, p = a_re.shape
    spec = pl.BlockSpec((1, g, p), lambda l: (l, 0, 0))
    out = jax.ShapeDtypeStruct((depth, g, p), F32)
    return pl.pallas_call(
        _ssm_prep_kernel,
        grid=(depth,),
        in_specs=[spec, spec, pl.BlockSpec((1, g, 1), lambda l: (l, 0, 0))],
        out_specs=[spec] * 4,
        out_shape=[out] * 4,
        name="ssm_discretise",
    )(a_re, a_im, log_dt.reshape(depth, g, 1))


def _mixer_kernel(sink_ref, x_ref, mod_ref, ng_ref, wq32, wkv32, wza32, wus32, wzs32, wup32, wzp32,
                  bmat_ref, cre_ref, cim_ref, lam_ref, dskip_ref, wglu32, bglu_ref, wpool32, pscale_ref,
                  y_ref,
                  hb_ref, att_ref, s_ref, p_ref, kring_ref, vring_ref, us_ref, ut_ref, bu_ref, st_ref, yt_ref,
                  pext_ref, zp_ref, wq_ref, wkv_ref, wza_ref, wus_ref, wzs_ref, wup_ref, wzp_ref, wglu_ref, wpool_ref,
                  *, nb, li):
    step = pl.program_id(0)
    tl = CHUNK
    m = nb * tl
    ya_ref, ys_ref, yp_ref = (y_ref.at[:, :, k * SSM_W:(k + 1) * SSM_W] for k in range(3))
    half = N_STATE // 4
    im0 = N_STATE + LANES
    hpk = N_HEADS // N_KV_HEADS
    rb = 2 * tl
    blk_heads = [(kv * hpk + e, kv * hpk + 2 + e) for kv in range(N_KV_HEADS) for e in range(2)]

    def chunk_index(c):
        return step * CHUNKS_PER_STEP + c

    def chunk_rows(c):
        return slice(c * tl, (c + 1) * tl)

    @pl.when(step == 0)
    def _init():
        kring_ref[...] = jnp.zeros_like(kring_ref)
        vring_ref[...] = jnp.zeros_like(vring_ref)
        st_ref[...] = jnp.zeros_like(st_ref)
        pext_ref[:, 0:POOL_HALO, :] = jnp.zeros((nb, POOL_HALO, POOL_W), F32)
        for src, dst in ((wq32, wq_ref), (wkv32, wkv_ref), (wza32, wza_ref), (wus32, wus_ref), (wzs32, wzs_ref),
                         (wup32, wup_ref), (wzp32, wzp_ref)):
            dst[...] = src[0].astype(BF16)
        wglu_ref[...] = wglu32[...].astype(BF16)
        wpool_ref[...] = wpool32[...].astype(BF16)

    def norm(c):
        x = x_ref[:, chunk_rows(c), :]
        ms = jnp.mean(x * x, axis=-1, keepdims=True)
        gain = ng_ref[...] * (1.0 + mod_ref[1])
        h = x * lax.rsqrt(ms + EPS) * gain + mod_ref[0]
        hb_ref[...] = h.reshape(m, D_MODEL).astype(BF16)

    def att_project():
        hb = hb_ref[...]
        att_ref[:, :ATT_W] = jnp.dot(hb, wq_ref[...], preferred_element_type=F32) * (LOG2E / math.sqrt(HEAD_DIM))
        att_ref[:, ATT_W:ATT_W + 2 * KV_W] = jnp.dot(hb, wkv_ref[...], preferred_element_type=F32)
        att_ref[:, ATT_W + 2 * KV_W:] = jnp.dot(hb, wza_ref[...], preferred_element_type=F32)
        for ring, c0 in ((kring_ref, ATT_W), (vring_ref, ATT_W + KV_W)):
            for b in range(nb):
                ring[b, 0:NK - CHUNK, :] = ring[b, CHUNK:NK, :]
            ring[:, NK - CHUNK:NK, :] = att_ref[:, c0:c0 + KV_W].reshape(nb, tl, LANES).astype(BF16)

    def att_scores():
        lo = lax.broadcasted_iota(jnp.int32, (tl, LANES), 1) < HEAD_DIM
        zero = jnp.zeros((tl, LANES), F32)
        for b in range(nb):
            pairs = [att_ref[b * tl:(b + 1) * tl, j * LANES:(j + 1) * LANES] for j in range(N_HEADS // 2)]
            swapped = [pltpu.roll(q, HEAD_DIM, axis=1) for q in pairs]
            rows = []
            for ha, hb_ in blk_heads:
                for h in (ha, hb_):
                    kv, e = h // hpk, h % 2
                    src = pairs[h // 2] if e == kv else swapped[h // 2]
                    rows.append(jnp.where(lo, src, zero) if kv == 0 else jnp.where(lo, zero, src))
            qq = jnp.concatenate(rows, axis=0).astype(BF16)
            s_ref[b] = lax.dot_general(qq, kring_ref[b], (((1,), (1,)), ((), ())), preferred_element_type=F32)

    def att_softmax(c, blk):
        ha, hb_ = blk_heads[blk]
        row = lax.broadcasted_iota(jnp.int32, (rb, NK), 0)
        kj = lax.broadcasted_iota(jnp.int32, (rb, NK), 1)
        qi = jnp.where(row < tl, row, row - tl)
        dist = jnp.abs(qi + WIN_CHUNKS * CHUNK - kj).astype(F32)
        valid = (chunk_index(c) * CHUNK + kj) >= WIN_CHUNKS * CHUNK
        first = lax.broadcasted_iota(jnp.int32, (rb, 1), 0) < tl
        slope = jnp.where(first, LOG2E * 2.0 ** (-(ha + 1)), LOG2E * 2.0 ** (-(hb_ + 1)))
        nbias = jnp.where(valid, -slope * dist, NEG_INF)
        sink = LOG2E * jnp.where(first, sink_ref[li, ha], sink_ref[li, hb_])
        for b in range(nb):
            s = s_ref[b, blk * rb:(blk + 1) * rb, :] + nbias
            mx = jnp.maximum(jnp.max(s, axis=-1, keepdims=True), sink)
            p = jnp.exp2(s - mx)
            den = jnp.sum(p, axis=-1, keepdims=True) + jnp.exp2(sink - mx)
            p_ref[b, blk * rb:(blk + 1) * rb, :] = (p * (1.0 / den)).astype(BF16)

    def att_values(c):
        lo = lax.broadcasted_iota(jnp.int32, (rb, LANES), 1) < HEAD_DIM
        for b in range(nb):
            t = jnp.dot(p_ref[b], vring_ref[b], preferred_element_type=F32)
            for kv in range(N_KV_HEADS):
                te = t[2 * kv * rb:(2 * kv + 1) * rb]
                to = t[(2 * kv + 1) * rb:(2 * kv + 2) * rb]
                if kv == 0:
                    o = jnp.where(lo, te, pltpu.roll(to, HEAD_DIM, axis=1))
                else:
                    o = jnp.where(lo, pltpu.roll(te, HEAD_DIM, axis=1), to)
                for j in range(2):
                    c0 = (2 * kv + j) * LANES
                    z = att_ref[b * tl:(b + 1) * tl, ATT_W + 2 * KV_W + c0:ATT_W + 2 * KV_W + c0 + LANES]
                    ya_ref[b, chunk_rows(c), c0:c0 + LANES] = (o[j * tl:(j + 1) * tl] * _silu(z)).astype(BF16)

    def ssm_project():
        hb = hb_ref[...]
        us_ref[:, :SSM_W] = jnp.dot(hb, wus_ref[...], preferred_element_type=F32)
        us_ref[:, SSM_W:] = jnp.dot(hb, wzs_ref[...], preferred_element_type=F32)
        u = us_ref[:, :SSM_W].reshape(nb, tl, SSM_W)
        ut_ref[...] = jnp.swapaxes(u, 0, 1).reshape(m, SSM_W).astype(BF16)

    def ssm_expand(r):
        res = jnp.dot(ut_ref[:, r * LANES:(r + 1) * LANES], bmat_ref[r], preferred_element_type=F32)
        bu_ref[:, r * half:(r + 1) * half] = res[:, :half]
        bu_ref[:, im0 + r * half:im0 + (r + 1) * half] = res[:, half:]

    def ssm_scan(r):
        c0 = r * half
        a_re = jnp.broadcast_to(lam_ref[0:1, c0:c0 + half], (nb, half))
        a_im = jnp.broadcast_to(lam_ref[1:2, c0:c0 + half], (nb, half))
        s_re = st_ref[:, c0:c0 + half]
        s_im = st_ref[:, N_STATE + c0:N_STATE + c0 + half]
        for t in range(tl):
            r0 = t * nb
            n_re = a_re * s_re - a_im * s_im + bu_ref[r0:r0 + nb, c0:c0 + half]
            n_im = a_re * s_im + a_im * s_re + bu_ref[r0:r0 + nb, im0 + c0:im0 + c0 + half]
            bu_ref[r0:r0 + nb, c0:c0 + half] = n_re
            bu_ref[r0:r0 + nb, im0 + c0:im0 + c0 + half] = n_im
            s_re, s_im = n_re, n_im
        st_ref[:, c0:c0 + half] = s_re
        st_ref[:, N_STATE + c0:N_STATE + c0 + half] = s_im

    def ssm_contract(r):
        x_re = bu_ref[:, r * half:(r + 1) * half].astype(BF16)
        x_im = bu_ref[:, im0 + r * half:im0 + (r + 1) * half].astype(BF16)
        yt_ref[:, r * LANES:(r + 1) * LANES] = (jnp.dot(x_re, cre_ref[r], preferred_element_type=F32)
                                               + jnp.dot(x_im, cim_ref[r], preferred_element_type=F32))

    def ssm_output(c):
        y = jnp.swapaxes(yt_ref[...].reshape(tl, nb, SSM_W), 0, 1).reshape(m, SSM_W)
        y = _gelu_tanh(y + dskip_ref[...] * us_ref[:, :SSM_W])
        gl = jnp.dot(y.astype(BF16), wglu_ref[...], preferred_element_type=F32) + bglu_ref[...]
        y = y * _sigmoid(gl)
        ys_ref[:, chunk_rows(c), :] = (y * _silu(us_ref[:, SSM_W:])).reshape(nb, tl, SSM_W).astype(BF16)

    def pool_project():
        hb = hb_ref[...]
        zp_ref[...] = jnp.dot(hb, wzp_ref[...], preferred_element_type=F32)
        pext_ref[:, POOL_HALO:POOL_HALO + tl, :] = jnp.dot(hb, wup_ref[...], preferred_element_type=F32
                                                           ).reshape(nb, tl, POOL_W)

    def pool_mix(c):
        tpos = chunk_index(c) * tl + lax.broadcasted_iota(jnp.int32, (1, tl, 1), 1)
        groups = []
        for gi, w in enumerate(POOL_WINDOWS):
            cs = slice(gi * POOL_GW, (gi + 1) * POOL_GW)
            ext = pext_ref[:, :, cs]
            tok = ext[:, POOL_HALO:, :]
            ssum = ext
            k = 1
            while k < w:
                ssum = ssum + pltpu.roll(ssum, k, axis=1)
                k *= 2
            cnt = jnp.minimum(tpos + 1, w).astype(F32)
            pooled = ssum[:, POOL_HALO:, :] / cnt - tok
            groups.append(jnp.dot(pooled.reshape(m, POOL_GW).astype(BF16), wpool_ref[gi],
                                  preferred_element_type=F32))
        y = jnp.concatenate(groups, axis=1) * pscale_ref[...]
        yp_ref[:, chunk_rows(c), :] = (y * _silu(zp_ref[...])).reshape(nb, tl, POOL_W).astype(BF16)
        pext_ref[:, 0:POOL_HALO, :] = pext_ref[:, tl:tl + POOL_HALO, :]

    P = functools.partial
    for c in range(CHUNKS_PER_STEP):
        schedule = (P(norm, c), att_project, pool_project, att_scores, ssm_project,
                    P(att_softmax, c, 0), P(ssm_expand, 0), P(att_softmax, c, 1), P(ssm_expand, 1),
                    P(ssm_scan, 0), P(ssm_expand, 2), P(att_softmax, c, 2), P(ssm_scan, 1), P(ssm_expand, 3),
                    P(att_softmax, c, 3), P(ssm_contract, 0), P(ssm_scan, 2), P(ssm_contract, 1), P(pool_mix, c),
                    P(ssm_scan, 3), P(att_values, c), P(ssm_contract, 2), P(ssm_contract, 3), P(ssm_output, c))
        for phase in schedule:
            phase()


def _mixers(x, li, mod, norm_g, sinks, wts):
    nb, seq, d = x.shape
    tl = CHUNK
    m = nb * tl
    consts = (wts["bmat"], wts["cre"],
              wts["cim"], wts["lam"], wts["dskip"], wts["w_glu"], wts["b_glu"], wts["w_pool"], wts["pscale"])
    w_in = wts["w_in"]
    proj_cols = ((0, ATT_W), (O_K, 2 * KV_W), (O_ZA, ATT_W), (O_US, SSM_W), (O_ZS, SSM_W), (O_UP, POOL_W),
                 (O_ZP, POOL_W))
    step_rows = CHUNKS_PER_STEP * tl
    out_spec = pl.BlockSpec((nb, step_rows, 3 * SSM_W), lambda i: (0, i, 0))
    out_shape = jax.ShapeDtypeStruct((nb, seq, 3 * SSM_W), BF16)
    return pl.pallas_call(
        functools.partial(_mixer_kernel, nb=nb, li=li),
        grid=(seq // step_rows,),
        in_specs=[pl.BlockSpec(memory_space=pltpu.SMEM),
                  pl.BlockSpec((nb, step_rows, d), lambda i: (0, i, 0)),
                  _layer_spec(mod.shape, li),
                  _layer_spec(norm_g.shape, li)]
                 + [_column_spec(w_in.shape, li, c0, width) for c0, width in proj_cols]
                 + [_layer_spec(c.shape, li) for c in consts],
        out_specs=out_spec,
        out_shape=out_shape,
        scratch_shapes=[pltpu.VMEM((m, d), BF16),
                        pltpu.VMEM((m, 2 * ATT_W + 2 * KV_W), F32),
                        pltpu.VMEM((nb, N_HEADS * tl, NK), F32),
                        pltpu.VMEM((nb, N_HEADS * tl, NK), BF16),
                        pltpu.VMEM((nb, NK, LANES), BF16),
                        pltpu.VMEM((nb, NK, LANES), BF16),
                        pltpu.VMEM((m, 2 * SSM_W), F32),
                        pltpu.VMEM((m, SSM_W), BF16),
                        pltpu.VMEM((m, 2 * N_STATE + LANES), F32),
                        pltpu.VMEM((nb, 2 * N_STATE), F32),
                        pltpu.VMEM((m, SSM_W), F32),
                        pltpu.VMEM((nb, POOL_HALO + tl, POOL_W), F32),
                        pltpu.VMEM((m, POOL_W), F32)]
                       + [pltpu.VMEM((d, width), BF16) for _, width in proj_cols]
                       + [pltpu.VMEM(wts[k].shape[1:], BF16) for k in ("w_glu", "w_pool")],
        compiler_params=pltpu.CompilerParams(dimension_semantics=("arbitrary",), vmem_limit_bytes=VMEM_LIMIT),
        name="mixers",
    )(sinks, x, mod, norm_g, *([w_in] * len(proj_cols)), *consts)


def _merge_kernel(x_ref, mod_ref, ng_ref, y_ref, wg32, wba_ref, wbs_ref, wbp_ref, wout_ref,
                  fg_ref, o_ref, wg_ref, *, final):
    ya_ref, ys_ref, yp_ref = (y_ref.at[:, :, k * SSM_W:(k + 1) * SSM_W] for k in range(3))
    @pl.when((pl.program_id(0) == 0) & (pl.program_id(1) == 0))
    def _cast_weights():
        wg_ref[...] = wg32[0].astype(BF16)

    tm = x_ref.shape[1]
    for r0 in range(0, tm, tm // 2):
        rows = slice(r0, r0 + tm // 2)
        x = x_ref[0, rows, :]
        ms = jnp.mean(x * x, axis=-1, keepdims=True)
        gain = ng_ref[...] * (1.0 + mod_ref[1, 0])
        hb = (x * lax.rsqrt(ms + EPS) * gain + mod_ref[0, 0]).astype(BF16)
        merged = None
        for j, (y_ref, wb_ref) in enumerate(((ya_ref, wba_ref), (ys_ref, wbs_ref), (yp_ref, wbp_ref))):
            g = jnp.dot(hb, wg_ref[:, j * D_MODEL:(j + 1) * D_MODEL], preferred_element_type=F32)
            p = jnp.dot(y_ref[0, rows, :], wb_ref[...], preferred_element_type=F32)
            term = _sigmoid(g) * p
            merged = term if merged is None else merged + term
        out = jnp.dot(merged.astype(BF16), wout_ref[...], preferred_element_type=F32)
        xn = x + mod_ref[2, 0] * out
        if final:
            ms = jnp.mean(xn * xn, axis=-1, keepdims=True)
            xn = xn * lax.rsqrt(ms + EPS) * fg_ref[...]
        o_ref[0, rows, :] = xn


def _merge(x, li, mod, norm_g, y, wts, final_g, final):
    nb, seq, d = x.shape
    tm = MERGE_ROWS
    row = lambda w: pl.BlockSpec((1, tm, w), lambda b, i: (b, i, 0))
    return pl.pallas_call(
        functools.partial(_merge_kernel, final=final),
        grid=(nb, seq // tm),
        in_specs=[row(d),
                  pl.BlockSpec((None, 3, 1, 1, d), lambda b, i: (li, 0, b, 0, 0)),
                  _layer_spec(norm_g.shape, li),
                  row(y.shape[-1]),
                  _column_spec(wts["w_in"].shape, li, O_G, 3 * d), _layer_spec(wts["w_br_att"].shape, li),
                  _layer_spec(wts["w_br_ssm"].shape, li), _layer_spec(wts["w_br_pool"].shape, li),
                  _layer_spec(wts["w_out"].shape, li), _const_spec(final_g.shape)],
        out_specs=row(d),
        out_shape=jax.ShapeDtypeStruct(x.shape, x.dtype),
        scratch_shapes=[pltpu.VMEM((d, 3 * d), BF16)],
        compiler_params=pltpu.CompilerParams(dimension_semantics=("arbitrary", "arbitrary"),
                                             vmem_limit_bytes=VMEM_LIMIT),
        name="merge",
    )(x, mod, norm_g, y, wts["w_in"], wts["w_br_att"], wts["w_br_ssm"], wts["w_br_pool"], wts["w_out"],
      final_g)


def _block_diag(blocks):
    *lead, n, r, c = blocks.shape
    mask = jnp.asarray(np.kron(np.eye(n, dtype=np.float32), np.ones((r, c), np.float32)))
    rows = blocks.reshape(*lead, n * r, c)
    return jnp.tile(rows, (1,) * len(lead) + (1, n)) * mask


def _stacked_weights(w_in, lam_re, lam_im, coef_re, coef_im, ssm_b_re, ssm_b_im, ssm_c_re, ssm_c_im, ssm_d,
                     w_glu, b_glu, w_pool, pool_scale, w_br_att, w_br_ssm, w_br_pool, w_out):
    depth = w_in.shape[0]
    wts = {"w_in": w_in}
    cr = coef_re[..., None]
    ci = coef_im[..., None]
    bb_re = cr * ssm_b_re - ci * ssm_b_im
    bb_im = cr * ssm_b_im + ci * ssm_b_re
    gpb = LANES // SSM_GROUP
    nblk = SSM_GROUPS // gpb
    bre = jnp.swapaxes(bb_re, 2, 3).reshape(depth, nblk, gpb, SSM_GROUP, SSM_STATE)
    bim = jnp.swapaxes(bb_im, 2, 3).reshape(depth, nblk, gpb, SSM_GROUP, SSM_STATE)
    wts["bmat"] = jnp.concatenate([_block_diag(bre), _block_diag(bim)], axis=-1).astype(BF16)
    cre = jnp.swapaxes(ssm_c_re, 2, 3).reshape(depth, nblk, gpb, SSM_STATE, SSM_GROUP)
    cim = jnp.swapaxes(-ssm_c_im, 2, 3).reshape(depth, nblk, gpb, SSM_STATE, SSM_GROUP)
    wts["cre"] = _block_diag(cre).astype(BF16)
    wts["cim"] = _block_diag(cim).astype(BF16)
    wts["lam"] = jnp.stack([lam_re.reshape(depth, N_STATE), lam_im.reshape(depth, N_STATE)], axis=1)
    wts["dskip"] = ssm_d.reshape(depth, 1, SSM_W)
    wts["b_glu"] = b_glu.reshape(depth, 1, SSM_W)
    wts["pscale"] = pool_scale.reshape(depth, 1, POOL_W)
    wts.update(w_glu=w_glu, w_pool=w_pool)
    for name, w in (("w_br_att", w_br_att), ("w_br_ssm", w_br_ssm), ("w_br_pool", w_br_pool), ("w_out", w_out)):
        wts[name] = w.astype(BF16)
    return wts


def kernel(x, c, norm_g, w_ada, b_ada, w_in, attn_sinks, ssm_a_re, ssm_a_im, ssm_log_dt, ssm_b_re, ssm_b_im,
           ssm_c_re, ssm_c_im, ssm_d, w_glu, b_glu, w_pool, pool_scale, w_br_att, w_br_ssm, w_br_pool, w_out,
           final_g):
    nb, seq, d = x.shape
    depth = w_in.shape[0]
    assert d == D_MODEL and nb == SUBLANES and seq % MERGE_ROWS == 0
    mod_all = _modulation(c, w_ada, b_ada)
    lam_re, lam_im, coef_re, coef_im = _ssm_prep(ssm_a_re, ssm_a_im, ssm_log_dt)
    fg = final_g.reshape(1, d)
    mod = jnp.transpose(mod_all.reshape(depth, nb, 3, d), (0, 2, 1, 3)).reshape(depth, 3, nb, 1, d)
    ng = norm_g.reshape(depth, 1, d)
    wts = _stacked_weights(w_in, lam_re, lam_im, coef_re, coef_im, ssm_b_re, ssm_b_im, ssm_c_re, ssm_c_im, ssm_d,
                           w_glu, b_glu, w_pool, pool_scale, w_br_att, w_br_ssm, w_br_pool, w_out)
    for li in range(depth):
        y = _mixers(x, li, mod, ng, attn_sinks, wts)
        x = _merge(x, li, mod, ng, y, wts, fg, final=(li == depth - 1))
    return x
```
